```python
import jax, jax.numpy as jnp
from jax import lax
import numpy as np

D_MODEL = 1024
BATCH = 8
SEQ = 2048
DEPTH = 4
DEC_BATCH = 128
DEC_SEQ = 1
PAST_LEN = 16384
PAGE_SIZE = 128

N_MIXERS = 2
N_POOL_LAYERS = (DEPTH + 1) // 2
N_CONV_LAYERS = DEPTH // 2
N_META = 16
POOL_WINDOWS = (2, 4, 8, 16)
N_POOL_GROUPS = len(POOL_WINDOWS)
POOL_GROUP_DIM = D_MODEL // N_POOL_GROUPS
POOL_STATE = max(POOL_WINDOWS) - 1
CONV_WIDTH = 31
CONV_STATE = CONV_WIDTH - 1
N_GROUPS = 4
EXPERTS_PER_GROUP = 8
N_EXPERTS = N_GROUPS * EXPERTS_PER_GROUP
TOP_K = 2
D_EXPERT = D_MODEL // 2
MOE_BLOCK = 128
ALPHA = (2.0 * DEPTH) ** 0.25
BETA = (8.0 * DEPTH) ** -0.25
LN_EPS = 1e-5

kernel_name = 'hybrid_pool_conformer_hmoe_decoder_step'


def layer_norm(x, g, b):
    xf = x.astype(jnp.float32)
    mu = jnp.mean(xf, axis=-1, keepdims=True)
    var = jnp.mean(jnp.square(xf - mu), axis=-1, keepdims=True)
    y = (xf - mu) * lax.rsqrt(var + LN_EPS) * g.astype(jnp.float32) + b.astype(jnp.float32)
    return y.astype(x.dtype)


def pool_mixer(x, prefix, pos0, w_pool, scale):
    B, L, D = x.shape
    xcat = jnp.concatenate([prefix.astype(x.dtype), x], axis=1)
    c = jnp.cumsum(xcat.astype(jnp.float32), axis=1)
    c = jnp.pad(c, ((0, 0), (1, 0), (0, 0)))
    pos = pos0 + jnp.arange(L)
    end = POOL_STATE + 1
    means = []
    for g, w in enumerate(POOL_WINDOWS):
        lo, hi = g * POOL_GROUP_DIM, (g + 1) * POOL_GROUP_DIM
        s = c[:, end:end + L, lo:hi] - c[:, end - w:end - w + L, lo:hi]
        cnt = jnp.minimum(pos + 1, w).astype(jnp.float32)
        means.append(s / cnt[None, :, None])
    mean = jnp.stack(means, axis=2)
    xg = x.reshape(B, L, N_POOL_GROUPS, POOL_GROUP_DIM).astype(jnp.float32)
    diff = (mean - xg).astype(x.dtype)
    y = jnp.einsum('blgc,gcd->blgd', diff, w_pool).reshape(B, L, D)
    return y * scale, xcat[:, -POOL_STATE:]


def conv_mixer(x, prefix, w_glu, b_glu, w_dw, b_dw, ln_g, ln_b, w_pw, b_pw):
    h = jnp.einsum('bld,de->ble', x, w_glu) + b_glu
    a, gate = jnp.split(h, 2, axis=-1)
    u = a * jax.nn.sigmoid(gate)
    ucat = jnp.concatenate([prefix.astype(u.dtype), u], axis=1)
    v = lax.conv_general_dilated(ucat, w_dw[:, None, :].astype(u.dtype), window_strides=(1,),
                                 padding='VALID', dimension_numbers=('NWC', 'WIO', 'NWC'),
                                 feature_group_count=D_MODEL) + b_dw
    v = jax.nn.silu(layer_norm(v, ln_g, ln_b))
    y = jnp.einsum('bld,de->ble', v, w_pw) + b_pw
    return y, ucat[:, -CONV_STATE:]


def hier_moe(x, w_rg, w_re, w_gate, w_up, w_down):
    T, D = x.shape
    pg = jax.nn.softmax(jnp.einsum('td,dg->tg', x, w_rg).astype(jnp.float32), axis=-1)
    p_grp, g_top = lax.top_k(pg, 1)
    le = jnp.einsum('td,de->te', x, w_re).astype(jnp.float32).reshape(T, N_GROUPS, EXPERTS_PER_GROUP)
    le_g = le[jnp.arange(T), g_top[:, 0]]
    pe = jax.nn.softmax(le_g, axis=-1)
    pe_top, e_top = lax.top_k(pe, TOP_K)
    gates = p_grp * (pe_top / jnp.sum(pe_top, axis=-1, keepdims=True))
    experts = (g_top * EXPERTS_PER_GROUP + e_top).astype(jnp.int32)

    A = T * TOP_K
    flat_e = experts.reshape(-1)
    order = jnp.argsort(flat_e)
    sorted_e = flat_e[order]
    counts = jnp.bincount(flat_e, length=N_EXPERTS)
    padded = ((counts + MOE_BLOCK - 1) // MOE_BLOCK) * MOE_BLOCK
    padded_ends = jnp.cumsum(padded)
    padded_starts = padded_ends - padded
    starts = jnp.cumsum(counts) - counts
    dest = padded_starts[sorted_e] + (jnp.arange(A) - starts[sorted_e])
    n_blocks = -(-A // MOE_BLOCK) + N_EXPERTS
    n_slots = n_blocks * MOE_BLOCK
    token_of_slot = jnp.full((n_slots,), T, jnp.int32).at[dest].set((order // TOP_K).astype(jnp.int32))
    block_expert = jnp.minimum(jnp.searchsorted(padded_ends, jnp.arange(n_blocks) * MOE_BLOCK, side='right'),
                               N_EXPERTS - 1).astype(jnp.int32)
    x_pad = jnp.concatenate([x, jnp.zeros((1, D), x.dtype)], axis=0)
    xs = x_pad[token_of_slot].reshape(n_blocks, MOE_BLOCK, D)

    def expert_block(args):
        xb, e = args
        hb = jax.nn.silu(xb @ w_gate[e]) * (xb @ w_up[e])
        return hb @ w_down[e]

    ys = lax.map(expert_block, (xs, block_expert)).reshape(n_slots, D)
    slot_of_assign = jnp.zeros((A,), jnp.int32).at[order].set(dest.astype(jnp.int32))
    y_assign = ys[slot_of_assign].reshape(T, TOP_K, D)
    return jnp.einsum('tkd,tk->td', y_assign, gates.astype(y_assign.dtype))


def run_trunk(h, pool_state, conv_state, pos0, w_pool, pool_scale, w_glu, b_glu, w_dw, b_dw,
              conv_ln_g, conv_ln_b, w_pw, b_pw, ln_mix_g, ln_mix_b, ln_ffn_g, ln_ffn_b,
              w_router_group, w_router_expert, w_gate, w_up, w_down):
    new_pool, new_conv = [], []
    for i in range(DEPTH):
        j = i // N_MIXERS
        if i % N_MIXERS == 0:
            m, st = pool_mixer(h, pool_state[j], pos0, w_pool[j], pool_scale[j])
            new_pool.append(st)
        else:
            m, st = conv_mixer(h, conv_state[j], w_glu[j], b_glu[j], w_dw[j], b_dw[j],
                               conv_ln_g[j], conv_ln_b[j], w_pw[j], b_pw[j])
            new_conv.append(st)
        h = layer_norm(ALPHA * h + m, ln_mix_g[i], ln_mix_b[i])
        B, L, D = h.shape
        f = hier_moe(h.reshape(B * L, D), w_router_group[i], w_router_expert[i],
                     w_gate[i], w_up[i], w_down[i]).reshape(B, L, D)
        h = layer_norm(ALPHA * h + f, ln_ffn_g[i], ln_ffn_b[i])
    return h, jnp.stack(new_pool, axis=0), jnp.stack(new_conv, axis=0)


def setup_inputs(seed: int = 0) -> dict:
    key = jax.random.key(seed)
    ks = jax.random.split(key, 26)
    n = lambda k, s: jax.random.normal(k, s, jnp.float32)
    D, F = D_MODEL, D_EXPERT
    return {
        'x_prompt': n(ks[0], (BATCH, SEQ, D)),
        'x_sample': n(ks[1], (DEC_BATCH, DEC_SEQ, D)),
        'state_pool': n(ks[2], (N_POOL_LAYERS, DEC_BATCH, POOL_STATE, D)),
        'state_conv': 0.5 * n(ks[3], (N_CONV_LAYERS, DEC_BATCH, CONV_STATE, D)),
        'meta': n(ks[4], (N_META, D)),
        'w_pool': n(ks[5], (N_POOL_LAYERS, N_POOL_GROUPS, POOL_GROUP_DIM, POOL_GROUP_DIM)) * (POOL_GROUP_DIM ** -0.5) * BETA,
        'pool_scale': 1.0 + 0.1 * n(ks[6], (N_POOL_LAYERS, D)),
        'w_glu': n(ks[7], (N_CONV_LAYERS, D, 2 * D)) * D ** -0.5,
        'b_glu': 0.02 * n(ks[8], (N_CONV_LAYERS, 2 * D)),
        'w_dw': n(ks[9], (N_CONV_LAYERS, CONV_WIDTH, D)) * CONV_WIDTH ** -0.5,
        'b_dw': 0.02 * n(ks[10], (N_CONV_LAYERS, D)),
        'conv_ln_g': 1.0 + 0.1 * n(ks[11], (N_CONV_LAYERS, D)),
        'conv_ln_b': 0.02 * n(ks[12], (N_CONV_LAYERS, D)),
        'w_pw': n(ks[13], (N_CONV_LAYERS, D, D)) * (D ** -0.5) * BETA,
        'b_pw': 0.02 * n(ks[14], (N_CONV_LAYERS, D)),
        'ln_mix_g': 1.0 + 0.1 * n(ks[15], (DEPTH, D)),
        'ln_mix_b': 0.02 * n(ks[16], (DEPTH, D)),
        'ln_ffn_g': 1.0 + 0.1 * n(ks[17], (DEPTH, D)),
        'ln_ffn_b': 0.02 * n(ks[18], (DEPTH, D)),
        'w_router_group': n(ks[19], (DEPTH, D, N_GROUPS)) * D ** -0.5,
        'w_router_expert': n(ks[20], (DEPTH, D, N_EXPERTS)) * D ** -0.5,
        'w_gate': n(ks[21], (DEPTH, N_EXPERTS, D, F)) * D ** -0.5,
        'w_up': n(ks[22], (DEPTH, N_EXPERTS, D, F)) * D ** -0.5,
        'w_down': n(ks[23], (DEPTH, N_EXPERTS, F, D)) * (F ** -0.5) * BETA,
    }


def reference(x_prompt, x_sample, state_pool, state_conv, meta, w_pool, pool_scale, w_glu, b_glu,
              w_dw, b_dw, conv_ln_g, conv_ln_b, w_pw, b_pw, ln_mix_g, ln_mix_b, ln_ffn_g, ln_ffn_b,
              w_router_group, w_router_expert, w_gate, w_up, w_down):
    B = x_prompt.shape[0]
    meta_b = jnp.broadcast_to(meta.astype(x_prompt.dtype)[None], (B, N_META, D_MODEL))
    h_p = jnp.concatenate([meta_b, x_prompt], axis=1)
    zero_pool = jnp.zeros((N_POOL_LAYERS, B, POOL_STATE, D_MODEL), x_prompt.dtype)
    zero_conv = jnp.zeros((N_CONV_LAYERS, B, CONV_STATE, D_MODEL), x_prompt.dtype)
    h_p, pool_p, conv_p = run_trunk(h_p, zero_pool, zero_conv, 0, w_pool, pool_scale, w_glu, b_glu,
                                    w_dw, b_dw, conv_ln_g, conv_ln_b, w_pw, b_pw, ln_mix_g, ln_mix_b,
                                    ln_ffn_g, ln_ffn_b, w_router_group, w_router_expert,
                                    w_gate, w_up, w_down)
    y_prompt = h_p[:, N_META:]
    h_s, pool_s, conv_s = run_trunk(x_sample, state_pool, state_conv, PAST_LEN, w_pool, pool_scale,
                                    w_glu, b_glu, w_dw, b_dw, conv_ln_g, conv_ln_b, w_pw, b_pw,
                                    ln_mix_g, ln_mix_b, ln_ffn_g, ln_ffn_b, w_router_group,
                                    w_router_expert, w_gate, w_up, w_down)
    return (y_prompt, h_s, pool_p, pool_s, conv_p, conv_s)
```

```python
import functools

import jax
import jax.numpy as jnp
from jax import lax
from jax.experimental import pallas as pl
from jax.experimental.pallas import tpu as pltpu

D = 1024
BATCH = 8
SEQ = 2048
DEPTH = 4
DEC_BATCH = 128
N_META = 16
POOL_WINDOWS = (2, 4, 8, 16)
POOL_GROUP_DIM = D // len(POOL_WINDOWS)
POOL_STATE = max(POOL_WINDOWS) - 1
CONV_WIDTH = 31
CONV_STATE = CONV_WIDTH - 1
N_GROUPS = 4
EXPERTS_PER_GROUP = 8
N_EXPERTS = N_GROUPS * EXPERTS_PER_GROUP
TOP_K = 2
D_EXPERT = D // 2
ALPHA = (2.0 * DEPTH) ** 0.25
LN_EPS = 1e-5

LANES = 128
L_SEQ = SEQ + N_META
T_PROMPT = BATCH * L_SEQ
T_REAL = T_PROMPT + DEC_BATCH
TB = 688
N_CHUNK = L_SEQ // TB
N_STEP = BATCH * N_CHUNK + 1
NTOK = N_STEP * TB
POOL_CARRY = 16
CONV_CARRY = 32
BM = 256
N_ASSIGN = T_REAL * TOP_K
NB = N_ASSIGN // BM + N_EXPERTS
N_SLOT = NB * BM
PAD_ROW = T_REAL
VMEM_LIMIT = 56 * 1024 * 1024
NEG = -1e30

assert L_SEQ % TB == 0 and TB % 8 == 0 and N_ASSIGN % BM == 0 and DEC_BATCH <= TB


def _layer_norm(x, g, b):
    mu = jnp.mean(x, axis=-1, keepdims=True)
    xc = x - mu
    var = jnp.mean(xc * xc, axis=-1, keepdims=True)
    return xc * lax.rsqrt(var + LN_EPS) * g + b


def _residual_route(x, m, g_ref, b_ref, wr_ref, h_ref, r_ref):
    h = _layer_norm(ALPHA * x + m, g_ref[...], b_ref[...])
    h_ref[...] = h
    logits = jnp.dot(h, wr_ref[...], precision=lax.Precision.HIGHEST,
                     preferred_element_type=jnp.float32)
    lane = lax.broadcasted_iota(jnp.int32, logits.shape, 1)
    lanef = lane.astype(jnp.float32)
    big = jnp.float32(1e9)
    lg = jnp.where(lane < N_GROUPS, logits, NEG)
    mg = jnp.max(lg, axis=1, keepdims=True)
    gidx = jnp.min(jnp.where(lg == mg, lanef, big), axis=1, keepdims=True)
    p_grp = 1.0 / jnp.sum(jnp.where(lane < N_GROUPS, jnp.exp(lg - mg), 0.0), axis=1, keepdims=True)
    lo = N_GROUPS + EXPERTS_PER_GROUP * gidx
    le = jnp.where(lanef >= lo, jnp.where(lanef < lo + EXPERTS_PER_GROUP, logits, NEG), NEG)
    m1 = jnp.max(le, axis=1, keepdims=True)
    i1 = jnp.min(jnp.where(le == m1, lanef, big), axis=1, keepdims=True)
    le2 = jnp.where(lanef == i1, NEG, le)
    m2 = jnp.max(le2, axis=1, keepdims=True)
    i2 = jnp.min(jnp.where(le2 == m2, lanef, big), axis=1, keepdims=True)
    ratio = jnp.exp(m2 - m1)
    g1 = p_grp / (1.0 + ratio)
    g2 = g1 * ratio
    r_ref[...] = jnp.where(lane == 0, g1,
                           jnp.where(lane == 1, g2,
                                     jnp.where(lane == 2, i1 - N_GROUPS,
                                               jnp.where(lane == 3, i2 - N_GROUPS, 0.0))))


def _presum_kernel(st_ref, coef_ref, o_ref, *, n_rows):
    acc = coef_ref[0:1, :] * st_ref[:, 0:D]
    for k in range(1, n_rows):
        acc = acc + coef_ref[k:k + 1, :] * st_ref[:, k * D:(k + 1) * D]
    o_ref[...] = acc


def _state_presum(state2d, coef):
    n, kd = state2d.shape
    k = kd // D
    rows = 32
    return pl.pallas_call(
        functools.partial(_presum_kernel, n_rows=k),
        grid=(n // rows,),
        in_specs=[pl.BlockSpec((rows, kd), lambda i: (i, 0)),
                  pl.BlockSpec((k, D), lambda i: (0, 0))],
        out_specs=pl.BlockSpec((rows, D), lambda i: (i, 0)),
        out_shape=jax.ShapeDtypeStruct((n, D), jnp.float32),
        compiler_params=pltpu.CompilerParams(dimension_semantics=("arbitrary",),
                                             vmem_limit_bytes=VMEM_LIMIT),
        name="state_presum",
    )(state2d, coef)


def _pool_kernel(x_ref, ps_ref, wp_ref, sc_ref, g_ref, b_ref, wr_ref, h_ref, r_ref, ext_ref, m_ref):
    s = pl.program_id(0)
    chunk = s % N_CHUNK
    is_sample = s == N_STEP - 1

    @pl.when(chunk == 0)
    def _():
        ext_ref[0:POOL_CARRY, :] = jnp.zeros((POOL_CARRY, D), jnp.float32)

    @pl.when(jnp.logical_not(is_sample))
    def _():
        x = x_ref[...]
        ext_ref[POOL_CARRY:POOL_CARRY + TB, :] = x
        pos = chunk * TB + lax.broadcasted_iota(jnp.int32, (TB, 1), 0)
        for g, w in enumerate(POOL_WINDOWS):
            lo, hi = g * POOL_GROUP_DIM, (g + 1) * POOL_GROUP_DIM
            xg = x[:, lo:hi]
            acc = xg
            for k in range(1, w):
                acc = acc + ext_ref[POOL_CARRY - k:POOL_CARRY - k + TB, lo:hi]
            cnt = jnp.minimum(pos + 1, w).astype(jnp.float32)
            diff = acc / cnt - xg
            y = jnp.dot(diff.astype(jnp.bfloat16), wp_ref[g], preferred_element_type=jnp.float32)
            m_ref[:, lo:hi] = y * sc_ref[:, lo:hi]
        ext_ref[0:POOL_CARRY, :] = x[TB - POOL_CARRY:TB, :]

    @pl.when(is_sample)
    def _():
        x = x_ref[0:DEC_BATCH, :]
        for g, w in enumerate(POOL_WINDOWS):
            lo, hi = g * POOL_GROUP_DIM, (g + 1) * POOL_GROUP_DIM
            xg = x[:, lo:hi]
            diff = (ps_ref[:, lo:hi] + xg) / float(w) - xg
            y = jnp.dot(diff.astype(jnp.bfloat16), wp_ref[g], preferred_element_type=jnp.float32)
            m_ref[0:DEC_BATCH, lo:hi] = y * sc_ref[:, lo:hi]
        m_ref[DEC_BATCH:TB, :] = jnp.zeros((TB - DEC_BATCH, D), jnp.float32)

    _residual_route(x_ref[...], m_ref[...], g_ref, b_ref, wr_ref, h_ref, r_ref)


def _const_spec(shape):
    nd = len(shape)
    return pl.BlockSpec(shape, lambda s: (0,) * nd)


def _pool_layer(h_all, ps, wp_bf, scale, ln_g, ln_b, wr):
    return pl.pallas_call(
        _pool_kernel,
        grid=(N_STEP,),
        in_specs=[pl.BlockSpec((TB, D), lambda s: (s, 0)),
                  _const_spec((DEC_BATCH, D)),
                  _const_spec((len(POOL_WINDOWS), POOL_GROUP_DIM, POOL_GROUP_DIM)),
                  _const_spec((1, D)), _const_spec((1, D)), _const_spec((1, D)),
                  _const_spec((D, LANES))],
        out_specs=[pl.BlockSpec((TB, D), lambda s: (s, 0)),
                   pl.BlockSpec((TB, LANES), lambda s: (s, 0))],
        out_shape=[jax.ShapeDtypeStruct((NTOK, D), jnp.float32),
                   jax.ShapeDtypeStruct((NTOK, LANES), jnp.float32)],
        scratch_shapes=[pltpu.VMEM((POOL_CARRY + TB, D), jnp.float32),
                        pltpu.VMEM((TB, D), jnp.float32)],
        compiler_params=pltpu.CompilerParams(dimension_semantics=("arbitrary",),
                                             vmem_limit_bytes=VMEM_LIMIT),
        name="pool_mixer",
    )(h_all, ps, wp_bf, scale, ln_g, ln_b, wr)


TBC = 344
N_CHUNK_C = L_SEQ // TBC
N_STEP_C = NTOK // TBC
SAMPLE_STEP_C = BATCH * N_CHUNK_C
CONV_RB = 128
CONV_CB = 128
SUBLANES = 8

assert L_SEQ % TBC == 0 and NTOK % TBC == 0 and TBC % SUBLANES == 0 and TBC >= CONV_RB


def _depthwise_conv(ext_ref, wdw_ref, v_ref):
    base = CONV_CARRY - CONV_STATE
    n_rb = -(-TBC // CONV_RB)
    for cb in range(D // CONV_CB):
        lo, hi = cb * CONV_CB, (cb + 1) * CONV_CB

        def body(i, carry, lo=lo, hi=hi):
            r0 = pl.multiple_of(jnp.minimum(i * CONV_RB, TBC - CONV_RB), SUBLANES)
            sub = ext_ref.at[pl.ds(r0, CONV_RB + CONV_CARRY), :]
            v = None
            for r in range(SUBLANES):
                qs = [q for q in range((base + CONV_WIDTH) // SUBLANES + 1)
                      if 0 <= SUBLANES * q + r - base < CONV_WIDTH]
                z0 = SUBLANES * qs[0] + r
                z = sub[z0:SUBLANES * qs[-1] + r + CONV_RB, lo:hi]
                p = None
                for q in qs:
                    k = SUBLANES * q + r - base
                    off = SUBLANES * (q - qs[0])
                    term = wdw_ref[k:k + 1, lo:hi] * z[off:off + CONV_RB]
                    p = term if p is None else p + term
                v = p if v is None else v + p
            v_ref[pl.ds(r0, CONV_RB), lo:hi] = v
            return carry
        lax.fori_loop(0, n_rb, body, 0)


def _conv_kernel(x_ref, vs_ref, wglu_ref, bglu_ref, wdw_ref, bdw_ref, lg_ref, lb_ref, wpw_ref, bpw_ref,
                 g_ref, b_ref, wr_ref, h_ref, r_ref, ust_ref, us_ref, ext_ref, v_ref):
    s = pl.program_id(0)
    chunk = s % N_CHUNK_C
    is_sample = s == SAMPLE_STEP_C

    x = x_ref[...]
    hh = jnp.dot(x.astype(jnp.bfloat16), wglu_ref[...], preferred_element_type=jnp.float32) + bglu_ref[...]
    u = hh[:, :D] * jax.nn.sigmoid(hh[:, D:])

    @pl.when(chunk == 0)
    def _():
        ext_ref[0:CONV_CARRY, :] = jnp.zeros((CONV_CARRY, D), jnp.float32)

    @pl.when(jnp.logical_not(is_sample))
    def _():
        ext_ref[CONV_CARRY:CONV_CARRY + TBC, :] = u
        _depthwise_conv(ext_ref, wdw_ref, v_ref)
        ext_ref[0:CONV_CARRY, :] = ext_ref[TBC:TBC + CONV_CARRY, :]

        @pl.when(chunk == N_CHUNK_C - 1)
        def _():
            ust_ref[0] = ext_ref[CONV_CARRY + TBC - CONV_STATE:CONV_CARRY + TBC, :]

    @pl.when(is_sample)
    def _():
        us = u[0:DEC_BATCH, :]
        us_ref[...] = us
        v_ref[0:DEC_BATCH, :] = vs_ref[...] + wdw_ref[CONV_WIDTH - 1:CONV_WIDTH, :] * us
        v_ref[DEC_BATCH:TBC, :] = jnp.zeros((TBC - DEC_BATCH, D), jnp.float32)

    v = v_ref[...] + bdw_ref[...]
    v = _layer_norm(v, lg_ref[...], lb_ref[...])
    v = v * jax.nn.sigmoid(v)
    m = jnp.dot(v.astype(jnp.bfloat16), wpw_ref[...], preferred_element_type=jnp.float32) + bpw_ref[...]
    _residual_route(x, m, g_ref, b_ref, wr_ref, h_ref, r_ref)


def _conv_layer(h_all, vs, wglu_bf, bglu, wdw, bdw, cln_g, cln_b, wpw_bf, bpw, ln_g, ln_b, wr):
    return pl.pallas_call(
        _conv_kernel,
        grid=(N_STEP_C,),
        in_specs=[pl.BlockSpec((TBC, D), lambda s: (s, 0)),
                  _const_spec((DEC_BATCH, D)),
                  _const_spec((D, 2 * D)), _const_spec((1, 2 * D)),
                  _const_spec((CONV_WIDTH, D)), _const_spec((1, D)),
                  _const_spec((1, D)), _const_spec((1, D)),
                  _const_spec((D, D)), _const_spec((1, D)),
                  _const_spec((1, D)), _const_spec((1, D)),
                  _const_spec((D, LANES))],
        out_specs=[pl.BlockSpec((TBC, D), lambda s: (s, 0)),
                   pl.BlockSpec((TBC, LANES), lambda s: (s, 0)),
                   pl.BlockSpec((1, CONV_STATE, D),
                                lambda s: (jnp.minimum(s // N_CHUNK_C, BATCH - 1), 0, 0)),
                   _const_spec((DEC_BATCH, D))],
        out_shape=[jax.ShapeDtypeStruct((NTOK, D), jnp.float32),
                   jax.ShapeDtypeStruct((NTOK, LANES), jnp.float32),
                   jax.ShapeDtypeStruct((BATCH, CONV_STATE, D), jnp.float32),
                   jax.ShapeDtypeStruct((DEC_BATCH, D), jnp.float32)],
        scratch_shapes=[pltpu.VMEM((CONV_CARRY + TBC, D), jnp.float32),
                        pltpu.VMEM((TBC, D), jnp.float32)],
        compiler_params=pltpu.CompilerParams(dimension_semantics=("arbitrary",),
                                             vmem_limit_bytes=VMEM_LIMIT),
        name="conv_mixer",
    )(h_all, vs, wglu_bf, bglu, wdw, bdw, cln_g, cln_b, wpw_bf, bpw, ln_g, ln_b, wr)


def _row_gather_start(src_hbm, dst, sem, idx_ref, base, n_rows):
    def body(r, carry):
        tok = idx_ref[base + r]
        pltpu.make_async_copy(src_hbm.at[pl.ds(tok, 1), :], dst.at[pl.ds(r, 1), :], sem).start()
        return carry
    lax.fori_loop(0, n_rows, body, 0, unroll=8)


def _row_gather_wait(src_hbm, dst, sem, n_rows):
    pltpu.make_async_copy(src_hbm.at[pl.ds(0, n_rows), :], dst, sem).wait()


def _expert_kernel(be_ref, nused_ref, tos_ref, x_hbm, wg_ref, wu_ref, wd_ref, ys_ref,
                   xbuf, wgb, wub, wdb, sem):
    i = pl.program_id(0)
    nused = nused_ref[0]

    @pl.when(i == 0)
    def _():
        _row_gather_start(x_hbm, xbuf.at[0], sem.at[0], tos_ref, 0, BM)

    @pl.when(i + 1 < nused)
    def _():
        nxt = (i + 1) % 2
        _row_gather_start(x_hbm, xbuf.at[nxt], sem.at[nxt], tos_ref, (i + 1) * BM, BM)

    @pl.when(i < nused)
    def _():
        slot = i % 2
        _row_gather_wait(x_hbm, xbuf.at[slot], sem.at[slot], BM)

        @pl.when(jnp.logical_or(i == 0, be_ref[i] != be_ref[jnp.maximum(i - 1, 0)]))
        def _():
            wgb[...] = wg_ref[0].astype(jnp.bfloat16)
            wub[...] = wu_ref[0].astype(jnp.bfloat16)
            wdb[...] = wd_ref[0].astype(jnp.bfloat16)

        xb = xbuf[slot].astype(jnp.bfloat16)
        gate = jnp.dot(xb, wgb[...], preferred_element_type=jnp.float32)
        up = jnp.dot(xb, wub[...], preferred_element_type=jnp.float32)
        hid = gate * jax.nn.sigmoid(gate) * up
        ys_ref[...] = jnp.dot(hid.astype(jnp.bfloat16), wdb[...], preferred_element_type=jnp.float32)

    @pl.when(i >= nused)
    def _():
        ys_ref[...] = jnp.zeros((BM, D), jnp.float32)


def _expert_layer(block_expert, nused, token_of_slot, h_all, w_gate, w_up, w_down):
    grid_spec = pltpu.PrefetchScalarGridSpec(
        num_scalar_prefetch=3,
        grid=(NB,),
        in_specs=[pl.BlockSpec(memory_space=pl.ANY),
                  pl.BlockSpec((1, D, D_EXPERT), lambda i, be, nu, tos: (be[i], 0, 0)),
                  pl.BlockSpec((1, D, D_EXPERT), lambda i, be, nu, tos: (be[i], 0, 0)),
                  pl.BlockSpec((1, D_EXPERT, D), lambda i, be, nu, tos: (be[i], 0, 0))],
        out_specs=pl.BlockSpec((BM, D), lambda i, be, nu, tos: (i, 0)),
        scratch_shapes=[pltpu.VMEM((2, BM, D), jnp.float32),
                        pltpu.VMEM((D, D_EXPERT), jnp.bfloat16),
                        pltpu.VMEM((D, D_EXPERT), jnp.bfloat16),
                        pltpu.VMEM((D_EXPERT, D), jnp.bfloat16),
                        pltpu.SemaphoreType.DMA((2,))],
    )
    return pl.pallas_call(
        _expert_kernel,
        grid_spec=grid_spec,
        out_shape=jax.ShapeDtypeStruct((N_SLOT, D), jnp.float32),
        compiler_params=pltpu.CompilerParams(dimension_semantics=("arbitrary",),
                                             vmem_limit_bytes=VMEM_LIMIT),
        name="expert_mlp",
    )(block_expert, nused, token_of_slot, h_all, w_gate, w_up, w_down)


def _combine_kernel(s0_ref, s1_ref, h_ref, r_ref, ys_hbm, g_ref, b_ref, o_ref, ybuf, sem):
    i = pl.program_id(0)
    n = pl.num_programs(0)

    def start(step, slot):
        _row_gather_start(ys_hbm, ybuf.at[slot, 0], sem.at[slot], s0_ref, step * TB, TB)
        _row_gather_start(ys_hbm, ybuf.at[slot, 1], sem.at[slot], s1_ref, step * TB, TB)

    @pl.when(i == 0)
    def _():
        start(0, 0)

    @pl.when(i + 1 < n)
    def _():
        start(i + 1, (i + 1) % 2)

    slot = i % 2
    _row_gather_wait(ys_hbm, ybuf.at[slot, 0], sem.at[slot], TB)
    _row_gather_wait(ys_hbm, ybuf.at[slot, 1], sem.at[slot], TB)
    r = r_ref[...]
    f = r[:, 0:1] * ybuf[slot, 0] + r[:, 1:2] * ybuf[slot, 1]
    o_ref[...] = _layer_norm(ALPHA * h_ref[...] + f, g_ref[...], b_ref[...])


def _combine_layer(slot0, slot1, h_all, r_all, ys, ln_g, ln_b):
    grid_spec = pltpu.PrefetchScalarGridSpec(
        num_scalar_prefetch=2,
        grid=(N_STEP,),
        in_specs=[pl.BlockSpec((TB, D), lambda i, a, b: (i, 0)),
                  pl.BlockSpec((TB, LANES), lambda i, a, b: (i, 0)),
                  pl.BlockSpec(memory_space=pl.ANY),
                  pl.BlockSpec((1, D), lambda i, a, b: (0, 0)),
                  pl.BlockSpec((1, D), lambda i, a, b: (0, 0))],
        out_specs=pl.BlockSpec((TB, D), lambda i, a, b: (i, 0)),
        scratch_shapes=[pltpu.VMEM((2, TOP_K, TB, D), jnp.float32),
                        pltpu.SemaphoreType.DMA((2,))],
    )
    return pl.pallas_call(
        _combine_kernel,
        grid_spec=grid_spec,
        out_shape=jax.ShapeDtypeStruct((NTOK, D), jnp.float32),
        compiler_params=pltpu.CompilerParams(dimension_semantics=("arbitrary",),
                                             vmem_limit_bytes=VMEM_LIMIT),
        name="moe_combine",
    )(slot0, slot1, h_all, r_all, ys, ln_g, ln_b)


def _routing_metadata(r_all):
    experts = r_all[:T_REAL, 2:4].astype(jnp.int32)
    flat_e = experts.reshape(-1)
    order = jnp.argsort(flat_e, stable=True).astype(jnp.int32)
    onehot = (flat_e[:, None] == jnp.arange(N_EXPERTS, dtype=jnp.int32)[None, :]).astype(jnp.int32)
    csum = jnp.cumsum(onehot, axis=0)
    counts = csum[-1]
    rank = jnp.sum(onehot * csum, axis=1) - 1
    padded = ((counts + BM - 1) // BM) * BM
    padded_ends = jnp.cumsum(padded)
    padded_starts = padded_ends - padded
    starts = jnp.cumsum(counts) - counts
    slot_of_assign = (padded_starts[flat_e] + rank).astype(jnp.int32)
    nused = (padded_ends[-1] // BM).astype(jnp.int32)
    blk = jnp.arange(NB, dtype=jnp.int32)
    block_expert = jnp.minimum(jnp.searchsorted(padded_ends, blk * BM, side='right'),
                               N_EXPERTS - 1).astype(jnp.int32)
    last_expert = block_expert[jnp.maximum(nused - 1, 0)]
    block_expert = jnp.where(blk < nused, block_expert, last_expert)
    slot = jnp.arange(N_SLOT, dtype=jnp.int32)
    e_s = block_expert[slot // BM]
    off = slot - padded_starts[e_s]
    valid = jnp.logical_and(off < counts[e_s], slot // BM < nused)
    pos = jnp.clip(starts[e_s] + off, 0, N_ASSIGN - 1)
    token_of_slot = jnp.where(valid, order[pos] // TOP_K, PAD_ROW).astype(jnp.int32)
    sa = slot_of_assign.reshape(T_REAL, TOP_K)
    pad = jnp.zeros((NTOK - T_REAL,), jnp.int32)
    slot0 = jnp.concatenate([sa[:, 0], pad])
    slot1 = jnp.concatenate([sa[:, 1], pad])
    return block_expert, nused.reshape(1), token_of_slot, slot0, slot1


def kernel(x_prompt, x_sample, state_pool, state_conv, meta, w_pool, pool_scale, w_glu, b_glu, w_dw, b_dw, conv_ln_g, conv_ln_b, w_pw, b_pw, ln_mix_g, ln_mix_b, ln_ffn_g, ln_ffn_b, w_router_group, w_router_expert, w_gate, w_up, w_down):
    f32 = jnp.float32
    meta_b = jnp.broadcast_to(meta.astype(f32)[None], (BATCH, N_META, D))
    h_all = jnp.concatenate([
        jnp.concatenate([meta_b, x_prompt], axis=1).reshape(T_PROMPT, D),
        x_sample.reshape(DEC_BATCH, D),
        jnp.zeros((NTOK - T_REAL, D), f32)], axis=0)

    k_idx = jnp.arange(POOL_STATE)[:, None]
    win = jnp.repeat(jnp.asarray(POOL_WINDOWS), POOL_GROUP_DIM)[None, :]
    pool_coef = (k_idx >= (POOL_STATE + 1 - win)).astype(f32)

    new_pool_p, new_pool_s, new_conv_p, new_conv_s = [], [], [], []
    for i in range(DEPTH):
        j = i // 2
        wr = jnp.concatenate([w_router_group[i], w_router_expert[i],
                              jnp.zeros((D, LANES - N_GROUPS - N_EXPERTS), f32)], axis=1)
        ln_g, ln_b = ln_mix_g[i].reshape(1, D), ln_mix_b[i].reshape(1, D)
        x_s = h_all[T_PROMPT:T_REAL]
        if i % 2 == 0:
            new_pool_p.append(h_all[:T_PROMPT].reshape(BATCH, L_SEQ, D)[:, L_SEQ - POOL_STATE:])
            new_pool_s.append(jnp.concatenate([state_pool[j][:, 1:], x_s[:, None, :]], axis=1))
            ps = _state_presum(state_pool[j].reshape(DEC_BATCH, POOL_STATE * D), pool_coef)
            h1, r_all = _pool_layer(h_all, ps, w_pool[j].astype(jnp.bfloat16),
                                    pool_scale[j].reshape(1, D), ln_g, ln_b, wr)
        else:
            vs = _state_presum(state_conv[j].reshape(DEC_BATCH, CONV_STATE * D), w_dw[j][:CONV_STATE])
            h1, r_all, ust, u_s = _conv_layer(
                h_all, vs, w_glu[j].astype(jnp.bfloat16), b_glu[j].reshape(1, 2 * D), w_dw[j],
                b_dw[j].reshape(1, D), conv_ln_g[j].reshape(1, D), conv_ln_b[j].reshape(1, D),
                w_pw[j].astype(jnp.bfloat16), b_pw[j].reshape(1, D), ln_g, ln_b, wr)
            new_conv_p.append(ust)
            new_conv_s.append(jnp.concatenate([state_conv[j][:, 1:], u_s[:, None, :]], axis=1))
        block_expert, nused, token_of_slot, slot0, slot1 = _routing_metadata(r_all)
        ys = _expert_layer(block_expert, nused, token_of_slot, h1, w_gate[i], w_up[i], w_down[i])
        h_all = _combine_layer(slot0, slot1, h1, r_all, ys,
                               ln_ffn_g[i].reshape(1, D), ln_ffn_b[i].reshape(1, D))

    y_prompt = h_all[:T_PROMPT].reshape(BATCH, L_SEQ, D)[:, N_META:]
    y_sample = h_all[T_PROMPT:T_REAL].reshape(DEC_BATCH, 1, D)
    return (y_prompt, y_sample, jnp.stack(new_pool_p), jnp.stack(new_pool_s),
            jnp.stack(new_conv_p), jnp.stack(new_conv_s))
```

```python
import functools

import jax
import jax.numpy as jnp
from jax import lax
from jax.experimental import pallas as pl
from jax.experimental.pallas import tpu as pltpu

D = 1024
BATCH = 8
SEQ = 2048
DEPTH = 4
DEC_BATCH = 128
N_META = 16
POOL_WINDOWS = (2, 4, 8, 16)
POOL_GROUP_DIM = D // len(POOL_WINDOWS)
POOL_STATE = max(POOL_WINDOWS) - 1
CONV_WIDTH = 31
CONV_STATE = CONV_WIDTH - 1
N_GROUPS = 4
EXPERTS_PER_GROUP = 8
N_EXPERTS = N_GROUPS * EXPERTS_PER_GROUP
TOP_K = 2
D_EXPERT = D // 2
ALPHA = (2.0 * DEPTH) ** 0.25
LN_EPS = 1e-5

LANES = 128
SUBLANES = 8
N_HEAD = BATCH * N_META
TB = 256
NTOK = N_HEAD + DEC_BATCH + BATCH * SEQ
N_STEP = NTOK // TB
CHUNKS = SEQ // TB
POOL_CARRY = 16
CONV_CARRY = 32
CONV_RB = 128
CONV_CB = 128
BM = 256
N_ASSIGN = NTOK * TOP_K
NB = N_ASSIGN // BM + N_EXPERTS
N_SLOT = NB * BM
VMEM_LIMIT = 48 * 1024 * 1024
NEG = -1e30

assert D == SUBLANES * LANES and N_HEAD + DEC_BATCH == TB and SEQ % TB == 0
assert N_ASSIGN % BM == 0 and TB % CONV_RB == 0 and N_META == POOL_CARRY and CONV_STATE <= TB


def _tile_load(ref, n):
    return jnp.concatenate([ref[pl.ds(s, n, stride=SUBLANES), :] for s in range(SUBLANES)], axis=1)


def _tile_store(ref, val, n):
    for s in range(SUBLANES):
        ref[pl.ds(s, n, stride=SUBLANES), :] = val[:, s * LANES:(s + 1) * LANES]


def _layer_norm(x, g, b):
    mu = jnp.mean(x, axis=-1, keepdims=True)
    xc = x - mu
    var = jnp.mean(xc * xc, axis=-1, keepdims=True)
    return xc * lax.rsqrt(var + LN_EPS) * g + b


def _residual_route(x, m, g_ref, b_ref, wr_ref, h_ref, r_ref, cnt_out_ref, cnt_ref):
    @pl.when(pl.program_id(0) == 0)
    def _():
        cnt_ref[...] = jnp.zeros((1, LANES), jnp.float32)

    h = _layer_norm(ALPHA * x + m, g_ref[...], b_ref[...])
    _tile_store(h_ref, h, TB)
    logits = jnp.dot(h, wr_ref[...], precision=lax.Precision.HIGHEST,
                     preferred_element_type=jnp.float32)
    lane = lax.broadcasted_iota(jnp.int32, logits.shape, 1)
    lanef = lane.astype(jnp.float32)
    big = jnp.float32(1e9)
    lg = jnp.where(lane < N_GROUPS, logits, NEG)
    mg = jnp.max(lg, axis=1, keepdims=True)
    gidx = jnp.min(jnp.where(lg == mg, lanef, big), axis=1, keepdims=True)
    p_grp = 1.0 / jnp.sum(jnp.where(lane < N_GROUPS, jnp.exp(lg - mg), 0.0), axis=1, keepdims=True)
    lo = N_GROUPS + EXPERTS_PER_GROUP * gidx
    le = jnp.where(lanef >= lo, jnp.where(lanef < lo + EXPERTS_PER_GROUP, logits, NEG), NEG)
    m1 = jnp.max(le, axis=1, keepdims=True)
    i1 = jnp.min(jnp.where(le == m1, lanef, big), axis=1, keepdims=True)
    le2 = jnp.where(lanef == i1, NEG, le)
    m2 = jnp.max(le2, axis=1, keepdims=True)
    i2 = jnp.min(jnp.where(le2 == m2, lanef, big), axis=1, keepdims=True)
    ratio = jnp.exp(m2 - m1)
    g1 = p_grp / (1.0 + ratio)
    g2 = g1 * ratio
    e1 = i1 - N_GROUPS
    e2 = i2 - N_GROUPS

    is1 = lanef == e1
    is2 = lanef == e2
    onehot = jnp.where(is1, 1.0, jnp.where(is2, 1.0, 0.0))
    row = lax.broadcasted_iota(jnp.int32, (TB, TB), 0)
    col = lax.broadcasted_iota(jnp.int32, (TB, TB), 1)
    ltri = jnp.where(col < row, 1.0, 0.0).astype(jnp.bfloat16)
    before = jnp.dot(ltri, onehot.astype(jnp.bfloat16), preferred_element_type=jnp.float32) + cnt_ref[...]
    rank1 = jnp.sum(jnp.where(is1, before, 0.0), axis=1, keepdims=True)
    rank2 = jnp.sum(jnp.where(is2, before, 0.0), axis=1, keepdims=True)
    cnt = cnt_ref[...] + jnp.sum(onehot, axis=0, keepdims=True)
    cnt_ref[...] = cnt
    cnt_out_ref[...] = cnt

    r_ref[...] = jnp.where(lane == 0, g1,
                           jnp.where(lane == 1, g2,
                                     jnp.where(lane == 2, e1,
                                               jnp.where(lane == 3, e2,
                                                         jnp.where(lane == 4, rank1,
                                                                   jnp.where(lane == 5, rank2, 0.0))))))


def _load_block(first_layer, head_ref, x_ref):
    if not first_layer:
        return _tile_load(x_ref, TB)
    return jnp.where(pl.program_id(0) == 0, head_ref[...], x_ref[...])


def _seq_of_step(s):
    sm1 = jnp.maximum(s - 1, 0)
    return lax.shift_right_logical(sm1, CHUNKS.bit_length() - 1), jnp.bitwise_and(sm1, CHUNKS - 1)


assert CHUNKS & (CHUNKS - 1) == 0


def _presum_kernel(st_ref, coef_ref, o_ref, *, n_rows):
    acc = coef_ref[0:1, :] * st_ref[:, 0:D]
    for k in range(1, n_rows):
        acc = acc + coef_ref[k:k + 1, :] * st_ref[:, k * D:(k + 1) * D]
    o_ref[...] = acc


def _state_presum(state2d, coef):
    n, kd = state2d.shape
    k = kd // D
    rows = 32
    return pl.pallas_call(
        functools.partial(_presum_kernel, n_rows=k),
        grid=(n // rows,),
        in_specs=[pl.BlockSpec((rows, kd), lambda i: (i, 0)),
                  pl.BlockSpec((k, D), lambda i: (0, 0))],
        out_specs=pl.BlockSpec((rows, D), lambda i: (i, 0)),
        out_shape=jax.ShapeDtypeStruct((n, D), jnp.float32),
        compiler_params=pltpu.CompilerParams(dimension_semantics=("arbitrary",),
                                             vmem_limit_bytes=VMEM_LIMIT),
        name="state_presum",
    )(state2d, coef)


def _pool_kernel(first_layer, head_ref, x_ref, ps_ref, wp_ref, sc_ref, g_ref, b_ref, wr_ref,
                 h_ref, r_ref, cnt_out_ref, ext_ref, m_ref, carry_ref, cnt_ref):
    s = pl.program_id(0)
    seq, chunk = _seq_of_step(s)
    x = _load_block(first_layer, head_ref, x_ref)
    ext_ref[POOL_CARRY:POOL_CARRY + TB, :] = x

    @pl.when(s == 0)
    def _():
        ext_ref[0:POOL_CARRY, :] = jnp.zeros((POOL_CARRY, D), jnp.float32)
        carry_ref[...] = x[0:N_HEAD, :]
        pos = jnp.bitwise_and(lax.broadcasted_iota(jnp.int32, (N_HEAD, 1), 0), N_META - 1)
        for g, w in enumerate(POOL_WINDOWS):
            lo, hi = g * POOL_GROUP_DIM, (g + 1) * POOL_GROUP_DIM
            xm = x[0:N_HEAD, lo:hi]
            acc = xm
            for k in range(1, w):
                acc = acc + jnp.where(pos >= k, ext_ref[POOL_CARRY - k:POOL_CARRY - k + N_HEAD, lo:hi], 0.0)
            cnt = jnp.minimum(pos + 1, w).astype(jnp.float32)
            d_meta = acc / cnt - xm
            xs = x[N_HEAD:TB, lo:hi]
            d_samp = (ps_ref[:, lo:hi] + xs) / float(w) - xs
            diff = jnp.concatenate([d_meta, d_samp], axis=0)
            y = jnp.dot(diff.astype(jnp.bfloat16), wp_ref[g], preferred_element_type=jnp.float32)
            m_ref[:, lo:hi] = y * sc_ref[:, lo:hi]

    @pl.when(s > 0)
    def _():
        @pl.when(chunk == 0)
        def _():
            ext_ref[0:POOL_CARRY, :] = carry_ref[pl.ds(pl.multiple_of(seq * N_META, N_META), N_META), :]

        for g, w in enumerate(POOL_WINDOWS):
            lo, hi = g * POOL_GROUP_DIM, (g + 1) * POOL_GROUP_DIM
            xg = x[:, lo:hi]
            acc = xg
            for k in range(1, w):
                acc = acc + ext_ref[POOL_CARRY - k:POOL_CARRY - k + TB, lo:hi]
            diff = acc / float(w) - xg
            y = jnp.dot(diff.astype(jnp.bfloat16), wp_ref[g], preferred_element_type=jnp.float32)
            m_ref[:, lo:hi] = y * sc_ref[:, lo:hi]
        ext_ref[0:POOL_CARRY, :] = x[TB - POOL_CARRY:TB, :]

    _residual_route(x, m_ref[...], g_ref, b_ref, wr_ref, h_ref, r_ref, cnt_out_ref, cnt_ref)


assert N_META >= max(POOL_WINDOWS)


def _const_spec(shape):
    nd = len(shape)
    return pl.BlockSpec(shape, lambda s: (0,) * nd)


def _x_spec(first_layer):
    if first_layer:
        return pl.BlockSpec((TB, D), lambda s: (jnp.maximum(s - 1, 0), 0))
    return pl.BlockSpec((TB * SUBLANES, LANES), lambda s: (s, 0))


_MIXER_OUT_SPECS = [pl.BlockSpec((TB * SUBLANES, LANES), lambda s: (s, 0)),
                    pl.BlockSpec((TB, LANES), lambda s: (s, 0)),
                    pl.BlockSpec((1, LANES), lambda s: (0, 0))]
_MIXER_OUT_SHAPES = [jax.ShapeDtypeStruct((NTOK * SUBLANES, LANES), jnp.float32),
                     jax.ShapeDtypeStruct((NTOK, LANES), jnp.float32),
                     jax.ShapeDtypeStruct((1, LANES), jnp.float32)]


def _pool_layer(first_layer, head, x, ps, wp_bf, scale, ln_g, ln_b, wr):
    return pl.pallas_call(
        functools.partial(_pool_kernel, first_layer),
        grid=(N_STEP,),
        in_specs=[_const_spec((TB, D)), _x_spec(first_layer),
                  _const_spec((DEC_BATCH, D)),
                  _const_spec((len(POOL_WINDOWS), POOL_GROUP_DIM, POOL_GROUP_DIM)),
                  _const_spec((1, D)), _const_spec((1, D)), _const_spec((1, D)),
                  _const_spec((D, LANES))],
        out_specs=_MIXER_OUT_SPECS,
        out_shape=_MIXER_OUT_SHAPES,
        scratch_shapes=[pltpu.VMEM((POOL_CARRY + TB, D), jnp.float32),
                        pltpu.VMEM((TB, D), jnp.float32),
                        pltpu.VMEM((N_HEAD, D), jnp.float32),
                        pltpu.VMEM((1, LANES), jnp.float32)],
        compiler_params=pltpu.CompilerParams(dimension_semantics=("arbitrary",),
                                             vmem_limit_bytes=VMEM_LIMIT),
        name="pool_mixer",
    )(head, x, ps, wp_bf, scale, ln_g, ln_b, wr)


def _depthwise_conv(ext_ref, wdw_ref, v_ref):
    base = CONV_CARRY - CONV_STATE
    for cb in range(D // CONV_CB):
        lo, hi = cb * CONV_CB, (cb + 1) * CONV_CB

        def body(i, carry, lo=lo, hi=hi):
            r0 = pl.multiple_of(i * CONV_RB, CONV_RB)
            sub = ext_ref.at[pl.ds(r0, CONV_RB + CONV_CARRY), :]
            v = None
            for r in range(SUBLANES):
                qs = [q for q in range((base + CONV_WIDTH) // SUBLANES + 1)
                      if 0 <= SUBLANES * q + r - base < CONV_WIDTH]
                z0 = SUBLANES * qs[0] + r
                z = sub[z0:SUBLANES * qs[-1] + r + CONV_RB, lo:hi]
                p = None
                for q in qs:
                    k = SUBLANES * q + r - base
                    off = SUBLANES * (q - qs[0])
                    term = wdw_ref[k:k + 1, lo:hi] * z[off:off + CONV_RB]
                    p = term if p is None else p + term
                v = p if v is None else v + p
            v_ref[pl.ds(r0, CONV_RB), lo:hi] = v
            return carry
        lax.fori_loop(0, TB // CONV_RB, body, 0)


def _conv_kernel(first_layer, head_ref, x_ref, vs_ref, wglu_ref, bglu_ref, wdw_ref, bdw_ref, lg_ref, lb_ref,
                 wpw_ref, bpw_ref, g_ref, b_ref, wr_ref, h_ref, r_ref, cnt_out_ref, ust_ref, us_ref,
                 ext_ref, v_ref, carry_ref, cnt_ref):
    s = pl.program_id(0)
    seq, chunk = _seq_of_step(s)
    x = _load_block(first_layer, head_ref, x_ref)
    hh = jnp.dot(x.astype(jnp.bfloat16), wglu_ref[...], preferred_element_type=jnp.float32) + bglu_ref[...]
    u = hh[:, :D] * jax.nn.sigmoid(hh[:, D:])

    @pl.when(s == 0)
    def _():
        um = u[0:N_HEAD, :]
        us = u[N_HEAD:TB, :]
        carry_ref[...] = um
        us_ref[...] = us
        ext_ref[0:N_META, :] = jnp.zeros((N_META, D), jnp.float32)
        ext_ref[N_META:N_META + N_HEAD, :] = um
        pos = jnp.bitwise_and(lax.broadcasted_iota(jnp.int32, (N_HEAD, 1), 0), N_META - 1)
        acc = wdw_ref[CONV_WIDTH - 1:CONV_WIDTH, :] * um
        for d in range(1, N_META):
            k = CONV_WIDTH - 1 - d
            acc = acc + wdw_ref[k:k + 1, :] * jnp.where(pos >= d, ext_ref[N_META - d:N_META - d + N_HEAD, :], 0.0)
        v_ref[0:N_HEAD, :] = acc
        v_ref[N_HEAD:TB, :] = vs_ref[...] + wdw_ref[CONV_WIDTH - 1:CONV_WIDTH, :] * us

    @pl.when(s > 0)
    def _():
        @pl.when(chunk == 0)
        def _():
            ext_ref[0:CONV_CARRY - N_META, :] = jnp.zeros((CONV_CARRY - N_META, D), jnp.float32)
            ext_ref[CONV_CARRY - N_META:CONV_CARRY, :] = carry_ref[
                pl.ds(pl.multiple_of(seq * N_META, N_META), N_META), :]

        ext_ref[CONV_CARRY:CONV_CARRY + TB, :] = u
        _depthwise_conv(ext_ref, wdw_ref, v_ref)
        ext_ref[0:CONV_CARRY, :] = ext_ref[TB:TB + CONV_CARRY, :]

        @pl.when(chunk == CHUNKS - 1)
        def _():
            ust_ref[0] = ext_ref[CONV_CARRY + TB - CONV_STATE:CONV_CARRY + TB, :]

    v = v_ref[...] + bdw_ref[...]
    v = _layer_norm(v, lg_ref[...], lb_ref[...])
    v = v * jax.nn.sigmoid(v)
    m = jnp.dot(v.astype(jnp.bfloat16), wpw_ref[...], preferred_element_type=jnp.float32) + bpw_ref[...]
    _residual_route(x, m, g_ref, b_ref, wr_ref, h_ref, r_ref, cnt_out_ref, cnt_ref)


assert CONV_CARRY - N_META + N_META >= CONV_STATE and N_META <= CONV_STATE


def _conv_layer(first_layer, head, x, vs, wglu_bf, bglu, wdw, bdw, cln_g, cln_b, wpw_bf, bpw, ln_g, ln_b, wr):
    return pl.pallas_call(
        functools.partial(_conv_kernel, first_layer),
        grid=(N_STEP,),
        in_specs=[_const_spec((TB, D)), _x_spec(first_layer),
                  _const_spec((DEC_BATCH, D)),
                  _const_spec((D, 2 * D)), _const_spec((1, 2 * D)),
                  _const_spec((CONV_WIDTH, D)), _const_spec((1, D)),
                  _const_spec((1, D)), _const_spec((1, D)),
                  _const_spec((D, D)), _const_spec((1, D)),
                  _const_spec((1, D)), _const_spec((1, D)),
                  _const_spec((D, LANES))],
        out_specs=_MIXER_OUT_SPECS + [
            pl.BlockSpec((1, CONV_STATE, D), lambda s: (jnp.maximum(s - 1, 0) // CHUNKS, 0, 0)),
            _const_spec((DEC_BATCH, D))],
        out_shape=_MIXER_OUT_SHAPES + [
            jax.ShapeDtypeStruct((BATCH, CONV_STATE, D), jnp.float32),
            jax.ShapeDtypeStruct((DEC_BATCH, D), jnp.float32)],
        scratch_shapes=[pltpu.VMEM((CONV_CARRY + TB, D), jnp.float32),
                        pltpu.VMEM((TB, D), jnp.float32),
                        pltpu.VMEM((N_HEAD, D), jnp.float32),
                        pltpu.VMEM((1, LANES), jnp.float32)],
        compiler_params=pltpu.CompilerParams(dimension_semantics=("arbitrary",),
                                             vmem_limit_bytes=VMEM_LIMIT),
        name="conv_mixer",
    )(head, x, vs, wglu_bf, bglu, wdw, bdw, cln_g, cln_b, wpw_bf, bpw, ln_g, ln_b, wr)


def _slot_kernel(slot_ref, pstart_ref, count_ref, tos_ref):
    def pad_one(sl, c):
        tos_ref[sl] = 0
        return c

    def pad_expert(e, carry):
        lax.fori_loop(pstart_ref[e] + count_ref[e], pstart_ref[e + 1], pad_one, 0)
        return carry
    lax.fori_loop(0, N_EXPERTS, pad_expert, 0)
    lax.fori_loop(pstart_ref[N_EXPERTS], N_SLOT, pad_one, 0)

    def body(a, carry):
        tos_ref[slot_ref[a]] = lax.shift_right_logical(a, 1)
        return carry
    lax.fori_loop(0, N_ASSIGN, body, 0, unroll=8)


assert TOP_K == 2


def _slot_layer(slot, pstart, counts):
    smem = pl.BlockSpec(memory_space=pltpu.SMEM)
    return pl.pallas_call(
        _slot_kernel,
        in_specs=[smem, smem, smem],
        out_specs=smem,
        out_shape=jax.ShapeDtypeStruct((N_SLOT,), jnp.int32),
        name="slot_invert",
    )(slot, pstart, counts)


def _tile_gather_start(src_hbm, dst, sem, idx_ref, base, n_rows):
    for r in range(n_rows):
        tok = idx_ref[base + r]
        pltpu.make_async_copy(src_hbm.at[pl.ds(pl.multiple_of(tok * SUBLANES, SUBLANES), SUBLANES), :],
                              dst.at[pl.ds(r * SUBLANES, SUBLANES), :], sem).start(priority=r % 2)


def _tile_gather_wait(src_hbm, dst, sem, n_rows):
    pltpu.make_async_copy(src_hbm.at[pl.ds(0, n_rows * SUBLANES), :], dst, sem).wait()


def _expert_kernel(be_ref, nused_ref, tos_ref, x_hbm, wg_ref, wu_ref, wd_ref, ys_ref,
                   xbuf, wgb, wub, wdb, sem):
    i = pl.program_id(0)
    nused = nused_ref[0]

    @pl.when(i == 0)
    def _():
        _tile_gather_start(x_hbm, xbuf.at[0], sem.at[0], tos_ref, 0, BM)

    @pl.when(i < nused)
    def _():
        slot = i % 2
        _tile_gather_wait(x_hbm, xbuf.at[slot], sem.at[slot], BM)

        @pl.when(jnp.logical_or(i == 0, be_ref[i] != be_ref[jnp.maximum(i - 1, 0)]))
        def _():
            wgb[...] = wg_ref[0].astype(jnp.bfloat16)
            wub[...] = wu_ref[0].astype(jnp.bfloat16)
            wdb[...] = wd_ref[0].astype(jnp.bfloat16)

        xb = _tile_load(xbuf.at[slot], BM).astype(jnp.bfloat16)

        nxt = jnp.minimum(i + 1, nused - 1)
        _tile_gather_start(x_hbm, xbuf.at[1 - slot], sem.at[1 - slot], tos_ref, nxt * BM, BM)

        gate = jnp.dot(xb, wgb[...], preferred_element_type=jnp.float32)
        up = jnp.dot(xb, wub[...], preferred_element_type=jnp.float32)
        hid = gate * jax.nn.sigmoid(gate) * up
        y = jnp.dot(hid.astype(jnp.bfloat16), wdb[...], preferred_element_type=jnp.float32)
        _tile_store(ys_ref, y, BM)

        @pl.when(i == nused - 1)
        def _():
            _tile_gather_wait(x_hbm, xbuf.at[1 - slot], sem.at[1 - slot], BM)

    @pl.when(i >= nused)
    def _():
        ys_ref[...] = jnp.zeros((BM * SUBLANES, LANES), jnp.float32)


def _expert_layer(block_expert, nused, token_of_slot, h_tiles, w_gate, w_up, w_down):
    grid_spec = pltpu.PrefetchScalarGridSpec(
        num_scalar_prefetch=3,
        grid=(NB,),
        in_specs=[pl.BlockSpec(memory_space=pl.ANY),
                  pl.BlockSpec((1, D, D_EXPERT), lambda i, be, nu, tos: (be[i], 0, 0)),
                  pl.BlockSpec((1, D, D_EXPERT), lambda i, be, nu, tos: (be[i], 0, 0)),
                  pl.BlockSpec((1, D_EXPERT, D), lambda i, be, nu, tos: (be[i], 0, 0))],
        out_specs=pl.BlockSpec((BM * SUBLANES, LANES), lambda i, be, nu, tos: (i, 0)),
        scratch_shapes=[pltpu.VMEM((2, BM * SUBLANES, LANES), jnp.float32),
                        pltpu.VMEM((D, D_EXPERT), jnp.bfloat16),
                        pltpu.VMEM((D, D_EXPERT), jnp.bfloat16),
                        pltpu.VMEM((D_EXPERT, D), jnp.bfloat16),
                        pltpu.SemaphoreType.DMA((2,))],
    )
    return pl.pallas_call(
        _expert_kernel,
        grid_spec=grid_spec,
        out_shape=jax.ShapeDtypeStruct((N_SLOT * SUBLANES, LANES), jnp.float32),
        compiler_params=pltpu.CompilerParams(dimension_semantics=("arbitrary",),
                                             vmem_limit_bytes=VMEM_LIMIT),
        name="expert_mlp",
    )(block_expert, nused, token_of_slot, h_tiles, w_gate, w_up, w_down)


def _combine_kernel(last_layer, slot_ref, h_ref, r_ref, ys_hbm, g_ref, b_ref, *rest):
    if last_layer:
        yp_ref, ysamp_ref, ybuf, sem = rest
    else:
        o_ref, ybuf, sem = rest
    i = pl.program_id(0)
    n = pl.num_programs(0)

    def start(step, buf):
        _tile_gather_start(ys_hbm, ybuf.at[buf], sem.at[buf], slot_ref, step * (TOP_K * TB), TOP_K * TB)

    @pl.when(i == 0)
    def _():
        start(0, 0)

    buf = i % 2
    _tile_gather_wait(ys_hbm, ybuf.at[buf], sem.at[buf], TOP_K * TB)
    yb = ybuf.at[buf]
    y0 = jnp.concatenate([yb[pl.ds(s, TB, stride=TOP_K * SUBLANES), :] for s in range(SUBLANES)], axis=1)
    y1 = jnp.concatenate([yb[pl.ds(SUBLANES + s, TB, stride=TOP_K * SUBLANES), :] for s in range(SUBLANES)],
                         axis=1)
    h = _tile_load(h_ref, TB)
    r = r_ref[...]

    start(jnp.minimum(i + 1, n - 1), 1 - buf)

    f = r[:, 0:1] * y0 + r[:, 1:2] * y1
    out = _layer_norm(ALPHA * h + f, g_ref[...], b_ref[...])
    if last_layer:
        yp_ref[...] = out

        @pl.when(i == 0)
        def _():
            ysamp_ref[...] = out[N_HEAD:TB, :]
    else:
        _tile_store(o_ref, out, TB)

    @pl.when(i == n - 1)
    def _():
        _tile_gather_wait(ys_hbm, ybuf.at[1 - buf], sem.at[1 - buf], TOP_K * TB)


def _combine_layer(last_layer, slot, h_tiles, r_all, ys, ln_g, ln_b):
    if last_layer:
        out_specs = [pl.BlockSpec((TB, D), lambda i, sl: (jnp.maximum(i - 1, 0), 0)),
                     pl.BlockSpec((DEC_BATCH, D), lambda i, sl: (0, 0))]
        out_shape = [jax.ShapeDtypeStruct((BATCH * SEQ, D), jnp.float32),
                     jax.ShapeDtypeStruct((DEC_BATCH, D), jnp.float32)]
    else:
        out_specs = pl.BlockSpec((TB * SUBLANES, LANES), lambda i, sl: (i, 0))
        out_shape = jax.ShapeDtypeStruct((NTOK * SUBLANES, LANES), jnp.float32)
    grid_spec = pltpu.PrefetchScalarGridSpec(
        num_scalar_prefetch=1,
        grid=(N_STEP,),
        in_specs=[pl.BlockSpec((TB * SUBLANES, LANES), lambda i, sl: (i, 0)),
                  pl.BlockSpec((TB, LANES), lambda i, sl: (i, 0)),
                  pl.BlockSpec(memory_space=pl.ANY),
                  pl.BlockSpec((1, D), lambda i, sl: (0, 0)),
                  pl.BlockSpec((1, D), lambda i, sl: (0, 0))],
        out_specs=out_specs,
        scratch_shapes=[pltpu.VMEM((2, TOP_K * TB * SUBLANES, LANES), jnp.float32),
                        pltpu.SemaphoreType.DMA((2,))],
    )
    return pl.pallas_call(
        functools.partial(_combine_kernel, last_layer),
        grid_spec=grid_spec,
        out_shape=out_shape,
        compiler_params=pltpu.CompilerParams(dimension_semantics=("arbitrary",),
                                             vmem_limit_bytes=VMEM_LIMIT),
        name="moe_combine",
    )(slot, h_tiles, r_all, ys, ln_g, ln_b)


def _routing_metadata(r_all, cnt):
    counts = cnt[0, :N_EXPERTS].astype(jnp.int32)
    nblk = (counts + BM - 1) // BM
    blk_end = jnp.cumsum(nblk)
    pstart = (blk_end - nblk) * BM
    nused = blk_end[-1]
    blk = jnp.arange(NB, dtype=jnp.int32)
    block_expert = jnp.sum((blk[:, None] >= blk_end[None, :]).astype(jnp.int32), axis=1)
    last_expert = jnp.max(jnp.where(nblk > 0, jnp.arange(N_EXPERTS, dtype=jnp.int32), 0))
    block_expert = jnp.where(blk < nused, block_expert, last_expert).astype(jnp.int32)
    expert = r_all[:, 2:2 + TOP_K].astype(jnp.int32)
    rank = r_all[:, 2 + TOP_K:2 + 2 * TOP_K].astype(jnp.int32)
    sel = expert[:, :, None] == jnp.arange(N_EXPERTS, dtype=jnp.int32)[None, None, :]
    slot = rank + jnp.sum(jnp.where(sel, pstart[None, None, :], 0), axis=-1)
    pstart_ext = jnp.concatenate([pstart, (blk_end[-1:] * BM)]).astype(jnp.int32)
    return (block_expert, nused.reshape(1).astype(jnp.int32), slot.reshape(N_ASSIGN).astype(jnp.int32),
            pstart_ext, counts)


def _tiles_to_rows(h_tiles, start, n):
    return h_tiles.reshape(NTOK, SUBLANES, LANES)[start:start + n].reshape(n, D)


def kernel(x_prompt, x_sample, state_pool, state_conv, meta, w_pool, pool_scale, w_glu, b_glu, w_dw, b_dw, conv_ln_g, conv_ln_b, w_pw, b_pw, ln_mix_g, ln_mix_b, ln_ffn_g, ln_ffn_b, w_router_group, w_router_expert, w_gate, w_up, w_down):
    f32 = jnp.float32
    x_samp2d = x_sample.reshape(DEC_BATCH, D)
    head = jnp.concatenate([jnp.tile(meta.astype(f32), (BATCH, 1)), x_samp2d], axis=0)
    x_in = x_prompt.reshape(BATCH * SEQ, D)

    k_idx = jnp.arange(POOL_STATE)[:, None]
    win = jnp.repeat(jnp.asarray(POOL_WINDOWS), POOL_GROUP_DIM)[None, :]
    pool_coef = (k_idx >= (POOL_STATE + 1 - win)).astype(f32)

    new_pool_p, new_pool_s, new_conv_p, new_conv_s = [], [], [], []
    for i in range(DEPTH):
        j = i // 2
        first = i == 0
        wr = jnp.concatenate([w_router_group[i], w_router_expert[i],
                              jnp.zeros((D, LANES - N_GROUPS - N_EXPERTS), f32)], axis=1)
        ln_g, ln_b = ln_mix_g[i].reshape(1, D), ln_mix_b[i].reshape(1, D)
        if i % 2 == 0:
            if first:
                tail = x_prompt[:, SEQ - POOL_STATE:]
                x_s = x_samp2d
            else:
                prompt_tiles = x_in.reshape(NTOK, SUBLANES, LANES)[TB:].reshape(BATCH, SEQ, SUBLANES, LANES)
                tail = prompt_tiles[:, SEQ - POOL_STATE:].reshape(BATCH, POOL_STATE, D)
                x_s = _tiles_to_rows(x_in, N_HEAD, DEC_BATCH)
            new_pool_p.append(tail)
            new_pool_s.append(jnp.concatenate([state_pool[j][:, 1:], x_s[:, None, :]], axis=1))
            ps = _state_presum(state_pool[j].reshape(DEC_BATCH, POOL_STATE * D), pool_coef)
            h1, r_all, cnt = _pool_layer(first, head, x_in, ps, w_pool[j].astype(jnp.bfloat16),
                                         pool_scale[j].reshape(1, D), ln_g, ln_b, wr)
        else:
            vs = _state_presum(state_conv[j].reshape(DEC_BATCH, CONV_STATE * D), w_dw[j][:CONV_STATE])
            h1, r_all, cnt, ust, u_s = _conv_layer(
                first, head, x_in, vs, w_glu[j].astype(jnp.bfloat16), b_glu[j].reshape(1, 2 * D), w_dw[j],
                b_dw[j].reshape(1, D), conv_ln_g[j].reshape(1, D), conv_ln_b[j].reshape(1, D),
                w_pw[j].astype(jnp.bfloat16), b_pw[j].reshape(1, D), ln_g, ln_b, wr)
            new_conv_p.append(ust)
            new_conv_s.append(jnp.concatenate([state_conv[j][:, 1:], u_s[:, None, :]], axis=1))
        block_expert, nused, slot, pstart_ext, counts = _routing_metadata(r_all, cnt)
        token_of_slot = _slot_layer(slot, pstart_ext, counts)
        ys = _expert_layer(block_expert, nused, token_of_slot, h1, w_gate[i], w_up[i], w_down[i])
        x_in = _combine_layer(i == DEPTH - 1, slot, h1, r_all, ys,
                              ln_ffn_g[i].reshape(1, D), ln_ffn_b[i].reshape(1, D))

    y_prompt, y_samp = x_in
    return (y_prompt.reshape(BATCH, SEQ, D), y_samp.reshape(DEC_BATCH, 1, D),
            jnp.stack(new_pool_p), jnp.stack(new_pool_s), jnp.stack(new_conv_p), jnp.stack(new_conv_s))
```

```python
import functools

import jax
import jax.numpy as jnp
from jax import lax
from jax.experimental import pallas as pl
from jax.experimental.pallas import tpu as pltpu

D = 1024
BATCH = 8
SEQ = 2048
DEPTH = 4
DEC_BATCH = 128
N_META = 16
POOL_WINDOWS = (2, 4, 8, 16)
POOL_GROUP_DIM = D // len(POOL_WINDOWS)
POOL_STATE = max(POOL_WINDOWS) - 1
CONV_WIDTH = 31
CONV_STATE = CONV_WIDTH - 1
N_GROUPS = 4
EXPERTS_PER_GROUP = 8
N_EXPERTS = N_GROUPS * EXPERTS_PER_GROUP
TOP_K = 2
D_EXPERT = D // 2
ALPHA = (2.0 * DEPTH) ** 0.25
LN_EPS = 1e-5

LANES = 128
SUBLANES = 8
N_HEAD = BATCH * N_META
TB = 256
NTOK = N_HEAD + DEC_BATCH + BATCH * SEQ
N_STEP = NTOK // TB
CHUNKS = SEQ // TB
POOL_CARRY = 16
CONV_CARRY = 32
CONV_RB = 128
CONV_CB = 128
BM = 256
N_ASSIGN = NTOK * TOP_K
NB = N_ASSIGN // BM + N_EXPERTS
N_SLOT = NB * BM
VMEM_LIMIT = 48 * 1024 * 1024
NEG = -1e30

assert D == SUBLANES * LANES and N_HEAD + DEC_BATCH == TB and SEQ % TB == 0
assert N_ASSIGN % BM == 0 and TB % CONV_RB == 0 and N_META == POOL_CARRY and CONV_STATE <= TB


def _tile_load(ref, n):
    return jnp.concatenate([ref[pl.ds(s, n, stride=SUBLANES), :] for s in range(SUBLANES)], axis=1)


def _tile_store(ref, val, n):
    for s in range(SUBLANES):
        ref[pl.ds(s, n, stride=SUBLANES), :] = val[:, s * LANES:(s + 1) * LANES]


def _layer_norm(x, g, b):
    mu = jnp.mean(x, axis=-1, keepdims=True)
    xc = x - mu
    var = jnp.mean(xc * xc, axis=-1, keepdims=True)
    return xc * lax.rsqrt(var + LN_EPS) * g + b


def _residual_route(x, m, g_ref, b_ref, wr_ref, h_ref, r_ref, cnt_out_ref, cnt_ref, ltri_ref):
    @pl.when(pl.program_id(0) == 0)
    def _():
        cnt_ref[...] = jnp.zeros((1, LANES), jnp.float32)
        row = lax.broadcasted_iota(jnp.int32, (TB, TB), 0)
        col = lax.broadcasted_iota(jnp.int32, (TB, TB), 1)
        ltri_ref[...] = jnp.where(col < row, 1.0, 0.0).astype(jnp.bfloat16)

    h = _layer_norm(ALPHA * x + m, g_ref[...], b_ref[...])
    _tile_store(h_ref, h, TB)
    h_hi = h.astype(jnp.bfloat16)
    h_lo = (h - h_hi.astype(jnp.float32)).astype(jnp.bfloat16)
    hi_both = jnp.dot(h_hi, wr_ref[...], preferred_element_type=jnp.float32)
    lo_hi = jnp.dot(h_lo, wr_ref[:, 0:LANES], preferred_element_type=jnp.float32)
    logits = hi_both[:, 0:LANES] + (hi_both[:, LANES:2 * LANES] + lo_hi)
    lane = lax.broadcasted_iota(jnp.int32, logits.shape, 1)
    lanef = lane.astype(jnp.float32)
    big = jnp.float32(1e9)
    lg = jnp.where(lane < N_GROUPS, logits, NEG)
    mg = jnp.max(lg, axis=1, keepdims=True)
    gidx = jnp.min(jnp.where(lg == mg, lanef, big), axis=1, keepdims=True)
    p_grp = 1.0 / jnp.sum(jnp.where(lane < N_GROUPS, jnp.exp(lg - mg), 0.0), axis=1, keepdims=True)
    lo = N_GROUPS + EXPERTS_PER_GROUP * gidx
    le = jnp.where(lanef >= lo, jnp.where(lanef < lo + EXPERTS_PER_GROUP, logits, NEG), NEG)
    m1 = jnp.max(le, axis=1, keepdims=True)
    i1 = jnp.min(jnp.where(le == m1, lanef, big), axis=1, keepdims=True)
    le2 = jnp.where(lanef == i1, NEG, le)
    m2 = jnp.max(le2, axis=1, keepdims=True)
    i2 = jnp.min(jnp.where(le2 == m2, lanef, big), axis=1, keepdims=True)
    ratio = jnp.exp(m2 - m1)
    g1 = p_grp / (1.0 + ratio)
    g2 = g1 * ratio
    e1 = i1 - N_GROUPS
    e2 = i2 - N_GROUPS

    is1 = lanef == e1
    is2 = lanef == e2
    onehot = jnp.where(is1, 1.0, jnp.where(is2, 1.0, 0.0))
    before = jnp.dot(ltri_ref[...], onehot.astype(jnp.bfloat16),
                     preferred_element_type=jnp.float32) + cnt_ref[...]
    rank1 = jnp.sum(jnp.where(is1, before, 0.0), axis=1, keepdims=True)
    rank2 = jnp.sum(jnp.where(is2, before, 0.0), axis=1, keepdims=True)
    cnt = cnt_ref[...] + jnp.sum(onehot, axis=0, keepdims=True)
    cnt_ref[...] = cnt
    cnt_out_ref[...] = cnt

    r_ref[...] = jnp.where(lane == 0, g1,
                           jnp.where(lane == 1, g2,
                                     jnp.where(lane == 2, e1,
                                               jnp.where(lane == 3, e2,
                                                         jnp.where(lane == 4, rank1,
                                                                   jnp.where(lane == 5, rank2, 0.0))))))


def _load_block(first_layer, head_ref, x_ref):
    if not first_layer:
        return _tile_load(x_ref, TB)
    return jnp.where(pl.program_id(0) == 0, head_ref[...], x_ref[...])


def _seq_of_step(s):
    sm1 = jnp.maximum(s - 1, 0)
    return lax.shift_right_logical(sm1, CHUNKS.bit_length() - 1), jnp.bitwise_and(sm1, CHUNKS - 1)


assert CHUNKS & (CHUNKS - 1) == 0


def _presum_kernel(st_ref, coef_ref, o_ref, *, n_rows):
    acc = coef_ref[0:1, :] * st_ref[:, 0:D]
    for k in range(1, n_rows):
        acc = acc + coef_ref[k:k + 1, :] * st_ref[:, k * D:(k + 1) * D]
    o_ref[...] = acc


def _state_presum(state2d, coef):
    n, kd = state2d.shape
    k = kd // D
    rows = 32
    return pl.pallas_call(
        functools.partial(_presum_kernel, n_rows=k),
        grid=(n // rows,),
        in_specs=[pl.BlockSpec((rows, kd), lambda i: (i, 0)),
                  pl.BlockSpec((k, D), lambda i: (0, 0))],
        out_specs=pl.BlockSpec((rows, D), lambda i: (i, 0)),
        out_shape=jax.ShapeDtypeStruct((n, D), jnp.float32),
        compiler_params=pltpu.CompilerParams(dimension_semantics=("arbitrary",),
                                             vmem_limit_bytes=VMEM_LIMIT),
        name="state_presum",
    )(state2d, coef)


def _pool_kernel(first_layer, head_ref, x_ref, ps_ref, wp_ref, sc_ref, g_ref, b_ref, wr_ref,
                 h_ref, r_ref, cnt_out_ref, ext_ref, m_ref, carry_ref, cnt_ref, ltri_ref):
    s = pl.program_id(0)
    seq, chunk = _seq_of_step(s)
    x = _load_block(first_layer, head_ref, x_ref)
    ext_ref[POOL_CARRY:POOL_CARRY + TB, :] = x

    @pl.when(s == 0)
    def _():
        ext_ref[0:POOL_CARRY, :] = jnp.zeros((POOL_CARRY, D), jnp.float32)
        carry_ref[...] = x[0:N_HEAD, :]
        pos = jnp.bitwise_and(lax.broadcasted_iota(jnp.int32, (N_HEAD, 1), 0), N_META - 1)
        for g, w in enumerate(POOL_WINDOWS):
            lo, hi = g * POOL_GROUP_DIM, (g + 1) * POOL_GROUP_DIM
            xm = x[0:N_HEAD, lo:hi]
            acc = xm
            for k in range(1, w):
                acc = acc + jnp.where(pos >= k, ext_ref[POOL_CARRY - k:POOL_CARRY - k + N_HEAD, lo:hi], 0.0)
            cnt = jnp.minimum(pos + 1, w).astype(jnp.float32)
            d_meta = acc / cnt - xm
            xs = x[N_HEAD:TB, lo:hi]
            d_samp = (ps_ref[:, lo:hi] + xs) / float(w) - xs
            diff = jnp.concatenate([d_meta, d_samp], axis=0)
            y = jnp.dot(diff.astype(jnp.bfloat16), wp_ref[g], preferred_element_type=jnp.float32)
            m_ref[:, lo:hi] = y * sc_ref[:, lo:hi]

    @pl.when(s > 0)
    def _():
        @pl.when(chunk == 0)
        def _():
            ext_ref[0:POOL_CARRY, :] = carry_ref[pl.ds(pl.multiple_of(seq * N_META, N_META), N_META), :]

        for g, w in enumerate(POOL_WINDOWS):
            lo, hi = g * POOL_GROUP_DIM, (g + 1) * POOL_GROUP_DIM
            xg = x[:, lo:hi]
            acc = xg
            for k in range(1, w):
                acc = acc + ext_ref[POOL_CARRY - k:POOL_CARRY - k + TB, lo:hi]
            diff = acc / float(w) - xg
            y = jnp.dot(diff.astype(jnp.bfloat16), wp_ref[g], preferred_element_type=jnp.float32)
            m_ref[:, lo:hi] = y * sc_ref[:, lo:hi]
        ext_ref[0:POOL_CARRY, :] = x[TB - POOL_CARRY:TB, :]

    _residual_route(x, m_ref[...], g_ref, b_ref, wr_ref, h_ref, r_ref, cnt_out_ref, cnt_ref, ltri_ref)


assert N_META >= max(POOL_WINDOWS)


def _const_spec(shape):
    nd = len(shape)
    return pl.BlockSpec(shape, lambda s: (0,) * nd)


def _x_spec(first_layer):
    if first_layer:
        return pl.BlockSpec((TB, D), lambda s: (jnp.maximum(s - 1, 0), 0))
    return pl.BlockSpec((TB * SUBLANES, LANES), lambda s: (s, 0))


_MIXER_OUT_SPECS = [pl.BlockSpec((TB * SUBLANES, LANES), lambda s: (s, 0)),
                    pl.BlockSpec((TB, LANES), lambda s: (s, 0)),
                    pl.BlockSpec((1, LANES), lambda s: (0, 0))]
_MIXER_OUT_SHAPES = [jax.ShapeDtypeStruct((NTOK * SUBLANES, LANES), jnp.float32),
                     jax.ShapeDtypeStruct((NTOK, LANES), jnp.float32),
                     jax.ShapeDtypeStruct((1, LANES), jnp.float32)]


def _pool_layer(first_layer, head, x, ps, wp_bf, scale, ln_g, ln_b, wr):
    return pl.pallas_call(
        functools.partial(_pool_kernel, first_layer),
        grid=(N_STEP,),
        in_specs=[_const_spec((TB, D)), _x_spec(first_layer),
                  _const_spec((DEC_BATCH, D)),
                  _const_spec((len(POOL_WINDOWS), POOL_GROUP_DIM, POOL_GROUP_DIM)),
                  _const_spec((1, D)), _const_spec((1, D)), _const_spec((1, D)),
                  _const_spec((D, 2 * LANES))],
        out_specs=_MIXER_OUT_SPECS,
        out_shape=_MIXER_OUT_SHAPES,
        scratch_shapes=[pltpu.VMEM((POOL_CARRY + TB, D), jnp.float32),
                        pltpu.VMEM((TB, D), jnp.float32),
                        pltpu.VMEM((N_HEAD, D), jnp.float32),
                        pltpu.VMEM((1, LANES), jnp.float32),
                        pltpu.VMEM((TB, TB), jnp.bfloat16)],
        compiler_params=pltpu.CompilerParams(dimension_semantics=("arbitrary",),
                                             vmem_limit_bytes=VMEM_LIMIT),
        name="pool_mixer",
    )(head, x, ps, wp_bf, scale, ln_g, ln_b, wr)


def _depthwise_conv(ext_ref, wdw_ref, v_ref):
    base = CONV_CARRY - CONV_STATE
    for cb in range(D // CONV_CB):
        lo, hi = cb * CONV_CB, (cb + 1) * CONV_CB

        def body(i, carry, lo=lo, hi=hi):
            r0 = pl.multiple_of(i * CONV_RB, CONV_RB)
            sub = ext_ref.at[pl.ds(r0, CONV_RB + CONV_CARRY), :]
            v = None
            for r in range(SUBLANES):
                qs = [q for q in range((base + CONV_WIDTH) // SUBLANES + 1)
                      if 0 <= SUBLANES * q + r - base < CONV_WIDTH]
                z0 = SUBLANES * qs[0] + r
                z = sub[z0:SUBLANES * qs[-1] + r + CONV_RB, lo:hi]
                p = None
                for q in qs:
                    k = SUBLANES * q + r - base
                    off = SUBLANES * (q - qs[0])
                    term = wdw_ref[k:k + 1, lo:hi] * z[off:off + CONV_RB]
                    p = term if p is None else p + term
                v = p if v is None else v + p
            v_ref[pl.ds(r0, CONV_RB), lo:hi] = v
            return carry
        lax.fori_loop(0, TB // CONV_RB, body, 0)


def _conv_kernel(first_layer, head_ref, x_ref, vs_ref, wglu_ref, bglu_ref, wdw_ref, bdw_ref, lg_ref, lb_ref,
                 wpw_ref, bpw_ref, g_ref, b_ref, wr_ref, h_ref, r_ref, cnt_out_ref, ust_ref, us_ref,
                 ext_ref, v_ref, carry_ref, cnt_ref, ltri_ref):
    s = pl.program_id(0)
    seq, chunk = _seq_of_step(s)
    x = _load_block(first_layer, head_ref, x_ref)
    hh = jnp.dot(x.astype(jnp.bfloat16), wglu_ref[...], preferred_element_type=jnp.float32) + bglu_ref[...]
    u = hh[:, :D] * jax.nn.sigmoid(hh[:, D:])

    @pl.when(s == 0)
    def _():
        um = u[0:N_HEAD, :]
        us = u[N_HEAD:TB, :]
        carry_ref[...] = um
        us_ref[...] = us
        ext_ref[0:N_META, :] = jnp.zeros((N_META, D), jnp.float32)
        ext_ref[N_META:N_META + N_HEAD, :] = um
        pos = jnp.bitwise_and(lax.broadcasted_iota(jnp.int32, (N_HEAD, 1), 0), N_META - 1)
        acc = wdw_ref[CONV_WIDTH - 1:CONV_WIDTH, :] * um
        for d in range(1, N_META):
            k = CONV_WIDTH - 1 - d
            acc = acc + wdw_ref[k:k + 1, :] * jnp.where(pos >= d, ext_ref[N_META - d:N_META - d + N_HEAD, :], 0.0)
        v_ref[0:N_HEAD, :] = acc
        v_ref[N_HEAD:TB, :] = vs_ref[...] + wdw_ref[CONV_WIDTH - 1:CONV_WIDTH, :] * us

    @pl.when(s > 0)
    def _():
        @pl.when(chunk == 0)
        def _():
            ext_ref[0:CONV_CARRY - N_META, :] = jnp.zeros((CONV_CARRY - N_META, D), jnp.float32)
            ext_ref[CONV_CARRY - N_META:CONV_CARRY, :] = carry_ref[
                pl.ds(pl.multiple_of(seq * N_META, N_META), N_META), :]

        ext_ref[CONV_CARRY:CONV_CARRY + TB, :] = u
        _depthwise_conv(ext_ref, wdw_ref, v_ref)
        ext_ref[0:CONV_CARRY, :] = ext_ref[TB:TB + CONV_CARRY, :]

        @pl.when(chunk == CHUNKS - 1)
        def _():
            ust_ref[0] = ext_ref[CONV_CARRY + TB - CONV_STATE:CONV_CARRY + TB, :]

    v = v_ref[...] + bdw_ref[...]
    v = _layer_norm(v, lg_ref[...], lb_ref[...])
    v = v * jax.nn.sigmoid(v)
    m = jnp.dot(v.astype(jnp.bfloat16), wpw_ref[...], preferred_element_type=jnp.float32) + bpw_ref[...]
    _residual_route(x, m, g_ref, b_ref, wr_ref, h_ref, r_ref, cnt_out_ref, cnt_ref, ltri_ref)


assert CONV_CARRY - N_META + N_META >= CONV_STATE and N_META <= CONV_STATE


def _conv_layer(first_layer, head, x, vs, wglu_bf, bglu, wdw, bdw, cln_g, cln_b, wpw_bf, bpw, ln_g, ln_b, wr):
    return pl.pallas_call(
        functools.partial(_conv_kernel, first_layer),
        grid=(N_STEP,),
        in_specs=[_const_spec((TB, D)), _x_spec(first_layer),
                  _const_spec((DEC_BATCH, D)),
                  _const_spec((D, 2 * D)), _const_spec((1, 2 * D)),
                  _const_spec((CONV_WIDTH, D)), _const_spec((1, D)),
                  _const_spec((1, D)), _const_spec((1, D)),
                  _const_spec((D, D)), _const_spec((1, D)),
                  _const_spec((1, D)), _const_spec((1, D)),
                  _const_spec((D, 2 * LANES))],
        out_specs=_MIXER_OUT_SPECS + [
            pl.BlockSpec((1, CONV_STATE, D), lambda s: (jnp.maximum(s - 1, 0) // CHUNKS, 0, 0)),
            _const_spec((DEC_BATCH, D))],
        out_shape=_MIXER_OUT_SHAPES + [
            jax.ShapeDtypeStruct((BATCH, CONV_STATE, D), jnp.float32),
            jax.ShapeDtypeStruct((DEC_BATCH, D), jnp.float32)],
        scratch_shapes=[pltpu.VMEM((CONV_CARRY + TB, D), jnp.float32),
                        pltpu.VMEM((TB, D), jnp.float32),
                        pltpu.VMEM((N_HEAD, D), jnp.float32),
                        pltpu.VMEM((1, LANES), jnp.float32),
                        pltpu.VMEM((TB, TB), jnp.bfloat16)],
        compiler_params=pltpu.CompilerParams(dimension_semantics=("arbitrary",),
                                             vmem_limit_bytes=VMEM_LIMIT),
        name="conv_mixer",
    )(head, x, vs, wglu_bf, bglu, wdw, bdw, cln_g, cln_b, wpw_bf, bpw, ln_g, ln_b, wr)


def _tile_gather_start(src_hbm, dst, sem, row_index, n_rows):
    for r in range(n_rows):
        tok = row_index(r)
        pltpu.make_async_copy(src_hbm.at[pl.ds(pl.multiple_of(tok * SUBLANES, SUBLANES), SUBLANES), :],
                              dst.at[pl.ds(r * SUBLANES, SUBLANES), :], sem).start(priority=r % 2)


def _tile_gather_wait(src_hbm, dst, sem, n_rows):
    pltpu.make_async_copy(src_hbm.at[pl.ds(0, n_rows * SUBLANES), :], dst, sem).wait()


N_XBUF = 3

assert TOP_K == 2


def _expert_kernel(be_ref, nused_ref, base_ref, limit_ref, order_ref, x_hbm, wg_ref, wu_ref, wd_ref, ys_ref,
                   xbuf, wgb, wub, wdb, sem):
    i = pl.program_id(0)
    nused = nused_ref[0]
    last = jnp.maximum(nused - 1, 0)

    def issue(blk, buf):
        b0 = base_ref[blk]
        lim = limit_ref[blk]
        _tile_gather_start(
            x_hbm, xbuf.at[buf], sem.at[buf],
            lambda r: lax.shift_right_logical(order_ref[jnp.minimum(b0 + r, lim)], 1), BM)

    @pl.when(jnp.logical_and(i == 0, nused > 0))
    def _():
        issue(0, 0)
        issue(jnp.minimum(1, last), 1)

    @pl.when(i < nused)
    def _():
        buf = lax.rem(i, N_XBUF)
        _tile_gather_wait(x_hbm, xbuf.at[buf], sem.at[buf], BM)

        @pl.when(jnp.logical_or(i == 0, be_ref[i] != be_ref[jnp.maximum(i - 1, 0)]))
        def _():
            wgb[...] = wg_ref[0, 0].astype(jnp.bfloat16)
            wub[...] = wu_ref[0, 0].astype(jnp.bfloat16)
            wdb[...] = wd_ref[0, 0].astype(jnp.bfloat16)

        xb = _tile_load(xbuf.at[buf], BM).astype(jnp.bfloat16)

        issue(jnp.minimum(i + 2, last), lax.rem(i + 2, N_XBUF))

        gate = jnp.dot(xb, wgb[...], preferred_element_type=jnp.float32)
        up = jnp.dot(xb, wub[...], preferred_element_type=jnp.float32)
        hid = gate * jax.nn.sigmoid(gate) * up
        y = jnp.dot(hid.astype(jnp.bfloat16), wdb[...], preferred_element_type=jnp.float32)
        _tile_store(ys_ref, y, BM)

        @pl.when(i == nused - 1)
        def _():
            for ahead in (1, 2):
                b = lax.rem(i + ahead, N_XBUF)
                _tile_gather_wait(x_hbm, xbuf.at[b], sem.at[b], BM)

    @pl.when(i >= nused)
    def _():
        ys_ref[...] = jnp.zeros((BM * SUBLANES, LANES), jnp.float32)


def _expert_layer(layer, block_expert, nused, base, limit, order, h_tiles, w_gate, w_up, w_down):
    def w_spec(shape):
        return pl.BlockSpec((1, 1) + shape, lambda i, be, *_: (layer, be[i], 0, 0))
    grid_spec = pltpu.PrefetchScalarGridSpec(
        num_scalar_prefetch=5,
        grid=(NB,),
        in_specs=[pl.BlockSpec(memory_space=pl.ANY),
                  w_spec((D, D_EXPERT)), w_spec((D, D_EXPERT)), w_spec((D_EXPERT, D))],
        out_specs=pl.BlockSpec((BM * SUBLANES, LANES), lambda i, *_: (i, 0)),
        scratch_shapes=[pltpu.VMEM((N_XBUF, BM * SUBLANES, LANES), jnp.float32),
                        pltpu.VMEM((D, D_EXPERT), jnp.bfloat16),
                        pltpu.VMEM((D, D_EXPERT), jnp.bfloat16),
                        pltpu.VMEM((D_EXPERT, D), jnp.bfloat16),
                        pltpu.SemaphoreType.DMA((N_XBUF,))],
    )
    return pl.pallas_call(
        _expert_kernel,
        grid_spec=grid_spec,
        out_shape=jax.ShapeDtypeStruct((N_SLOT * SUBLANES, LANES), jnp.float32),
        compiler_params=pltpu.CompilerParams(dimension_semantics=("arbitrary",),
                                             vmem_limit_bytes=VMEM_LIMIT),
        name="expert_mlp",
    )(block_expert, nused, base, limit, order, h_tiles, w_gate, w_up, w_down)


def _combine_kernel(last_layer, slot_ref, h_ref, r_ref, ys_hbm, g_ref, b_ref, *rest):
    if last_layer:
        yp_ref, ysamp_ref, ybuf, sem = rest
    else:
        o_ref, ybuf, sem = rest
    i = pl.program_id(0)
    n = pl.num_programs(0)

    def start(step, buf):
        base = step * (TOP_K * TB)
        _tile_gather_start(ys_hbm, ybuf.at[buf], sem.at[buf], lambda r: slot_ref[base + r], TOP_K * TB)

    @pl.when(i == 0)
    def _():
        start(0, 0)

    buf = i % 2
    _tile_gather_wait(ys_hbm, ybuf.at[buf], sem.at[buf], TOP_K * TB)
    yb = ybuf.at[buf]
    y0 = jnp.concatenate([yb[pl.ds(s, TB, stride=TOP_K * SUBLANES), :] for s in range(SUBLANES)], axis=1)
    y1 = jnp.concatenate([yb[pl.ds(SUBLANES + s, TB, stride=TOP_K * SUBLANES), :] for s in range(SUBLANES)],
                         axis=1)
    h = _tile_load(h_ref, TB)
    r = r_ref[...]

    start(jnp.minimum(i + 1, n - 1), 1 - buf)

    f = r[:, 0:1] * y0 + r[:, 1:2] * y1
    out = _layer_norm(ALPHA * h + f, g_ref[...], b_ref[...])
    if last_layer:
        yp_ref[...] = out

        @pl.when(i == 0)
        def _():
            ysamp_ref[...] = out[N_HEAD:TB, :]
    else:
        _tile_store(o_ref, out, TB)

    @pl.when(i == n - 1)
    def _():
        _tile_gather_wait(ys_hbm, ybuf.at[1 - buf], sem.at[1 - buf], TOP_K * TB)


def _combine_layer(last_layer, slot, h_tiles, r_all, ys, ln_g, ln_b):
    if last_layer:
        out_specs = [pl.BlockSpec((TB, D), lambda i, sl: (jnp.maximum(i - 1, 0), 0)),
                     pl.BlockSpec((DEC_BATCH, D), lambda i, sl: (0, 0))]
        out_shape = [jax.ShapeDtypeStruct((BATCH * SEQ, D), jnp.float32),
                     jax.ShapeDtypeStruct((DEC_BATCH, D), jnp.float32)]
    else:
        out_specs = pl.BlockSpec((TB * SUBLANES, LANES), lambda i, sl: (i, 0))
        out_shape = jax.ShapeDtypeStruct((NTOK * SUBLANES, LANES), jnp.float32)
    grid_spec = pltpu.PrefetchScalarGridSpec(
        num_scalar_prefetch=1,
        grid=(N_STEP,),
        in_specs=[pl.BlockSpec((TB * SUBLANES, LANES), lambda i, sl: (i, 0)),
                  pl.BlockSpec((TB, LANES), lambda i, sl: (i, 0)),
                  pl.BlockSpec(memory_space=pl.ANY),
                  pl.BlockSpec((1, D), lambda i, sl: (0, 0)),
                  pl.BlockSpec((1, D), lambda i, sl: (0, 0))],
        out_specs=out_specs,
        scratch_shapes=[pltpu.VMEM((2, TOP_K * TB * SUBLANES, LANES), jnp.float32),
                        pltpu.SemaphoreType.DMA((2,))],
    )
    return pl.pallas_call(
        functools.partial(_combine_kernel, last_layer),
        grid_spec=grid_spec,
        out_shape=out_shape,
        compiler_params=pltpu.CompilerParams(dimension_semantics=("arbitrary",),
                                             vmem_limit_bytes=VMEM_LIMIT),
        name="moe_combine",
    )(slot, h_tiles, r_all, ys, ln_g, ln_b)


def _routing_metadata(r_all, cnt):
    counts = cnt[0, :N_EXPERTS].astype(jnp.int32)
    nblk = (counts + BM - 1) // BM
    blk_end = jnp.cumsum(nblk)
    pstart = (blk_end - nblk) * BM
    nused = blk_end[-1]
    blk = jnp.arange(NB, dtype=jnp.int32)
    block_expert = jnp.sum((blk[:, None] >= blk_end[None, :]).astype(jnp.int32), axis=1)
    last_expert = jnp.max(jnp.where(nblk > 0, jnp.arange(N_EXPERTS, dtype=jnp.int32), 0))
    block_expert = jnp.where(blk < nused, block_expert, last_expert).astype(jnp.int32)
    expert = r_all[:, 2:2 + TOP_K].astype(jnp.int32)
    rank = r_all[:, 2 + TOP_K:2 + 2 * TOP_K].astype(jnp.int32)
    sel = expert[:, :, None] == jnp.arange(N_EXPERTS, dtype=jnp.int32)[None, None, :]
    slot = rank + jnp.sum(jnp.where(sel, pstart[None, None, :], 0), axis=-1)
    order = jnp.argsort(expert.reshape(N_ASSIGN), stable=True).astype(jnp.int32)
    starts = jnp.cumsum(counts) - counts
    base = starts[block_expert] - pstart[block_expert] + blk * BM
    limit = starts[block_expert] + counts[block_expert] - 1
    return (block_expert, nused.reshape(1).astype(jnp.int32), base.astype(jnp.int32),
            limit.astype(jnp.int32), order, slot.reshape(N_ASSIGN).astype(jnp.int32))


def _tiles_to_rows(h_tiles, start, n):
    return h_tiles.reshape(NTOK, SUBLANES, LANES)[start:start + n].reshape(n, D)


def kernel(x_prompt, x_sample, state_pool, state_conv, meta, w_pool, pool_scale, w_glu, b_glu, w_dw, b_dw, conv_ln_g, conv_ln_b, w_pw, b_pw, ln_mix_g, ln_mix_b, ln_ffn_g, ln_ffn_b, w_router_group, w_router_expert, w_gate, w_up, w_down):
    f32 = jnp.float32
    x_samp2d = x_sample.reshape(DEC_BATCH, D)
    head = jnp.concatenate([jnp.tile(meta.astype(f32), (BATCH, 1)), x_samp2d], axis=0)
    x_in = x_prompt.reshape(BATCH * SEQ, D)

    k_idx = jnp.arange(POOL_STATE)[:, None]
    win = jnp.repeat(jnp.asarray(POOL_WINDOWS), POOL_GROUP_DIM)[None, :]
    pool_coef = (k_idx >= (POOL_STATE + 1 - win)).astype(f32)

    new_pool_p, new_pool_s, new_conv_p, new_conv_s = [], [], [], []
    for i in range(DEPTH):
        j = i // 2
        first = i == 0
        wr32 = jnp.concatenate([w_router_group[i], w_router_expert[i],
                                jnp.zeros((D, LANES - N_GROUPS - N_EXPERTS), f32)], axis=1)
        wr_hi = wr32.astype(jnp.bfloat16)
        wr = jnp.concatenate([wr_hi, (wr32 - wr_hi.astype(f32)).astype(jnp.bfloat16)], axis=1)
        ln_g, ln_b = ln_mix_g[i].reshape(1, D), ln_mix_b[i].reshape(1, D)
        if i % 2 == 0:
            if first:
                tail = x_prompt[:, SEQ - POOL_STATE:]
                x_s = x_samp2d
            else:
                prompt_tiles = x_in.reshape(NTOK, SUBLANES, LANES)[TB:].reshape(BATCH, SEQ, SUBLANES, LANES)
                tail = prompt_tiles[:, SEQ - POOL_STATE:].reshape(BATCH, POOL_STATE, D)
                x_s = _tiles_to_rows(x_in, N_HEAD, DEC_BATCH)
            new_pool_p.append(tail)
            new_pool_s.append(jnp.concatenate([state_pool[j][:, 1:], x_s[:, None, :]], axis=1))
            ps = _state_presum(state_pool[j].reshape(DEC_BATCH, POOL_STATE * D), pool_coef)
            h1, r_all, cnt = _pool_layer(first, head, x_in, ps, w_pool[j].astype(jnp.bfloat16),
                                         pool_scale[j].reshape(1, D), ln_g, ln_b, wr)
        else:
            vs = _state_presum(state_conv[j].reshape(DEC_BATCH, CONV_STATE * D), w_dw[j][:CONV_STATE])
            h1, r_all, cnt, ust, u_s = _conv_layer(
                first, head, x_in, vs, w_glu[j].astype(jnp.bfloat16), b_glu[j].reshape(1, 2 * D), w_dw[j],
                b_dw[j].reshape(1, D), conv_ln_g[j].reshape(1, D), conv_ln_b[j].reshape(1, D),
                w_pw[j].astype(jnp.bfloat16), b_pw[j].reshape(1, D), ln_g, ln_b, wr)
            new_conv_p.append(ust)
            new_conv_s.append(jnp.concatenate([state_conv[j][:, 1:], u_s[:, None, :]], axis=1))
        block_expert, nused, base, limit, order, slot = _routing_metadata(r_all, cnt)
        ys = _expert_layer(i, block_expert, nused, base, limit, order, h1, w_gate, w_up, w_down)
        x_in = _combine_layer(i == DEPTH - 1, slot, h1, r_all, ys,
                              ln_ffn_g[i].reshape(1, D), ln_ffn_b[i].reshape(1, D))

    y_prompt, y_samp = x_in
    return (y_prompt.reshape(BATCH, SEQ, D), y_samp.reshape(DEC_BATCH, 1, D),
            jnp.stack(new_pool_p), jnp.stack(new_pool_s), jnp.stack(new_conv_p), jnp.stack(new_conv_s))
```

```python
import functools

import jax
import jax.numpy as jnp
from jax import lax
from jax.experimental import pallas as pl
from jax.experimental.pallas import tpu as pltpu

D = 1024
BATCH = 8
SEQ = 2048
DEPTH = 4
DEC_BATCH = 128
N_META = 16
POOL_WINDOWS = (2, 4, 8, 16)
POOL_GROUP_DIM = D // len(POOL_WINDOWS)
POOL_STATE = max(POOL_WINDOWS) - 1
CONV_WIDTH = 31
CONV_STATE = CONV_WIDTH - 1
N_GROUPS = 4
EXPERTS_PER_GROUP = 8
N_EXPERTS = N_GROUPS * EXPERTS_PER_GROUP
TOP_K = 2
D_EXPERT = D // 2
ALPHA = (2.0 * DEPTH) ** 0.25
LN_EPS = 1e-5

LANES = 128
SUBLANES = 8
N_HEAD = BATCH * N_META
TB = 256
NTOK = N_HEAD + DEC_BATCH + BATCH * SEQ
N_STEP = NTOK // TB
CHUNKS = SEQ // TB
POOL_CARRY = 16
CONV_CARRY = 32
CONV_RB = 128
CONV_CB = 128
BM = 512
N_ASSIGN = NTOK * TOP_K
NB = N_ASSIGN // BM + N_EXPERTS
N_SLOT = NB * BM
VMEM_LIMIT = 48 * 1024 * 1024
NEG = -1e30

assert D == SUBLANES * LANES and N_HEAD + DEC_BATCH == TB and SEQ % TB == 0
assert N_ASSIGN % BM == 0 and TB % CONV_RB == 0 and N_META == POOL_CARRY and CONV_STATE <= TB


def _tile_load(ref, n):
    return jnp.concatenate([ref[pl.ds(s, n, stride=SUBLANES), :] for s in range(SUBLANES)], axis=1)


def _tile_store(ref, val, n):
    for s in range(SUBLANES):
        ref[pl.ds(s, n, stride=SUBLANES), :] = val[:, s * LANES:(s + 1) * LANES]


def _layer_norm(x, g, b):
    mu = jnp.mean(x, axis=-1, keepdims=True)
    xc = x - mu
    var = jnp.mean(xc * xc, axis=-1, keepdims=True)
    return xc * lax.rsqrt(var + LN_EPS) * g + b


def _residual_route(x, m, g_ref, b_ref, wr_ref, h_ref, r_ref, cnt_out_ref, cnt_ref, ltri_ref):
    @pl.when(pl.program_id(0) == 0)
    def _():
        cnt_ref[...] = jnp.zeros((1, LANES), jnp.float32)
        row = lax.broadcasted_iota(jnp.int32, (TB, TB), 0)
        col = lax.broadcasted_iota(jnp.int32, (TB, TB), 1)
        ltri_ref[...] = jnp.where(col < row, 1.0, 0.0).astype(jnp.bfloat16)

    h = _layer_norm(ALPHA * x + m, g_ref[...], b_ref[...])
    _tile_store(h_ref, h, TB)
    h_hi = h.astype(jnp.bfloat16)
    h_lo = (h - h_hi.astype(jnp.float32)).astype(jnp.bfloat16)
    hi_both = jnp.dot(h_hi, wr_ref[...], preferred_element_type=jnp.float32)
    lo_hi = jnp.dot(h_lo, wr_ref[:, 0:LANES], preferred_element_type=jnp.float32)
    logits = hi_both[:, 0:LANES] + (hi_both[:, LANES:2 * LANES] + lo_hi)
    lane = lax.broadcasted_iota(jnp.int32, logits.shape, 1)
    lanef = lane.astype(jnp.float32)
    big = jnp.float32(1e9)
    lg = jnp.where(lane < N_GROUPS, logits, NEG)
    mg = jnp.max(lg, axis=1, keepdims=True)
    gidx = jnp.min(jnp.where(lg == mg, lanef, big), axis=1, keepdims=True)
    p_grp = 1.0 / jnp.sum(jnp.where(lane < N_GROUPS, jnp.exp(lg - mg), 0.0), axis=1, keepdims=True)
    lo = N_GROUPS + EXPERTS_PER_GROUP * gidx
    le = jnp.where(lanef >= lo, jnp.where(lanef < lo + EXPERTS_PER_GROUP, logits, NEG), NEG)
    m1 = jnp.max(le, axis=1, keepdims=True)
    i1 = jnp.min(jnp.where(le == m1, lanef, big), axis=1, keepdims=True)
    le2 = jnp.where(lanef == i1, NEG, le)
    m2 = jnp.max(le2, axis=1, keepdims=True)
    i2 = jnp.min(jnp.where(le2 == m2, lanef, big), axis=1, keepdims=True)
    ratio = jnp.exp(m2 - m1)
    g1 = p_grp / (1.0 + ratio)
    g2 = g1 * ratio
    e1 = i1 - N_GROUPS
    e2 = i2 - N_GROUPS

    is1 = lanef == e1
    is2 = lanef == e2
    onehot = jnp.where(is1, 1.0, jnp.where(is2, 1.0, 0.0))
    before = jnp.dot(ltri_ref[...], onehot.astype(jnp.bfloat16),
                     preferred_element_type=jnp.float32) + cnt_ref[...]
    rank1 = jnp.sum(jnp.where(is1, before, 0.0), axis=1, keepdims=True)
    rank2 = jnp.sum(jnp.where(is2, before, 0.0), axis=1, keepdims=True)
    cnt = cnt_ref[...] + jnp.sum(onehot, axis=0, keepdims=True)
    cnt_ref[...] = cnt
    cnt_out_ref[...] = cnt

    r_ref[...] = jnp.where(lane == 0, g1,
                           jnp.where(lane == 1, g2,
                                     jnp.where(lane == 2, e1,
                                               jnp.where(lane == 3, e2,
                                                         jnp.where(lane == 4, rank1,
                                                                   jnp.where(lane == 5, rank2, 0.0))))))


def _load_block(first_layer, head_ref, x_ref):
    if not first_layer:
        return _tile_load(x_ref, TB)
    return jnp.where(pl.program_id(0) == 0, head_ref[...], x_ref[...])


def _seq_of_step(s):
    sm1 = jnp.maximum(s - 1, 0)
    return lax.shift_right_logical(sm1, CHUNKS.bit_length() - 1), jnp.bitwise_and(sm1, CHUNKS - 1)


assert CHUNKS & (CHUNKS - 1) == 0


def _presum_kernel(st_ref, coef_ref, o_ref, *, n_rows):
    acc = coef_ref[0:1, :] * st_ref[:, 0:D]
    for k in range(1, n_rows):
        acc = acc + coef_ref[k:k + 1, :] * st_ref[:, k * D:(k + 1) * D]
    o_ref[...] = acc


def _state_presum(state2d, coef):
    n, kd = state2d.shape
    k = kd // D
    rows = 32
    return pl.pallas_call(
        functools.partial(_presum_kernel, n_rows=k),
        grid=(n // rows,),
        in_specs=[pl.BlockSpec((rows, kd), lambda i: (i, 0)),
                  pl.BlockSpec((k, D), lambda i: (0, 0))],
        out_specs=pl.BlockSpec((rows, D), lambda i: (i, 0)),
        out_shape=jax.ShapeDtypeStruct((n, D), jnp.float32),
        compiler_params=pltpu.CompilerParams(dimension_semantics=("arbitrary",),
                                             vmem_limit_bytes=VMEM_LIMIT),
        name="state_presum",
    )(state2d, coef)


def _pool_kernel(first_layer, head_ref, x_ref, ps_ref, wp_ref, sc_ref, g_ref, b_ref, wr_ref,
                 h_ref, r_ref, cnt_out_ref, ext_ref, m_ref, carry_ref, cnt_ref, ltri_ref):
    s = pl.program_id(0)
    seq, chunk = _seq_of_step(s)
    x = _load_block(first_layer, head_ref, x_ref)
    ext_ref[POOL_CARRY:POOL_CARRY + TB, :] = x

    @pl.when(s == 0)
    def _():
        ext_ref[0:POOL_CARRY, :] = jnp.zeros((POOL_CARRY, D), jnp.float32)
        carry_ref[...] = x[0:N_HEAD, :]
        pos = jnp.bitwise_and(lax.broadcasted_iota(jnp.int32, (N_HEAD, 1), 0), N_META - 1)
        for g, w in enumerate(POOL_WINDOWS):
            lo, hi = g * POOL_GROUP_DIM, (g + 1) * POOL_GROUP_DIM
            xm = x[0:N_HEAD, lo:hi]
            acc = xm
            for k in range(1, w):
                acc = acc + jnp.where(pos >= k, ext_ref[POOL_CARRY - k:POOL_CARRY - k + N_HEAD, lo:hi], 0.0)
            cnt = jnp.minimum(pos + 1, w).astype(jnp.float32)
            d_meta = acc / cnt - xm
            xs = x[N_HEAD:TB, lo:hi]
            d_samp = (ps_ref[:, lo:hi] + xs) / float(w) - xs
            diff = jnp.concatenate([d_meta, d_samp], axis=0)
            y = jnp.dot(diff.astype(jnp.bfloat16), wp_ref[g], preferred_element_type=jnp.float32)
            m_ref[:, lo:hi] = y * sc_ref[:, lo:hi]

    @pl.when(s > 0)
    def _():
        @pl.when(chunk == 0)
        def _():
            ext_ref[0:POOL_CARRY, :] = carry_ref[pl.ds(pl.multiple_of(seq * N_META, N_META), N_META), :]

        for g, w in enumerate(POOL_WINDOWS):
            lo, hi = g * POOL_GROUP_DIM, (g + 1) * POOL_GROUP_DIM
            xg = x[:, lo:hi]
            acc = xg
            for k in range(1, w):
                acc = acc + ext_ref[POOL_CARRY - k:POOL_CARRY - k + TB, lo:hi]
            diff = acc / float(w) - xg
            y = jnp.dot(diff.astype(jnp.bfloat16), wp_ref[g], preferred_element_type=jnp.float32)
            m_ref[:, lo:hi] = y * sc_ref[:, lo:hi]
        ext_ref[0:POOL_CARRY, :] = x[TB - POOL_CARRY:TB, :]

    _residual_route(x, m_ref[...], g_ref, b_ref, wr_ref, h_ref, r_ref, cnt_out_ref, cnt_ref, ltri_ref)


assert N_META >= max(POOL_WINDOWS)


def _const_spec(shape):
    nd = len(shape)
    return pl.BlockSpec(shape, lambda s: (0,) * nd)


def _x_spec(first_layer):
    if first_layer:
        return pl.BlockSpec((TB, D), lambda s: (jnp.maximum(s - 1, 0), 0))
    return pl.BlockSpec((TB * SUBLANES, LANES), lambda s: (s, 0))


_MIXER_OUT_SPECS = [pl.BlockSpec((TB * SUBLANES, LANES), lambda s: (s, 0)),
                    pl.BlockSpec((TB, LANES), lambda s: (s, 0)),
                    pl.BlockSpec((1, LANES), lambda s: (0, 0))]
_MIXER_OUT_SHAPES = [jax.ShapeDtypeStruct((NTOK * SUBLANES, LANES), jnp.float32),
                     jax.ShapeDtypeStruct((NTOK, LANES), jnp.float32),
                     jax.ShapeDtypeStruct((1, LANES), jnp.float32)]


def _pool_layer(first_layer, head, x, ps, wp_bf, scale, ln_g, ln_b, wr):
    return pl.pallas_call(
        functools.partial(_pool_kernel, first_layer),
        grid=(N_STEP,),
        in_specs=[_const_spec((TB, D)), _x_spec(first_layer),
                  _const_spec((DEC_BATCH, D)),
                  _const_spec((len(POOL_WINDOWS), POOL_GROUP_DIM, POOL_GROUP_DIM)),
                  _const_spec((1, D)), _const_spec((1, D)), _const_spec((1, D)),
                  _const_spec((D, 2 * LANES))],
        out_specs=_MIXER_OUT_SPECS,
        out_shape=_MIXER_OUT_SHAPES,
        scratch_shapes=[pltpu.VMEM((POOL_CARRY + TB, D), jnp.float32),
                        pltpu.VMEM((TB, D), jnp.float32),
                        pltpu.VMEM((N_HEAD, D), jnp.float32),
                        pltpu.VMEM((1, LANES), jnp.float32),
                        pltpu.VMEM((TB, TB), jnp.bfloat16)],
        compiler_params=pltpu.CompilerParams(dimension_semantics=("arbitrary",),
                                             vmem_limit_bytes=VMEM_LIMIT),
        name="pool_mixer",
    )(head, x, ps, wp_bf, scale, ln_g, ln_b, wr)


def _depthwise_conv(ext_ref, wdw_ref, v_ref):
    base = CONV_CARRY - CONV_STATE
    for cb in range(D // CONV_CB):
        lo, hi = cb * CONV_CB, (cb + 1) * CONV_CB

        def body(i, carry, lo=lo, hi=hi):
            r0 = pl.multiple_of(i * CONV_RB, CONV_RB)
            sub = ext_ref.at[pl.ds(r0, CONV_RB + CONV_CARRY), :]
            v = None
            for r in range(SUBLANES):
                qs = [q for q in range((base + CONV_WIDTH) // SUBLANES + 1)
                      if 0 <= SUBLANES * q + r - base < CONV_WIDTH]
                z0 = SUBLANES * qs[0] + r
                z = sub[z0:SUBLANES * qs[-1] + r + CONV_RB, lo:hi]
                p = None
                for q in qs:
                    k = SUBLANES * q + r - base
                    off = SUBLANES * (q - qs[0])
                    term = wdw_ref[k:k + 1, lo:hi] * z[off:off + CONV_RB]
                    p = term if p is None else p + term
                v = p if v is None else v + p
            v_ref[pl.ds(r0, CONV_RB), lo:hi] = v
            return carry
        lax.fori_loop(0, TB // CONV_RB, body, 0)


def _conv_kernel(first_layer, head_ref, x_ref, vs_ref, wglu_ref, bglu_ref, wdw_ref, bdw_ref, lg_ref, lb_ref,
                 wpw_ref, bpw_ref, g_ref, b_ref, wr_ref, h_ref, r_ref, cnt_out_ref, ust_ref, us_ref,
                 ext_ref, v_ref, carry_ref, cnt_ref, ltri_ref):
    s = pl.program_id(0)
    seq, chunk = _seq_of_step(s)
    x = _load_block(first_layer, head_ref, x_ref)
    hh = jnp.dot(x.astype(jnp.bfloat16), wglu_ref[...], preferred_element_type=jnp.float32) + bglu_ref[...]
    u = hh[:, :D] * jax.nn.sigmoid(hh[:, D:])

    @pl.when(s == 0)
    def _():
        um = u[0:N_HEAD, :]
        us = u[N_HEAD:TB, :]
        carry_ref[...] = um
        us_ref[...] = us
        ext_ref[0:N_META, :] = jnp.zeros((N_META, D), jnp.float32)
        ext_ref[N_META:N_META + N_HEAD, :] = um
        pos = jnp.bitwise_and(lax.broadcasted_iota(jnp.int32, (N_HEAD, 1), 0), N_META - 1)
        acc = wdw_ref[CONV_WIDTH - 1:CONV_WIDTH, :] * um
        for d in range(1, N_META):
            k = CONV_WIDTH - 1 - d
            acc = acc + wdw_ref[k:k + 1, :] * jnp.where(pos >= d, ext_ref[N_META - d:N_META - d + N_HEAD, :], 0.0)
        v_ref[0:N_HEAD, :] = acc
        v_ref[N_HEAD:TB, :] = vs_ref[...] + wdw_ref[CONV_WIDTH - 1:CONV_WIDTH, :] * us

    @pl.when(s > 0)
    def _():
        @pl.when(chunk == 0)
        def _():
            ext_ref[0:CONV_CARRY - N_META, :] = jnp.zeros((CONV_CARRY - N_META, D), jnp.float32)
            ext_ref[CONV_CARRY - N_META:CONV_CARRY, :] = carry_ref[
                pl.ds(pl.multiple_of(seq * N_META, N_META), N_META), :]

        ext_ref[CONV_CARRY:CONV_CARRY + TB, :] = u
        _depthwise_conv(ext_ref, wdw_ref, v_ref)
        ext_ref[0:CONV_CARRY, :] = ext_ref[TB:TB + CONV_CARRY, :]

        @pl.when(chunk == CHUNKS - 1)
        def _():
            ust_ref[0] = ext_ref[CONV_CARRY + TB - CONV_STATE:CONV_CARRY + TB, :]

    v = v_ref[...] + bdw_ref[...]
    v = _layer_norm(v, lg_ref[...], lb_ref[...])
    v = v * jax.nn.sigmoid(v)
    m = jnp.dot(v.astype(jnp.bfloat16), wpw_ref[...], preferred_element_type=jnp.float32) + bpw_ref[...]
    _residual_route(x, m, g_ref, b_ref, wr_ref, h_ref, r_ref, cnt_out_ref, cnt_ref, ltri_ref)


assert CONV_CARRY - N_META + N_META >= CONV_STATE and N_META <= CONV_STATE


def _conv_layer(first_layer, head, x, vs, wglu_bf, bglu, wdw, bdw, cln_g, cln_b, wpw_bf, bpw, ln_g, ln_b, wr):
    return pl.pallas_call(
        functools.partial(_conv_kernel, first_layer),
        grid=(N_STEP,),
        in_specs=[_const_spec((TB, D)), _x_spec(first_layer),
                  _const_spec((DEC_BATCH, D)),
                  _const_spec((D, 2 * D)), _const_spec((1, 2 * D)),
                  _const_spec((CONV_WIDTH, D)), _const_spec((1, D)),
                  _const_spec((1, D)), _const_spec((1, D)),
                  _const_spec((D, D)), _const_spec((1, D)),
                  _const_spec((1, D)), _const_spec((1, D)),
                  _const_spec((D, 2 * LANES))],
        out_specs=_MIXER_OUT_SPECS + [
            pl.BlockSpec((1, CONV_STATE, D), lambda s: (jnp.maximum(s - 1, 0) // CHUNKS, 0, 0)),
            _const_spec((DEC_BATCH, D))],
        out_shape=_MIXER_OUT_SHAPES + [
            jax.ShapeDtypeStruct((BATCH, CONV_STATE, D), jnp.float32),
            jax.ShapeDtypeStruct((DEC_BATCH, D), jnp.float32)],
        scratch_shapes=[pltpu.VMEM((CONV_CARRY + TB, D), jnp.float32),
                        pltpu.VMEM((TB, D), jnp.float32),
                        pltpu.VMEM((N_HEAD, D), jnp.float32),
                        pltpu.VMEM((1, LANES), jnp.float32),
                        pltpu.VMEM((TB, TB), jnp.bfloat16)],
        compiler_params=pltpu.CompilerParams(dimension_semantics=("arbitrary",),
                                             vmem_limit_bytes=VMEM_LIMIT),
        name="conv_mixer",
    )(head, x, vs, wglu_bf, bglu, wdw, bdw, cln_g, cln_b, wpw_bf, bpw, ln_g, ln_b, wr)


def _tile_gather_start(src_hbm, dst, sem, row_index, n_rows, dst_tile=lambda r: r):
    for r in range(n_rows):
        tok = row_index(r)
        pltpu.make_async_copy(src_hbm.at[pl.ds(pl.multiple_of(tok * SUBLANES, SUBLANES), SUBLANES), :],
                              dst.at[pl.ds(dst_tile(r) * SUBLANES, SUBLANES), :], sem).start(priority=r % 2)


def _tile_gather_wait(src_hbm, dst, sem, n_rows):
    pltpu.make_async_copy(src_hbm.at[pl.ds(0, n_rows * SUBLANES), :], dst, sem).wait()


N_XBUF = 3

assert TOP_K == 2


def _expert_kernel(be_ref, nused_ref, base_ref, limit_ref, order_ref, x_hbm, wg_ref, wu_ref, wd_ref, ys_ref,
                   xbuf, wgb, wub, wdb, sem):
    i = pl.program_id(0)
    nused = nused_ref[0]
    last = jnp.maximum(nused - 1, 0)

    def issue(blk, buf):
        b0 = base_ref[blk]
        lim = limit_ref[blk]
        _tile_gather_start(
            x_hbm, xbuf.at[buf], sem.at[buf],
            lambda r: lax.shift_right_logical(order_ref[jnp.minimum(b0 + r, lim)], 1), BM)

    @pl.when(jnp.logical_and(i == 0, nused > 0))
    def _():
        issue(0, 0)
        issue(jnp.minimum(1, last), 1)

    @pl.when(i < nused)
    def _():
        buf = lax.rem(i, N_XBUF)
        _tile_gather_wait(x_hbm, xbuf.at[buf], sem.at[buf], BM)

        @pl.when(jnp.logical_or(i == 0, be_ref[i] != be_ref[jnp.maximum(i - 1, 0)]))
        def _():
            wgb[...] = wg_ref[0, 0].astype(jnp.bfloat16)
            wub[...] = wu_ref[0, 0].astype(jnp.bfloat16)
            wdb[...] = wd_ref[0, 0].astype(jnp.bfloat16)

        xb = _tile_load(xbuf.at[buf], BM).astype(jnp.bfloat16)

        issue(jnp.minimum(i + 2, last), lax.rem(i + 2, N_XBUF))

        gate = jnp.dot(xb, wgb[...], preferred_element_type=jnp.float32)
        up = jnp.dot(xb, wub[...], preferred_element_type=jnp.float32)
        hid = gate * jax.nn.sigmoid(gate) * up
        y = jnp.dot(hid.astype(jnp.bfloat16), wdb[...], preferred_element_type=jnp.float32)
        _tile_store(ys_ref, y, BM)

        @pl.when(i == nused - 1)
        def _():
            for ahead in (1, 2):
                b = lax.rem(i + ahead, N_XBUF)
                _tile_gather_wait(x_hbm, xbuf.at[b], sem.at[b], BM)

    @pl.when(i >= nused)
    def _():
        ys_ref[...] = jnp.zeros((BM * SUBLANES, LANES), jnp.float32)


def _expert_layer(layer, block_expert, nused, base, limit, order, h_tiles, w_gate, w_up, w_down):
    def w_spec(shape):
        return pl.BlockSpec((1, 1) + shape, lambda i, be, *_: (layer, be[i], 0, 0))
    grid_spec = pltpu.PrefetchScalarGridSpec(
        num_scalar_prefetch=5,
        grid=(NB,),
        in_specs=[pl.BlockSpec(memory_space=pl.ANY),
                  w_spec((D, D_EXPERT)), w_spec((D, D_EXPERT)), w_spec((D_EXPERT, D))],
        out_specs=pl.BlockSpec((BM * SUBLANES, LANES), lambda i, *_: (i, 0)),
        scratch_shapes=[pltpu.VMEM((N_XBUF, BM * SUBLANES, LANES), jnp.float32),
                        pltpu.VMEM((D, D_EXPERT), jnp.bfloat16),
                        pltpu.VMEM((D, D_EXPERT), jnp.bfloat16),
                        pltpu.VMEM((D_EXPERT, D), jnp.bfloat16),
                        pltpu.SemaphoreType.DMA((N_XBUF,))],
    )
    return pl.pallas_call(
        _expert_kernel,
        grid_spec=grid_spec,
        out_shape=jax.ShapeDtypeStruct((N_SLOT * SUBLANES, LANES), jnp.float32),
        compiler_params=pltpu.CompilerParams(dimension_semantics=("arbitrary",),
                                             vmem_limit_bytes=VMEM_LIMIT),
        name="expert_mlp",
    )(block_expert, nused, base, limit, order, h_tiles, w_gate, w_up, w_down)


def _combine_kernel(last_layer, slot_ref, h_ref, r_ref, ys_hbm, g_ref, b_ref, *rest):
    if last_layer:
        yp_ref, ysamp_ref, ybuf, sem = rest
    else:
        o_ref, ybuf, sem = rest
    i = pl.program_id(0)
    n = pl.num_programs(0)

    def start(step, buf):
        base = step * (TOP_K * TB)
        _tile_gather_start(ys_hbm, ybuf.at[buf], sem.at[buf], lambda r: slot_ref[base + r], TOP_K * TB,
                           dst_tile=lambda r: (r % TOP_K) * TB + r // TOP_K)

    @pl.when(i == 0)
    def _():
        start(0, 0)

    buf = i % 2
    _tile_gather_wait(ys_hbm, ybuf.at[buf], sem.at[buf], TOP_K * TB)
    yb = ybuf.at[buf]
    y0 = _tile_load(yb.at[pl.ds(0, TB * SUBLANES), :], TB)
    y1 = _tile_load(yb.at[pl.ds(TB * SUBLANES, TB * SUBLANES), :], TB)
    h = _tile_load(h_ref, TB)
    r = r_ref[...]

    start(jnp.minimum(i + 1, n - 1), 1 - buf)

    f = r[:, 0:1] * y0 + r[:, 1:2] * y1
    out = _layer_norm(ALPHA * h + f, g_ref[...], b_ref[...])
    if last_layer:
        yp_ref[...] = out

        @pl.when(i == 0)
        def _():
            ysamp_ref[...] = out[N_HEAD:TB, :]
    else:
        _tile_store(o_ref, out, TB)

    @pl.when(i == n - 1)
    def _():
        _tile_gather_wait(ys_hbm, ybuf.at[1 - buf], sem.at[1 - buf], TOP_K * TB)


def _combine_layer(last_layer, slot, h_tiles, r_all, ys, ln_g, ln_b):
    if last_layer:
        out_specs = [pl.BlockSpec((TB, D), lambda i, sl: (jnp.maximum(i - 1, 0), 0)),
                     pl.BlockSpec((DEC_BATCH, D), lambda i, sl: (0, 0))]
        out_shape = [jax.ShapeDtypeStruct((BATCH * SEQ, D), jnp.float32),
                     jax.ShapeDtypeStruct((DEC_BATCH, D), jnp.float32)]
    else:
        out_specs = pl.BlockSpec((TB * SUBLANES, LANES), lambda i, sl: (i, 0))
        out_shape = jax.ShapeDtypeStruct((NTOK * SUBLANES, LANES), jnp.float32)
    grid_spec = pltpu.PrefetchScalarGridSpec(
        num_scalar_prefetch=1,
        grid=(N_STEP,),
        in_specs=[pl.BlockSpec((TB * SUBLANES, LANES), lambda i, sl: (i, 0)),
                  pl.BlockSpec((TB, LANES), lambda i, sl: (i, 0)),
                  pl.BlockSpec(memory_space=pl.ANY),
                  pl.BlockSpec((1, D), lambda i, sl: (0, 0)),
                  pl.BlockSpec((1, D), lambda i, sl: (0, 0))],
        out_specs=out_specs,
        scratch_shapes=[pltpu.VMEM((2, TOP_K * TB * SUBLANES, LANES), jnp.float32),
                        pltpu.SemaphoreType.DMA((2,))],
    )
    return pl.pallas_call(
        functools.partial(_combine_kernel, last_layer),
        grid_spec=grid_spec,
        out_shape=out_shape,
        compiler_params=pltpu.CompilerParams(dimension_semantics=("arbitrary",),
                                             vmem_limit_bytes=VMEM_LIMIT),
        name="moe_combine",
    )(slot, h_tiles, r_all, ys, ln_g, ln_b)


def _routing_metadata(r_all, cnt):
    counts = cnt[0, :N_EXPERTS].astype(jnp.int32)
    nblk = (counts + BM - 1) // BM
    blk_end = jnp.cumsum(nblk)
    pstart = (blk_end - nblk) * BM
    nused = blk_end[-1]
    blk = jnp.arange(NB, dtype=jnp.int32)
    block_expert = jnp.sum((blk[:, None] >= blk_end[None, :]).astype(jnp.int32), axis=1)
    last_expert = jnp.max(jnp.where(nblk > 0, jnp.arange(N_EXPERTS, dtype=jnp.int32), 0))
    block_expert = jnp.where(blk < nused, block_expert, last_expert).astype(jnp.int32)
    expert = r_all[:, 2:2 + TOP_K].astype(jnp.int32)
    rank = r_all[:, 2 + TOP_K:2 + 2 * TOP_K].astype(jnp.int32)
    sel = expert[:, :, None] == jnp.arange(N_EXPERTS, dtype=jnp.int32)[None, None, :]
    slot = rank + jnp.sum(jnp.where(sel, pstart[None, None, :], 0), axis=-1)
    order = jnp.argsort(expert.reshape(N_ASSIGN), stable=True).astype(jnp.int32)
    starts = jnp.cumsum(counts) - counts
    base = starts[block_expert] - pstart[block_expert] + blk * BM
    limit = starts[block_expert] + counts[block_expert] - 1
    return (block_expert, nused.reshape(1).astype(jnp.int32), base.astype(jnp.int32),
            limit.astype(jnp.int32), order, slot.reshape(N_ASSIGN).astype(jnp.int32))


def _tiles_to_rows(h_tiles, start, n):
    return h_tiles.reshape(NTOK, SUBLANES, LANES)[start:start + n].reshape(n, D)


def kernel(x_prompt, x_sample, state_pool, state_conv, meta, w_pool, pool_scale, w_glu, b_glu, w_dw, b_dw, conv_ln_g, conv_ln_b, w_pw, b_pw, ln_mix_g, ln_mix_b, ln_ffn_g, ln_ffn_b, w_router_group, w_router_expert, w_gate, w_up, w_down):
    f32 = jnp.float32
    x_samp2d = x_sample.reshape(DEC_BATCH, D)
    head = jnp.concatenate([jnp.tile(meta.astype(f32), (BATCH, 1)), x_samp2d], axis=0)
    x_in = x_prompt.reshape(BATCH * SEQ, D)

    k_idx = jnp.arange(POOL_STATE)[:, None]
    win = jnp.repeat(jnp.asarray(POOL_WINDOWS), POOL_GROUP_DIM)[None, :]
    pool_coef = (k_idx >= (POOL_STATE + 1 - win)).astype(f32)

    new_pool_p, new_pool_s, new_conv_p, new_conv_s = [], [], [], []
    for i in range(DEPTH):
        j = i // 2
        first = i == 0
        wr32 = jnp.concatenate([w_router_group[i], w_router_expert[i],
                                jnp.zeros((D, LANES - N_GROUPS - N_EXPERTS), f32)], axis=1)
        wr_hi = wr32.astype(jnp.bfloat16)
        wr = jnp.concatenate([wr_hi, (wr32 - wr_hi.astype(f32)).astype(jnp.bfloat16)], axis=1)
        ln_g, ln_b = ln_mix_g[i].reshape(1, D), ln_mix_b[i].reshape(1, D)
        if i % 2 == 0:
            if first:
                tail = x_prompt[:, SEQ - POOL_STATE:]
                x_s = x_samp2d
            else:
                tiles = x_in.reshape(NTOK, SUBLANES, LANES)
                tail = jnp.stack([tiles[TB + (b + 1) * SEQ - POOL_STATE:TB + (b + 1) * SEQ]
                                  for b in range(BATCH)]).reshape(BATCH, POOL_STATE, D)
                x_s = _tiles_to_rows(x_in, N_HEAD, DEC_BATCH)
            new_pool_p.append(tail)
            new_pool_s.append(jnp.concatenate([state_pool[j][:, 1:], x_s[:, None, :]], axis=1))
            ps = _state_presum(state_pool[j].reshape(DEC_BATCH, POOL_STATE * D), pool_coef)
            h1, r_all, cnt = _pool_layer(first, head, x_in, ps, w_pool[j].astype(jnp.bfloat16),
                                         pool_scale[j].reshape(1, D), ln_g, ln_b, wr)
        else:
            vs = _state_presum(state_conv[j].reshape(DEC_BATCH, CONV_STATE * D), w_dw[j][:CONV_STATE])
            h1, r_all, cnt, ust, u_s = _conv_layer(
                first, head, x_in, vs, w_glu[j].astype(jnp.bfloat16), b_glu[j].reshape(1, 2 * D), w_dw[j],
                b_dw[j].reshape(1, D), conv_ln_g[j].reshape(1, D), conv_ln_b[j].reshape(1, D),
                w_pw[j].astype(jnp.bfloat16), b_pw[j].reshape(1, D), ln_g, ln_b, wr)
            new_conv_p.append(ust)
            new_conv_s.append(jnp.concatenate([state_conv[j][:, 1:], u_s[:, None, :]], axis=1))
        block_expert, nused, base, limit, order, slot = _routing_metadata(r_all, cnt)
        ys = _expert_layer(i, block_expert, nused, base, limit, order, h1, w_gate, w_up, w_down)
        x_in = _combine_layer(i == DEPTH - 1, slot, h1, r_all, ys,
                              ln_ffn_g[i].reshape(1, D), ln_ffn_b[i].reshape(1, D))

    y_prompt, y_samp = x_in
    return (y_prompt.reshape(BATCH, SEQ, D), y_samp.reshape(DEC_BATCH, 1, D),
            jnp.stack(new_pool_p), jnp.stack(new_pool_s), jnp.stack(new_conv_p), jnp.stack(new_conv_s))
```

```python
import functools

import jax
import jax.numpy as jnp
from jax import lax
from jax.experimental import pallas as pl
from jax.experimental.pallas import tpu as pltpu

D = 1024
BATCH = 8
SEQ = 2048
DEPTH = 4
DEC_BATCH = 128
N_META = 16
POOL_WINDOWS = (2, 4, 8, 16)
POOL_GROUP_DIM = D // len(POOL_WINDOWS)
POOL_STATE = max(POOL_WINDOWS) - 1
CONV_WIDTH = 31
CONV_STATE = CONV_WIDTH - 1
N_GROUPS = 4
EXPERTS_PER_GROUP = 8
N_EXPERTS = N_GROUPS * EXPERTS_PER_GROUP
TOP_K = 2
D_EXPERT = D // 2
ALPHA = (2.0 * DEPTH) ** 0.25
LN_EPS = 1e-5

LANES = 128
SUBLANES = 8
N_HEAD = BATCH * N_META
TB = 256
NTOK = N_HEAD + DEC_BATCH + BATCH * SEQ
N_STEP = NTOK // TB
CHUNKS = SEQ // TB
POOL_CARRY = 16
CONV_CARRY = 32
CONV_RB = 128
CONV_CB = 128
BM = 256
N_ASSIGN = NTOK * TOP_K
NB = N_ASSIGN // BM + N_EXPERTS
N_SLOT = NB * BM
VMEM_LIMIT = 48 * 1024 * 1024
EXPERT_VMEM_LIMIT = 56 * 1024 * 1024
NEG = -1e30

assert D == SUBLANES * LANES and N_HEAD + DEC_BATCH == TB and SEQ % TB == 0
assert N_ASSIGN % BM == 0 and TB % CONV_RB == 0 and N_META == POOL_CARRY and CONV_STATE <= TB


def _tile_load(ref, n):
    return jnp.concatenate([ref[pl.ds(s, n, stride=SUBLANES), :] for s in range(SUBLANES)], axis=1)


def _tile_store(ref, val, n):
    for s in range(SUBLANES):
        ref[pl.ds(s, n, stride=SUBLANES), :] = val[:, s * LANES:(s + 1) * LANES]


def _layer_norm(x, g, b):
    mu = jnp.mean(x, axis=-1, keepdims=True)
    xc = x - mu
    var = jnp.mean(xc * xc, axis=-1, keepdims=True)
    return xc * lax.rsqrt(var + LN_EPS) * g + b


PACK_ROWS = D // 2 // LANES


def _pack_bf16(h_bf16, hp_ref):
    bits = pltpu.bitcast(h_bf16.astype(jnp.float32), jnp.uint32)
    words = jnp.bitwise_or(lax.shift_right_logical(bits[:, :D // 2], jnp.uint32(16)),
                           jnp.bitwise_and(bits[:, D // 2:], jnp.uint32(0xFFFF0000)))
    for s in range(PACK_ROWS):
        hp_ref[:, s, :] = words[:, s * LANES:(s + 1) * LANES]


def _unpack_bf16(words):
    lo = pltpu.bitcast(lax.shift_left(words, jnp.uint32(16)), jnp.float32)
    hi = pltpu.bitcast(jnp.bitwise_and(words, jnp.uint32(0xFFFF0000)), jnp.float32)
    return jnp.concatenate([lo, hi], axis=1).astype(jnp.bfloat16)


def _residual_route(x, m, g_ref, b_ref, wr_ref, h_ref, r_ref, cnt_out_ref, hp_ref, cnt_ref, ltri_ref):
    @pl.when(pl.program_id(0) == 0)
    def _():
        cnt_ref[...] = jnp.zeros((1, LANES), jnp.float32)
        row = lax.broadcasted_iota(jnp.int32, (TB, TB), 0)
        col = lax.broadcasted_iota(jnp.int32, (TB, TB), 1)
        ltri_ref[...] = jnp.where(col < row, 1.0, 0.0).astype(jnp.bfloat16)

    h = _layer_norm(ALPHA * x + m, g_ref[...], b_ref[...])
    h_ref[...] = h
    h_hi = h.astype(jnp.bfloat16)
    _pack_bf16(h_hi, hp_ref)
    h_lo = (h - h_hi.astype(jnp.float32)).astype(jnp.bfloat16)
    hi_both = jnp.dot(h_hi, wr_ref[...], preferred_element_type=jnp.float32)
    lo_hi = jnp.dot(h_lo, wr_ref[:, 0:LANES], preferred_element_type=jnp.float32)
    logits = hi_both[:, 0:LANES] + (hi_both[:, LANES:2 * LANES] + lo_hi)
    lane = lax.broadcasted_iota(jnp.int32, logits.shape, 1)
    lanef = lane.astype(jnp.float32)
    big = jnp.float32(1e9)
    lg = jnp.where(lane < N_GROUPS, logits, NEG)
    mg = jnp.max(lg, axis=1, keepdims=True)
    gidx = jnp.min(jnp.where(lg == mg, lanef, big), axis=1, keepdims=True)
    p_grp = 1.0 / jnp.sum(jnp.where(lane < N_GROUPS, jnp.exp(lg - mg), 0.0), axis=1, keepdims=True)
    lo = N_GROUPS + EXPERTS_PER_GROUP * gidx
    le = jnp.where(lanef >= lo, jnp.where(lanef < lo + EXPERTS_PER_GROUP, logits, NEG), NEG)
    m1 = jnp.max(le, axis=1, keepdims=True)
    i1 = jnp.min(jnp.where(le == m1, lanef, big), axis=1, keepdims=True)
    le2 = jnp.where(lanef == i1, NEG, le)
    m2 = jnp.max(le2, axis=1, keepdims=True)
    i2 = jnp.min(jnp.where(le2 == m2, lanef, big), axis=1, keepdims=True)
    ratio = jnp.exp(m2 - m1)
    g1 = p_grp / (1.0 + ratio)
    g2 = g1 * ratio
    e1 = i1 - N_GROUPS
    e2 = i2 - N_GROUPS

    is1 = lanef == e1
    is2 = lanef == e2
    onehot = jnp.where(is1, 1.0, jnp.where(is2, 1.0, 0.0))
    before = jnp.dot(ltri_ref[...], onehot.astype(jnp.bfloat16),
                     preferred_element_type=jnp.float32) + cnt_ref[...]
    rank1 = jnp.sum(jnp.where(is1, before, 0.0), axis=1, keepdims=True)
    rank2 = jnp.sum(jnp.where(is2, before, 0.0), axis=1, keepdims=True)
    cnt = cnt_ref[...] + jnp.sum(onehot, axis=0, keepdims=True)
    cnt_ref[...] = cnt
    cnt_out_ref[...] = cnt

    r_ref[...] = jnp.where(lane == 0, g1,
                           jnp.where(lane == 1, g2,
                                     jnp.where(lane == 2, e1,
                                               jnp.where(lane == 3, e2,
                                                         jnp.where(lane == 4, rank1,
                                                                   jnp.where(lane == 5, rank2, 0.0))))))


def _load_block(first_layer, head_ref, x_ref):
    if not first_layer:
        return x_ref[...]
    return jnp.where(pl.program_id(0) == 0, head_ref[...], x_ref[...])


def _seq_of_step(s):
    sm1 = jnp.maximum(s - 1, 0)
    return lax.shift_right_logical(sm1, CHUNKS.bit_length() - 1), jnp.bitwise_and(sm1, CHUNKS - 1)


assert CHUNKS & (CHUNKS - 1) == 0


def _presum_kernel(st_ref, coef_ref, o_ref, *, n_rows):
    acc = coef_ref[0:1, :] * st_ref[:, 0:D]
    for k in range(1, n_rows):
        acc = acc + coef_ref[k:k + 1, :] * st_ref[:, k * D:(k + 1) * D]
    o_ref[...] = acc


def _state_presum(state2d, coef):
    n, kd = state2d.shape
    k = kd // D
    rows = 32
    return pl.pallas_call(
        functools.partial(_presum_kernel, n_rows=k),
        grid=(n // rows,),
        in_specs=[pl.BlockSpec((rows, kd), lambda i: (i, 0)),
                  pl.BlockSpec((k, D), lambda i: (0, 0))],
        out_specs=pl.BlockSpec((rows, D), lambda i: (i, 0)),
        out_shape=jax.ShapeDtypeStruct((n, D), jnp.float32),
        compiler_params=pltpu.CompilerParams(dimension_semantics=("arbitrary",),
                                             vmem_limit_bytes=VMEM_LIMIT),
        name="state_presum",
    )(state2d, coef)


def _pool_kernel(first_layer, head_ref, x_ref, ps_ref, wp_ref, sc_ref, g_ref, b_ref, wr_ref,
                 h_ref, r_ref, cnt_out_ref, hp_ref, ext_ref, m_ref, carry_ref, cnt_ref, ltri_ref):
    s = pl.program_id(0)
    seq, chunk = _seq_of_step(s)
    x = _load_block(first_layer, head_ref, x_ref)
    ext_ref[POOL_CARRY:POOL_CARRY + TB, :] = x

    @pl.when(s == 0)
    def _():
        ext_ref[0:POOL_CARRY, :] = jnp.zeros((POOL_CARRY, D), jnp.float32)
        carry_ref[...] = x[0:N_HEAD, :]
        pos = jnp.bitwise_and(lax.broadcasted_iota(jnp.int32, (N_HEAD, 1), 0), N_META - 1)
        for g, w in enumerate(POOL_WINDOWS):
            lo, hi = g * POOL_GROUP_DIM, (g + 1) * POOL_GROUP_DIM
            xm = x[0:N_HEAD, lo:hi]
            acc = xm
            for k in range(1, w):
                acc = acc + jnp.where(pos >= k, ext_ref[POOL_CARRY - k:POOL_CARRY - k + N_HEAD, lo:hi], 0.0)
            cnt = jnp.minimum(pos + 1, w).astype(jnp.float32)
            d_meta = acc / cnt - xm
            xs = x[N_HEAD:TB, lo:hi]
            d_samp = (ps_ref[:, lo:hi] + xs) / float(w) - xs
            diff = jnp.concatenate([d_meta, d_samp], axis=0)
            y = jnp.dot(diff.astype(jnp.bfloat16), wp_ref[g], preferred_element_type=jnp.float32)
            m_ref[:, lo:hi] = y * sc_ref[:, lo:hi]

    @pl.when(s > 0)
    def _():
        @pl.when(chunk == 0)
        def _():
            ext_ref[0:POOL_CARRY, :] = carry_ref[pl.ds(pl.multiple_of(seq * N_META, N_META), N_META), :]

        for g, w in enumerate(POOL_WINDOWS):
            lo, hi = g * POOL_GROUP_DIM, (g + 1) * POOL_GROUP_DIM
            xg = x[:, lo:hi]
            acc = xg
            for k in range(1, w):
                acc = acc + ext_ref[POOL_CARRY - k:POOL_CARRY - k + TB, lo:hi]
            diff = acc / float(w) - xg
            y = jnp.dot(diff.astype(jnp.bfloat16), wp_ref[g], preferred_element_type=jnp.float32)
            m_ref[:, lo:hi] = y * sc_ref[:, lo:hi]
        ext_ref[0:POOL_CARRY, :] = x[TB - POOL_CARRY:TB, :]

    _residual_route(x, m_ref[...], g_ref, b_ref, wr_ref, h_ref, r_ref, cnt_out_ref, hp_ref, cnt_ref, ltri_ref)


assert N_META >= max(POOL_WINDOWS)


def _const_spec(shape):
    nd = len(shape)
    return pl.BlockSpec(shape, lambda s: (0,) * nd)


def _x_spec(first_layer):
    if first_layer:
        return pl.BlockSpec((TB, D), lambda s: (jnp.maximum(s - 1, 0), 0))
    return pl.BlockSpec((TB, D), lambda s: (s, 0))


_MIXER_OUT_SPECS = [pl.BlockSpec((TB, D), lambda s: (s, 0)),
                    pl.BlockSpec((TB, LANES), lambda s: (s, 0)),
                    pl.BlockSpec((1, LANES), lambda s: (0, 0)),
                    pl.BlockSpec((TB, PACK_ROWS, LANES), lambda s: (s, 0, 0))]
_MIXER_OUT_SHAPES = [jax.ShapeDtypeStruct((NTOK, D), jnp.float32),
                     jax.ShapeDtypeStruct((NTOK, LANES), jnp.float32),
                     jax.ShapeDtypeStruct((1, LANES), jnp.float32),
                     jax.ShapeDtypeStruct((NTOK, PACK_ROWS, LANES), jnp.uint32)]


def _pool_layer(first_layer, head, x, ps, wp_bf, scale, ln_g, ln_b, wr):
    return pl.pallas_call(
        functools.partial(_pool_kernel, first_layer),
        grid=(N_STEP,),
        in_specs=[_const_spec((TB, D)), _x_spec(first_layer),
                  _const_spec((DEC_BATCH, D)),
                  _const_spec((len(POOL_WINDOWS), POOL_GROUP_DIM, POOL_GROUP_DIM)),
                  _const_spec((1, D)), _const_spec((1, D)), _const_spec((1, D)),
                  _const_spec((D, 2 * LANES))],
        out_specs=_MIXER_OUT_SPECS,
        out_shape=_MIXER_OUT_SHAPES,
        scratch_shapes=[pltpu.VMEM((POOL_CARRY + TB, D), jnp.float32),
                        pltpu.VMEM((TB, D), jnp.float32),
                        pltpu.VMEM((N_HEAD, D), jnp.float32),
                        pltpu.VMEM((1, LANES), jnp.float32),
                        pltpu.VMEM((TB, TB), jnp.bfloat16)],
        compiler_params=pltpu.CompilerParams(dimension_semantics=("arbitrary",),
                                             vmem_limit_bytes=VMEM_LIMIT),
        name="pool_mixer",
    )(head, x, ps, wp_bf, scale, ln_g, ln_b, wr)


def _depthwise_conv(ext_ref, wdw_ref, v_ref):
    base = CONV_CARRY - CONV_STATE
    for cb in range(D // CONV_CB):
        lo, hi = cb * CONV_CB, (cb + 1) * CONV_CB

        def body(i, carry, lo=lo, hi=hi):
            r0 = pl.multiple_of(i * CONV_RB, CONV_RB)
            sub = ext_ref.at[pl.ds(r0, CONV_RB + CONV_CARRY), :]
            v = None
            for r in range(SUBLANES):
                qs = [q for q in range((base + CONV_WIDTH) // SUBLANES + 1)
                      if 0 <= SUBLANES * q + r - base < CONV_WIDTH]
                z0 = SUBLANES * qs[0] + r
                z = sub[z0:SUBLANES * qs[-1] + r + CONV_RB, lo:hi]
                p = None
                for q in qs:
                    k = SUBLANES * q + r - base
                    off = SUBLANES * (q - qs[0])
                    term = wdw_ref[k:k + 1, lo:hi] * z[off:off + CONV_RB]
                    p = term if p is None else p + term
                v = p if v is None else v + p
            v_ref[pl.ds(r0, CONV_RB), lo:hi] = v
            return carry
        lax.fori_loop(0, TB // CONV_RB, body, 0)


def _conv_kernel(first_layer, head_ref, x_ref, vs_ref, wglu_ref, bglu_ref, wdw_ref, bdw_ref, lg_ref, lb_ref,
                 wpw_ref, bpw_ref, g_ref, b_ref, wr_ref, h_ref, r_ref, cnt_out_ref, hp_ref, ust_ref, us_ref,
                 ext_ref, v_ref, carry_ref, cnt_ref, ltri_ref):
    s = pl.program_id(0)
    seq, chunk = _seq_of_step(s)
    x = _load_block(first_layer, head_ref, x_ref)
    hh = jnp.dot(x.astype(jnp.bfloat16), wglu_ref[...], preferred_element_type=jnp.float32) + bglu_ref[...]
    u = hh[:, :D] * jax.nn.sigmoid(hh[:, D:])

    @pl.when(s == 0)
    def _():
        um = u[0:N_HEAD, :]
        us = u[N_HEAD:TB, :]
        carry_ref[...] = um
        us_ref[...] = us
        ext_ref[0:N_META, :] = jnp.zeros((N_META, D), jnp.float32)
        ext_ref[N_META:N_META + N_HEAD, :] = um
        pos = jnp.bitwise_and(lax.broadcasted_iota(jnp.int32, (N_HEAD, 1), 0), N_META - 1)
        acc = wdw_ref[CONV_WIDTH - 1:CONV_WIDTH, :] * um
        for d in range(1, N_META):
            k = CONV_WIDTH - 1 - d
            acc = acc + wdw_ref[k:k + 1, :] * jnp.where(pos >= d, ext_ref[N_META - d:N_META - d + N_HEAD, :], 0.0)
        v_ref[0:N_HEAD, :] = acc
        v_ref[N_HEAD:TB, :] = vs_ref[...] + wdw_ref[CONV_WIDTH - 1:CONV_WIDTH, :] * us

    @pl.when(s > 0)
    def _():
        @pl.when(chunk == 0)
        def _():
            ext_ref[0:CONV_CARRY - N_META, :] = jnp.zeros((CONV_CARRY - N_META, D), jnp.float32)
            ext_ref[CONV_CARRY - N_META:CONV_CARRY, :] = carry_ref[
                pl.ds(pl.multiple_of(seq * N_META, N_META), N_META), :]

        ext_ref[CONV_CARRY:CONV_CARRY + TB, :] = u
        _depthwise_conv(ext_ref, wdw_ref, v_ref)
        ext_ref[0:CONV_CARRY, :] = ext_ref[TB:TB + CONV_CARRY, :]

        @pl.when(chunk == CHUNKS - 1)
        def _():
            ust_ref[0] = ext_ref[CONV_CARRY + TB - CONV_STATE:CONV_CARRY + TB, :]

    v = v_ref[...] + bdw_ref[...]
    v = _layer_norm(v, lg_ref[...], lb_ref[...])
    v = v * jax.nn.sigmoid(v)
    m = jnp.dot(v.astype(jnp.bfloat16), wpw_ref[...], preferred_element_type=jnp.float32) + bpw_ref[...]
    _residual_route(x, m, g_ref, b_ref, wr_ref, h_ref, r_ref, cnt_out_ref, hp_ref, cnt_ref, ltri_ref)


assert CONV_CARRY - N_META + N_META >= CONV_STATE and N_META <= CONV_STATE


def _conv_layer(first_layer, head, x, vs, wglu_bf, bglu, wdw, bdw, cln_g, cln_b, wpw_bf, bpw, ln_g, ln_b, wr):
    return pl.pallas_call(
        functools.partial(_conv_kernel, first_layer),
        grid=(N_STEP,),
        in_specs=[_const_spec((TB, D)), _x_spec(first_layer),
                  _const_spec((DEC_BATCH, D)),
                  _const_spec((D, 2 * D)), _const_spec((1, 2 * D)),
                  _const_spec((CONV_WIDTH, D)), _const_spec((1, D)),
                  _const_spec((1, D)), _const_spec((1, D)),
                  _const_spec((D, D)), _const_spec((1, D)),
                  _const_spec((1, D)), _const_spec((1, D)),
                  _const_spec((D, 2 * LANES))],
        out_specs=_MIXER_OUT_SPECS + [
            pl.BlockSpec((1, CONV_STATE, D), lambda s: (jnp.maximum(s - 1, 0) // CHUNKS, 0, 0)),
            _const_spec((DEC_BATCH, D))],
        out_shape=_MIXER_OUT_SHAPES + [
            jax.ShapeDtypeStruct((BATCH, CONV_STATE, D), jnp.float32),
            jax.ShapeDtypeStruct((DEC_BATCH, D), jnp.float32)],
        scratch_shapes=[pltpu.VMEM((CONV_CARRY + TB, D), jnp.float32),
                        pltpu.VMEM((TB, D), jnp.float32),
                        pltpu.VMEM((N_HEAD, D), jnp.float32),
                        pltpu.VMEM((1, LANES), jnp.float32),
                        pltpu.VMEM((TB, TB), jnp.bfloat16)],
        compiler_params=pltpu.CompilerParams(dimension_semantics=("arbitrary",),
                                             vmem_limit_bytes=VMEM_LIMIT),
        name="conv_mixer",
    )(head, x, vs, wglu_bf, bglu, wdw, bdw, cln_g, cln_b, wpw_bf, bpw, ln_g, ln_b, wr)


def _tile_gather_start(src_hbm, dst, sem, row_index, n_rows, dst_tile=lambda r: r):
    for r in range(n_rows):
        tok = row_index(r)
        pltpu.make_async_copy(src_hbm.at[pl.ds(pl.multiple_of(tok * SUBLANES, SUBLANES), SUBLANES), :],
                              dst.at[pl.ds(dst_tile(r) * SUBLANES, SUBLANES), :], sem).start(priority=r % 2)


def _tile_gather_wait(src_hbm, dst, sem, n_rows):
    pltpu.make_async_copy(src_hbm.at[pl.ds(0, n_rows * SUBLANES), :], dst, sem).wait()


assert TOP_K == 2


def _expert_kernel(be_ref, nused_ref, base_ref, limit_ref, order_ref, hp_ref, wg_ref, wu_ref, wd_ref, ys_ref,
                   xbuf, wgb, wub, wdb):
    i = pl.program_id(0)

    @pl.when(i < nused_ref[0])
    def _():
        @pl.when(jnp.logical_or(i == 0, be_ref[i] != be_ref[jnp.maximum(i - 1, 0)]))
        def _():
            wgb[...] = wg_ref[0, 0].astype(jnp.bfloat16)
            wub[...] = wu_ref[0, 0].astype(jnp.bfloat16)
            wdb[...] = wd_ref[0, 0].astype(jnp.bfloat16)

        b0 = base_ref[i]
        lim = limit_ref[i]
        for r in range(BM):
            tok = lax.shift_right_logical(order_ref[jnp.minimum(b0 + r, lim)], 1)
            xbuf[r] = hp_ref[tok]
        words = jnp.concatenate([xbuf[:, s, :] for s in range(PACK_ROWS)], axis=1)
        xb = _unpack_bf16(words)
        gate = jnp.dot(xb, wgb[...], preferred_element_type=jnp.float32)
        up = jnp.dot(xb, wub[...], preferred_element_type=jnp.float32)
        hid = gate * jax.nn.sigmoid(gate) * up
        y = jnp.dot(hid.astype(jnp.bfloat16), wdb[...], preferred_element_type=jnp.float32)
        _tile_store(ys_ref, y, BM)

    @pl.when(i >= nused_ref[0])
    def _():
        ys_ref[...] = jnp.zeros((BM * SUBLANES, LANES), jnp.float32)


def _expert_layer(layer, block_expert, nused, base, limit, order, h_packed, w_gate, w_up, w_down):
    def w_spec(shape):
        return pl.BlockSpec((1, 1) + shape, lambda i, be, *_: (layer, be[i], 0, 0))
    grid_spec = pltpu.PrefetchScalarGridSpec(
        num_scalar_prefetch=5,
        grid=(NB,),
        in_specs=[pl.BlockSpec((NTOK, PACK_ROWS, LANES), lambda i, *_: (0, 0, 0)),
                  w_spec((D, D_EXPERT)), w_spec((D, D_EXPERT)), w_spec((D_EXPERT, D))],
        out_specs=pl.BlockSpec((BM * SUBLANES, LANES), lambda i, *_: (i, 0)),
        scratch_shapes=[pltpu.VMEM((BM, PACK_ROWS, LANES), jnp.uint32),
                        pltpu.VMEM((D, D_EXPERT), jnp.bfloat16),
                        pltpu.VMEM((D, D_EXPERT), jnp.bfloat16),
                        pltpu.VMEM((D_EXPERT, D), jnp.bfloat16)],
    )
    return pl.pallas_call(
        _expert_kernel,
        grid_spec=grid_spec,
        out_shape=jax.ShapeDtypeStruct((N_SLOT * SUBLANES, LANES), jnp.float32),
        compiler_params=pltpu.CompilerParams(dimension_semantics=("arbitrary",),
                                             vmem_limit_bytes=EXPERT_VMEM_LIMIT),
        name="expert_mlp",
    )(block_expert, nused, base, limit, order, h_packed, w_gate, w_up, w_down)


def _combine_kernel(last_layer, slot_ref, h_ref, r_ref, ys_hbm, g_ref, b_ref, *rest):
    if last_layer:
        yp_ref, ysamp_ref, ybuf, sem = rest
    else:
        o_ref, ybuf, sem = rest
    i = pl.program_id(0)
    n = pl.num_programs(0)

    def start(step, buf):
        base = step * (TOP_K * TB)
        _tile_gather_start(ys_hbm, ybuf.at[buf], sem.at[buf], lambda r: slot_ref[base + r], TOP_K * TB,
                           dst_tile=lambda r: (r % TOP_K) * TB + r // TOP_K)

    @pl.when(i == 0)
    def _():
        start(0, 0)

    buf = i % 2
    _tile_gather_wait(ys_hbm, ybuf.at[buf], sem.at[buf], TOP_K * TB)
    yb = ybuf.at[buf]
    y0 = _tile_load(yb.at[pl.ds(0, TB * SUBLANES), :], TB)
    y1 = _tile_load(yb.at[pl.ds(TB * SUBLANES, TB * SUBLANES), :], TB)
    h = h_ref[...]
    r = r_ref[...]

    start(jnp.minimum(i + 1, n - 1), 1 - buf)

    f = r[:, 0:1] * y0 + r[:, 1:2] * y1
    out = _layer_norm(ALPHA * h + f, g_ref[...], b_ref[...])
    if last_layer:
        yp_ref[...] = out

        @pl.when(i == 0)
        def _():
            ysamp_ref[...] = out[N_HEAD:TB, :]
    else:
        o_ref[...] = out

    @pl.when(i == n - 1)
    def _():
        _tile_gather_wait(ys_hbm, ybuf.at[1 - buf], sem.at[1 - buf], TOP_K * TB)


def _combine_layer(last_layer, slot, h_tiles, r_all, ys, ln_g, ln_b):
    if last_layer:
        out_specs = [pl.BlockSpec((TB, D), lambda i, sl: (jnp.maximum(i - 1, 0), 0)),
                     pl.BlockSpec((DEC_BATCH, D), lambda i, sl: (0, 0))]
        out_shape = [jax.ShapeDtypeStruct((BATCH * SEQ, D), jnp.float32),
                     jax.ShapeDtypeStruct((DEC_BATCH, D), jnp.float32)]
    else:
        out_specs = pl.BlockSpec((TB, D), lambda i, sl: (i, 0))
        out_shape = jax.ShapeDtypeStruct((NTOK, D), jnp.float32)
    grid_spec = pltpu.PrefetchScalarGridSpec(
        num_scalar_prefetch=1,
        grid=(N_STEP,),
        in_specs=[pl.BlockSpec((TB, D), lambda i, sl: (i, 0)),
                  pl.BlockSpec((TB, LANES), lambda i, sl: (i, 0)),
                  pl.BlockSpec(memory_space=pl.ANY),
                  pl.BlockSpec((1, D), lambda i, sl: (0, 0)),
                  pl.BlockSpec((1, D), lambda i, sl: (0, 0))],
        out_specs=out_specs,
        scratch_shapes=[pltpu.VMEM((2, TOP_K * TB * SUBLANES, LANES), jnp.float32),
                        pltpu.SemaphoreType.DMA((2,))],
    )
    return pl.pallas_call(
        functools.partial(_combine_kernel, last_layer),
        grid_spec=grid_spec,
        out_shape=out_shape,
        compiler_params=pltpu.CompilerParams(dimension_semantics=("arbitrary",),
                                             vmem_limit_bytes=VMEM_LIMIT),
        name="moe_combine",
    )(slot, h_tiles, r_all, ys, ln_g, ln_b)


def _routing_metadata(r_all, cnt):
    counts = cnt[0, :N_EXPERTS].astype(jnp.int32)
    nblk = (counts + BM - 1) // BM
    blk_end = jnp.cumsum(nblk)
    pstart = (blk_end - nblk) * BM
    nused = blk_end[-1]
    blk = jnp.arange(NB, dtype=jnp.int32)
    block_expert = jnp.sum((blk[:, None] >= blk_end[None, :]).astype(jnp.int32), axis=1)
    last_expert = jnp.max(jnp.where(nblk > 0, jnp.arange(N_EXPERTS, dtype=jnp.int32), 0))
    block_expert = jnp.where(blk < nused, block_expert, last_expert).astype(jnp.int32)
    expert = r_all[:, 2:2 + TOP_K].astype(jnp.int32)
    rank = r_all[:, 2 + TOP_K:2 + 2 * TOP_K].astype(jnp.int32)
    sel = expert[:, :, None] == jnp.arange(N_EXPERTS, dtype=jnp.int32)[None, None, :]
    slot = rank + jnp.sum(jnp.where(sel, pstart[None, None, :], 0), axis=-1)
    order = jnp.argsort(expert.reshape(N_ASSIGN), stable=True).astype(jnp.int32)
    starts = jnp.cumsum(counts) - counts
    base = starts[block_expert] - pstart[block_expert] + blk * BM
    limit = starts[block_expert] + counts[block_expert] - 1
    return (block_expert, nused.reshape(1).astype(jnp.int32), base.astype(jnp.int32),
            limit.astype(jnp.int32), order, slot.reshape(N_ASSIGN).astype(jnp.int32))


def kernel(x_prompt, x_sample, state_pool, state_conv, meta, w_pool, pool_scale, w_glu, b_glu, w_dw, b_dw, conv_ln_g, conv_ln_b, w_pw, b_pw, ln_mix_g, ln_mix_b, ln_ffn_g, ln_ffn_b, w_router_group, w_router_expert, w_gate, w_up, w_down):
    f32 = jnp.float32
    x_samp2d = x_sample.reshape(DEC_BATCH, D)
    head = jnp.concatenate([jnp.tile(meta.astype(f32), (BATCH, 1)), x_samp2d], axis=0)
    x_in = x_prompt.reshape(BATCH * SEQ, D)

    k_idx = jnp.arange(POOL_STATE)[:, None]
    win = jnp.repeat(jnp.asarray(POOL_WINDOWS), POOL_GROUP_DIM)[None, :]
    pool_coef = (k_idx >= (POOL_STATE + 1 - win)).astype(f32)

    new_pool_p, new_pool_s, new_conv_p, new_conv_s = [], [], [], []
    for i in range(DEPTH):
        j = i // 2
        first = i == 0
        wr32 = jnp.concatenate([w_router_group[i], w_router_expert[i],
                                jnp.zeros((D, LANES - N_GROUPS - N_EXPERTS), f32)], axis=1)
        wr_hi = wr32.astype(jnp.bfloat16)
        wr = jnp.concatenate([wr_hi, (wr32 - wr_hi.astype(f32)).astype(jnp.bfloat16)], axis=1)
        ln_g, ln_b = ln_mix_g[i].reshape(1, D), ln_mix_b[i].reshape(1, D)
        if i % 2 == 0:
            if first:
                tail = x_prompt[:, SEQ - POOL_STATE:]
                x_s = x_samp2d
            else:
                tail = jnp.stack([x_in[TB + (b + 1) * SEQ - POOL_STATE:TB + (b + 1) * SEQ]
                                  for b in range(BATCH)])
                x_s = x_in[N_HEAD:TB]
            new_pool_p.append(tail)
            new_pool_s.append(jnp.concatenate([state_pool[j][:, 1:], x_s[:, None, :]], axis=1))
            ps = _state_presum(state_pool[j].reshape(DEC_BATCH, POOL_STATE * D), pool_coef)
            h1, r_all, cnt, h1p = _pool_layer(first, head, x_in, ps, w_pool[j].astype(jnp.bfloat16),
                                              pool_scale[j].reshape(1, D), ln_g, ln_b, wr)
        else:
            vs = _state_presum(state_conv[j].reshape(DEC_BATCH, CONV_STATE * D), w_dw[j][:CONV_STATE])
            h1, r_all, cnt, h1p, ust, u_s = _conv_layer(
                first, head, x_in, vs, w_glu[j].astype(jnp.bfloat16), b_glu[j].reshape(1, 2 * D), w_dw[j],
                b_dw[j].reshape(1, D), conv_ln_g[j].reshape(1, D), conv_ln_b[j].reshape(1, D),
                w_pw[j].astype(jnp.bfloat16), b_pw[j].reshape(1, D), ln_g, ln_b, wr)
            new_conv_p.append(ust)
            new_conv_s.append(jnp.concatenate([state_conv[j][:, 1:], u_s[:, None, :]], axis=1))
        block_expert, nused, base, limit, order, slot = _routing_metadata(r_all, cnt)
        ys = _expert_layer(i, block_expert, nused, base, limit, order, h1p, w_gate, w_up, w_down)
        x_in = _combine_layer(i == DEPTH - 1, slot, h1, r_all, ys,
                              ln_ffn_g[i].reshape(1, D), ln_ffn_b[i].reshape(1, D))

    y_prompt, y_samp = x_in
    return (y_prompt.reshape(BATCH, SEQ, D), y_samp.reshape(DEC_BATCH, 1, D),
            jnp.stack(new_pool_p), jnp.stack(new_pool_s), jnp.stack(new_conv_p), jnp.stack(new_conv_s))
```

```python
import functools

import jax
import jax.numpy as jnp
from jax import lax
from jax.experimental import pallas as pl
from jax.experimental.pallas import tpu as pltpu

D = 1024
BATCH = 8
SEQ = 2048
DEPTH = 4
DEC_BATCH = 128
N_META = 16
POOL_WINDOWS = (2, 4, 8, 16)
POOL_GROUP_DIM = D // len(POOL_WINDOWS)
POOL_STATE = max(POOL_WINDOWS) - 1
CONV_WIDTH = 31
CONV_STATE = CONV_WIDTH - 1
N_GROUPS = 4
EXPERTS_PER_GROUP = 8
N_EXPERTS = N_GROUPS * EXPERTS_PER_GROUP
TOP_K = 2
D_EXPERT = D // 2
ALPHA = (2.0 * DEPTH) ** 0.25
LN_EPS = 1e-5

LANES = 128
SUBLANES = 8
N_HEAD = BATCH * N_META
TB = 256
NTOK = N_HEAD + DEC_BATCH + BATCH * SEQ
N_STEP = NTOK // TB
CHUNKS = SEQ // TB
POOL_CARRY = 16
CONV_CARRY = 32
CONV_RB = 128
CONV_CB = 128
BM = 512
N_ASSIGN = NTOK * TOP_K
NB = N_ASSIGN // BM + N_EXPERTS
N_SLOT = NB * BM
VMEM_LIMIT = 48 * 1024 * 1024
EXPERT_VMEM_LIMIT = 56 * 1024 * 1024
NEG = -1e30

assert D == SUBLANES * LANES and N_HEAD + DEC_BATCH == TB and SEQ % TB == 0
assert N_ASSIGN % BM == 0 and TB % CONV_RB == 0 and N_META == POOL_CARRY and CONV_STATE <= TB


def _tile_load(ref, n):
    return jnp.concatenate([ref[pl.ds(s, n, stride=SUBLANES), :] for s in range(SUBLANES)], axis=1)


def _tile_store(ref, val, n):
    for s in range(SUBLANES):
        ref[pl.ds(s, n, stride=SUBLANES), :] = val[:, s * LANES:(s + 1) * LANES]


def _layer_norm(x, g, b):
    mu = jnp.mean(x, axis=-1, keepdims=True)
    xc = x - mu
    var = jnp.mean(xc * xc, axis=-1, keepdims=True)
    return xc * lax.rsqrt(var + LN_EPS) * g + b


PACK_ROWS = D // 2 // LANES


def _pack_bf16(h_bf16, hp_ref):
    bits = pltpu.bitcast(h_bf16.astype(jnp.float32), jnp.uint32)
    words = jnp.bitwise_or(lax.shift_right_logical(bits[:, :D // 2], jnp.uint32(16)),
                           jnp.bitwise_and(bits[:, D // 2:], jnp.uint32(0xFFFF0000)))
    for s in range(PACK_ROWS):
        hp_ref[:, s, :] = words[:, s * LANES:(s + 1) * LANES]


def _unpack_bf16(words):
    lo = pltpu.bitcast(lax.shift_left(words, jnp.uint32(16)), jnp.float32)
    hi = pltpu.bitcast(jnp.bitwise_and(words, jnp.uint32(0xFFFF0000)), jnp.float32)
    return jnp.concatenate([lo, hi], axis=1).astype(jnp.bfloat16)


def _residual_route(x, m, g_ref, b_ref, wr_ref, h_ref, r_ref, cnt_out_ref, hp_ref, rt_ref, cnt_ref, ltri_ref):
    @pl.when(pl.program_id(0) == 0)
    def _():
        cnt_ref[...] = jnp.zeros((1, LANES), jnp.float32)
        row = lax.broadcasted_iota(jnp.int32, (TB, TB), 0)
        col = lax.broadcasted_iota(jnp.int32, (TB, TB), 1)
        ltri_ref[...] = jnp.where(col < row, 1.0, 0.0).astype(jnp.bfloat16)

    h = _layer_norm(ALPHA * x + m, g_ref[...], b_ref[...])
    h_ref[...] = h
    h_hi = h.astype(jnp.bfloat16)
    _pack_bf16(h_hi, hp_ref)
    h_lo = (h - h_hi.astype(jnp.float32)).astype(jnp.bfloat16)
    hi_both = jnp.dot(h_hi, wr_ref[...], preferred_element_type=jnp.float32)
    lo_hi = jnp.dot(h_lo, wr_ref[:, 0:LANES], preferred_element_type=jnp.float32)
    logits = hi_both[:, 0:LANES] + (hi_both[:, LANES:2 * LANES] + lo_hi)
    lane = lax.broadcasted_iota(jnp.int32, logits.shape, 1)
    lanef = lane.astype(jnp.float32)
    big = jnp.float32(1e9)
    lg = jnp.where(lane < N_GROUPS, logits, NEG)
    mg = jnp.max(lg, axis=1, keepdims=True)
    gidx = jnp.min(jnp.where(lg == mg, lanef, big), axis=1, keepdims=True)
    p_grp = 1.0 / jnp.sum(jnp.where(lane < N_GROUPS, jnp.exp(lg - mg), 0.0), axis=1, keepdims=True)
    lo = N_GROUPS + EXPERTS_PER_GROUP * gidx
    le = jnp.where(lanef >= lo, jnp.where(lanef < lo + EXPERTS_PER_GROUP, logits, NEG), NEG)
    m1 = jnp.max(le, axis=1, keepdims=True)
    i1 = jnp.min(jnp.where(le == m1, lanef, big), axis=1, keepdims=True)
    le2 = jnp.where(lanef == i1, NEG, le)
    m2 = jnp.max(le2, axis=1, keepdims=True)
    i2 = jnp.min(jnp.where(le2 == m2, lanef, big), axis=1, keepdims=True)
    ratio = jnp.exp(m2 - m1)
    g1 = p_grp / (1.0 + ratio)
    g2 = g1 * ratio
    e1 = i1 - N_GROUPS
    e2 = i2 - N_GROUPS

    is1 = lanef == e1
    is2 = lanef == e2
    onehot = jnp.where(is1, 1.0, jnp.where(is2, 1.0, 0.0))
    before = jnp.dot(ltri_ref[...], onehot.astype(jnp.bfloat16),
                     preferred_element_type=jnp.float32) + cnt_ref[...]
    rank1 = jnp.sum(jnp.where(is1, before, 0.0), axis=1, keepdims=True)
    rank2 = jnp.sum(jnp.where(is2, before, 0.0), axis=1, keepdims=True)
    cnt = cnt_ref[...] + jnp.sum(onehot, axis=0, keepdims=True)
    cnt_ref[...] = cnt
    cnt_out_ref[...] = cnt

    packed = jnp.where(lane == 0, g1,
                       jnp.where(lane == 1, g2,
                                 jnp.where(lane == 2, e1,
                                           jnp.where(lane == 3, e2,
                                                     jnp.where(lane == 4, rank1,
                                                               jnp.where(lane == 5, rank2, 0.0))))))
    r_ref[...] = packed
    rt_ref[...] = jnp.transpose(packed)[0:SUBLANES, :]


def _load_block(first_layer, head_ref, x_ref):
    if not first_layer:
        return x_ref[...]
    return jnp.where(pl.program_id(0) == 0, head_ref[...], x_ref[...])


def _seq_of_step(s):
    sm1 = jnp.maximum(s - 1, 0)
    return lax.shift_right_logical(sm1, CHUNKS.bit_length() - 1), jnp.bitwise_and(sm1, CHUNKS - 1)


assert CHUNKS & (CHUNKS - 1) == 0


PRESUM_ROWS = 32


def _presum_kernel(st_ref, coef_ref, o_ref):
    coef = coef_ref[...]
    for n in range(PRESUM_ROWS):
        o_ref[n:n + 1, :] = jnp.sum(st_ref[0, n] * coef, axis=0, keepdims=True)


def _state_presum(j, state, coef):
    _, n, k, _ = state.shape
    return pl.pallas_call(
        _presum_kernel,
        grid=(n // PRESUM_ROWS,),
        in_specs=[pl.BlockSpec((1, PRESUM_ROWS, k, D), lambda i: (j, i, 0, 0)),
                  pl.BlockSpec((k, D), lambda i: (0, 0))],
        out_specs=pl.BlockSpec((PRESUM_ROWS, D), lambda i: (i, 0)),
        out_shape=jax.ShapeDtypeStruct((n, D), jnp.float32),
        compiler_params=pltpu.CompilerParams(dimension_semantics=("arbitrary",),
                                             vmem_limit_bytes=VMEM_LIMIT),
        name="state_presum",
    )(state, coef)


def _pool_kernel(first_layer, head_ref, x_ref, ps_ref, wp_ref, sc_ref, g_ref, b_ref, wr_ref,
                 h_ref, r_ref, cnt_out_ref, hp_ref, rt_ref, ext_ref, m_ref, carry_ref, cnt_ref, ltri_ref):
    s = pl.program_id(0)
    seq, chunk = _seq_of_step(s)
    x = _load_block(first_layer, head_ref, x_ref)
    ext_ref[POOL_CARRY:POOL_CARRY + TB, :] = x

    @pl.when(s == 0)
    def _():
        ext_ref[0:POOL_CARRY, :] = jnp.zeros((POOL_CARRY, D), jnp.float32)
        carry_ref[...] = x[0:N_HEAD, :]
        pos = jnp.bitwise_and(lax.broadcasted_iota(jnp.int32, (N_HEAD, 1), 0), N_META - 1)
        for g, w in enumerate(POOL_WINDOWS):
            lo, hi = g * POOL_GROUP_DIM, (g + 1) * POOL_GROUP_DIM
            xm = x[0:N_HEAD, lo:hi]
            acc = xm
            for k in range(1, w):
                acc = acc + jnp.where(pos >= k, ext_ref[POOL_CARRY - k:POOL_CARRY - k + N_HEAD, lo:hi], 0.0)
            cnt = jnp.minimum(pos + 1, w).astype(jnp.float32)
            d_meta = acc / cnt - xm
            xs = x[N_HEAD:TB, lo:hi]
            d_samp = (ps_ref[:, lo:hi] + xs) / float(w) - xs
            diff = jnp.concatenate([d_meta, d_samp], axis=0)
            y = jnp.dot(diff.astype(jnp.bfloat16), wp_ref[g], preferred_element_type=jnp.float32)
            m_ref[:, lo:hi] = y * sc_ref[:, lo:hi]

    @pl.when(s > 0)
    def _():
        @pl.when(chunk == 0)
        def _():
            ext_ref[0:POOL_CARRY, :] = carry_ref[pl.ds(pl.multiple_of(seq * N_META, N_META), N_META), :]

        for g, w in enumerate(POOL_WINDOWS):
            lo, hi = g * POOL_GROUP_DIM, (g + 1) * POOL_GROUP_DIM
            xg = x[:, lo:hi]
            acc = xg
            for k in range(1, w):
                acc = acc + ext_ref[POOL_CARRY - k:POOL_CARRY - k + TB, lo:hi]
            diff = acc / float(w) - xg
            y = jnp.dot(diff.astype(jnp.bfloat16), wp_ref[g], preferred_element_type=jnp.float32)
            m_ref[:, lo:hi] = y * sc_ref[:, lo:hi]
        ext_ref[0:POOL_CARRY, :] = x[TB - POOL_CARRY:TB, :]

    _residual_route(x, m_ref[...], g_ref, b_ref, wr_ref, h_ref, r_ref, cnt_out_ref, hp_ref, rt_ref, cnt_ref, ltri_ref)


assert N_META >= max(POOL_WINDOWS)


def _const_spec(shape):
    nd = len(shape)
    return pl.BlockSpec(shape, lambda s: (0,) * nd)


def _x_spec(first_layer):
    if first_layer:
        return pl.BlockSpec((TB, D), lambda s: (jnp.maximum(s - 1, 0), 0))
    return pl.BlockSpec((TB, D), lambda s: (s, 0))


_MIXER_OUT_SPECS = [pl.BlockSpec((TB, D), lambda s: (s, 0)),
                    pl.BlockSpec((TB, LANES), lambda s: (s, 0)),
                    pl.BlockSpec((1, LANES), lambda s: (0, 0)),
                    pl.BlockSpec((TB, PACK_ROWS, LANES), lambda s: (s, 0, 0)),
                    pl.BlockSpec((SUBLANES, TB), lambda s: (s, 0))]
_MIXER_OUT_SHAPES = [jax.ShapeDtypeStruct((NTOK, D), jnp.float32),
                     jax.ShapeDtypeStruct((NTOK, LANES), jnp.float32),
                     jax.ShapeDtypeStruct((1, LANES), jnp.float32),
                     jax.ShapeDtypeStruct((NTOK, PACK_ROWS, LANES), jnp.uint32),
                     jax.ShapeDtypeStruct((N_STEP * SUBLANES, TB), jnp.float32)]


def _pool_layer(first_layer, head, x, ps, wp_bf, scale, ln_g, ln_b, wr):
    return pl.pallas_call(
        functools.partial(_pool_kernel, first_layer),
        grid=(N_STEP,),
        in_specs=[_const_spec((TB, D)), _x_spec(first_layer),
                  _const_spec((DEC_BATCH, D)),
                  _const_spec((len(POOL_WINDOWS), POOL_GROUP_DIM, POOL_GROUP_DIM)),
                  _const_spec((1, D)), _const_spec((1, D)), _const_spec((1, D)),
                  _const_spec((D, 2 * LANES))],
        out_specs=_MIXER_OUT_SPECS,
        out_shape=_MIXER_OUT_SHAPES,
        scratch_shapes=[pltpu.VMEM((POOL_CARRY + TB, D), jnp.float32),
                        pltpu.VMEM((TB, D), jnp.float32),
                        pltpu.VMEM((N_HEAD, D), jnp.float32),
                        pltpu.VMEM((1, LANES), jnp.float32),
                        pltpu.VMEM((TB, TB), jnp.bfloat16)],
        compiler_params=pltpu.CompilerParams(dimension_semantics=("arbitrary",),
                                             vmem_limit_bytes=VMEM_LIMIT),
        name="pool_mixer",
    )(head, x, ps, wp_bf, scale, ln_g, ln_b, wr)


def _depthwise_conv(ext_ref, wdw_ref, v_ref):
    base = CONV_CARRY - CONV_STATE
    for cb in range(D // CONV_CB):
        lo, hi = cb * CONV_CB, (cb + 1) * CONV_CB

        def body(i, carry, lo=lo, hi=hi):
            r0 = pl.multiple_of(i * CONV_RB, CONV_RB)
            sub = ext_ref.at[pl.ds(r0, CONV_RB + CONV_CARRY), :]
            v = None
            for r in range(SUBLANES):
                qs = [q for q in range((base + CONV_WIDTH) // SUBLANES + 1)
                      if 0 <= SUBLANES * q + r - base < CONV_WIDTH]
                z0 = SUBLANES * qs[0] + r
                z = sub[z0:SUBLANES * qs[-1] + r + CONV_RB, lo:hi]
                p = None
                for q in qs:
                    k = SUBLANES * q + r - base
                    off = SUBLANES * (q - qs[0])
                    term = wdw_ref[k:k + 1, lo:hi] * z[off:off + CONV_RB]
                    p = term if p is None else p + term
                v = p if v is None else v + p
            v_ref[pl.ds(r0, CONV_RB), lo:hi] = v
            return carry
        lax.fori_loop(0, TB // CONV_RB, body, 0)


def _conv_kernel(first_layer, head_ref, x_ref, vs_ref, wglu_ref, bglu_ref, wdw_ref, bdw_ref, lg_ref, lb_ref,
                 wpw_ref, bpw_ref, g_ref, b_ref, wr_ref, h_ref, r_ref, cnt_out_ref, hp_ref, rt_ref, ust_ref, us_ref,
                 ext_ref, v_ref, carry_ref, cnt_ref, ltri_ref):
    s = pl.program_id(0)
    seq, chunk = _seq_of_step(s)
    x = _load_block(first_layer, head_ref, x_ref)
    hh = jnp.dot(x.astype(jnp.bfloat16), wglu_ref[...], preferred_element_type=jnp.float32) + bglu_ref[...]
    u = hh[:, :D] * jax.nn.sigmoid(hh[:, D:])

    @pl.when(s == 0)
    def _():
        um = u[0:N_HEAD, :]
        us = u[N_HEAD:TB, :]
        carry_ref[...] = um
        us_ref[...] = us
        ext_ref[0:N_META, :] = jnp.zeros((N_META, D), jnp.float32)
        ext_ref[N_META:N_META + N_HEAD, :] = um
        pos = jnp.bitwise_and(lax.broadcasted_iota(jnp.int32, (N_HEAD, 1), 0), N_META - 1)
        acc = wdw_ref[CONV_WIDTH - 1:CONV_WIDTH, :] * um
        for d in range(1, N_META):
            k = CONV_WIDTH - 1 - d
            acc = acc + wdw_ref[k:k + 1, :] * jnp.where(pos >= d, ext_ref[N_META - d:N_META - d + N_HEAD, :], 0.0)
        v_ref[0:N_HEAD, :] = acc
        v_ref[N_HEAD:TB, :] = vs_ref[...] + wdw_ref[CONV_WIDTH - 1:CONV_WIDTH, :] * us

    @pl.when(s > 0)
    def _():
        @pl.when(chunk == 0)
        def _():
            ext_ref[0:CONV_CARRY - N_META, :] = jnp.zeros((CONV_CARRY - N_META, D), jnp.float32)
            ext_ref[CONV_CARRY - N_META:CONV_CARRY, :] = carry_ref[
                pl.ds(pl.multiple_of(seq * N_META, N_META), N_META), :]

        ext_ref[CONV_CARRY:CONV_CARRY + TB, :] = u
        _depthwise_conv(ext_ref, wdw_ref, v_ref)
        ext_ref[0:CONV_CARRY, :] = ext_ref[TB:TB + CONV_CARRY, :]

        @pl.when(chunk == CHUNKS - 1)
        def _():
            ust_ref[0] = ext_ref[CONV_CARRY + TB - CONV_STATE:CONV_CARRY + TB, :]

    v = v_ref[...] + bdw_ref[...]
    v = _layer_norm(v, lg_ref[...], lb_ref[...])
    v = v * jax.nn.sigmoid(v)
    m = jnp.dot(v.astype(jnp.bfloat16), wpw_ref[...], preferred_element_type=jnp.float32) + bpw_ref[...]
    _residual_route(x, m, g_ref, b_ref, wr_ref, h_ref, r_ref, cnt_out_ref, hp_ref, rt_ref, cnt_ref, ltri_ref)


assert CONV_CARRY - N_META + N_META >= CONV_STATE and N_META <= CONV_STATE


def _conv_layer(first_layer, head, x, vs, wglu_bf, bglu, wdw, bdw, cln_g, cln_b, wpw_bf, bpw, ln_g, ln_b, wr):
    return pl.pallas_call(
        functools.partial(_conv_kernel, first_layer),
        grid=(N_STEP,),
        in_specs=[_const_spec((TB, D)), _x_spec(first_layer),
                  _const_spec((DEC_BATCH, D)),
                  _const_spec((D, 2 * D)), _const_spec((1, 2 * D)),
                  _const_spec((CONV_WIDTH, D)), _const_spec((1, D)),
                  _const_spec((1, D)), _const_spec((1, D)),
                  _const_spec((D, D)), _const_spec((1, D)),
                  _const_spec((1, D)), _const_spec((1, D)),
                  _const_spec((D, 2 * LANES))],
        out_specs=_MIXER_OUT_SPECS + [
            pl.BlockSpec((1, CONV_STATE, D), lambda s: (jnp.maximum(s - 1, 0) // CHUNKS, 0, 0)),
            _const_spec((DEC_BATCH, D))],
        out_shape=_MIXER_OUT_SHAPES + [
            jax.ShapeDtypeStruct((BATCH, CONV_STATE, D), jnp.float32),
            jax.ShapeDtypeStruct((DEC_BATCH, D), jnp.float32)],
        scratch_shapes=[pltpu.VMEM((CONV_CARRY + TB, D), jnp.float32),
                        pltpu.VMEM((TB, D), jnp.float32),
                        pltpu.VMEM((N_HEAD, D), jnp.float32),
                        pltpu.VMEM((1, LANES), jnp.float32),
                        pltpu.VMEM((TB, TB), jnp.bfloat16)],
        compiler_params=pltpu.CompilerParams(dimension_semantics=("arbitrary",),
                                             vmem_limit_bytes=VMEM_LIMIT),
        name="conv_mixer",
    )(head, x, vs, wglu_bf, bglu, wdw, bdw, cln_g, cln_b, wpw_bf, bpw, ln_g, ln_b, wr)


def _tile_gather_start(src_hbm, dst, sem, row_index, n_rows, dst_tile=lambda r: r):
    for r in range(n_rows):
        tok = row_index(r)
        pltpu.make_async_copy(src_hbm.at[pl.ds(pl.multiple_of(tok * SUBLANES, SUBLANES), SUBLANES), :],
                              dst.at[pl.ds(dst_tile(r) * SUBLANES, SUBLANES), :], sem).start(priority=r % 2)


def _tile_gather_wait(src_hbm, dst, sem, n_rows):
    pltpu.make_async_copy(src_hbm.at[pl.ds(0, n_rows * SUBLANES), :], dst, sem).wait()


assert TOP_K == 2


def _expert_kernel(be_ref, nused_ref, base_ref, limit_ref, order_ref, hp_ref, wg_ref, wu_ref, wd_ref, ys_ref,
                   xbuf, wgb, wub, wdb):
    i = pl.program_id(0)

    @pl.when(i < nused_ref[0])
    def _():
        @pl.when(jnp.logical_or(i == 0, be_ref[i] != be_ref[jnp.maximum(i - 1, 0)]))
        def _():
            wgb[...] = wg_ref[0, 0].astype(jnp.bfloat16)
            wub[...] = wu_ref[0, 0].astype(jnp.bfloat16)
            wdb[...] = wd_ref[0, 0].astype(jnp.bfloat16)

        b0 = base_ref[i]
        lim = limit_ref[i]
        for r in range(BM):
            tok = lax.shift_right_logical(order_ref[jnp.minimum(b0 + r, lim)], 1)
            xbuf[r] = hp_ref[tok]
        words = jnp.concatenate([xbuf[:, s, :] for s in range(PACK_ROWS)], axis=1)
        xb = _unpack_bf16(words)
        gate = jnp.dot(xb, wgb[...], preferred_element_type=jnp.float32)
        up = jnp.dot(xb, wub[...], preferred_element_type=jnp.float32)
        hid = gate * jax.nn.sigmoid(gate) * up
        y = jnp.dot(hid.astype(jnp.bfloat16), wdb[...], preferred_element_type=jnp.float32)
        _tile_store(ys_ref, y, BM)

    @pl.when(i >= nused_ref[0])
    def _():
        ys_ref[...] = jnp.zeros((BM * SUBLANES, LANES), jnp.float32)


def _expert_layer(layer, block_expert, nused, base, limit, order, h_packed, w_gate, w_up, w_down):
    def w_spec(shape):
        return pl.BlockSpec((1, 1) + shape, lambda i, be, *_: (layer, be[i], 0, 0))
    grid_spec = pltpu.PrefetchScalarGridSpec(
        num_scalar_prefetch=5,
        grid=(NB,),
        in_specs=[pl.BlockSpec((NTOK, PACK_ROWS, LANES), lambda i, *_: (0, 0, 0)),
                  w_spec((D, D_EXPERT)), w_spec((D, D_EXPERT)), w_spec((D_EXPERT, D))],
        out_specs=pl.BlockSpec((BM * SUBLANES, LANES), lambda i, *_: (i, 0)),
        scratch_shapes=[pltpu.VMEM((BM, PACK_ROWS, LANES), jnp.uint32),
                        pltpu.VMEM((D, D_EXPERT), jnp.bfloat16),
                        pltpu.VMEM((D, D_EXPERT), jnp.bfloat16),
                        pltpu.VMEM((D_EXPERT, D), jnp.bfloat16)],
    )
    return pl.pallas_call(
        _expert_kernel,
        grid_spec=grid_spec,
        out_shape=jax.ShapeDtypeStruct((N_SLOT * SUBLANES, LANES), jnp.float32),
        compiler_params=pltpu.CompilerParams(dimension_semantics=("arbitrary",),
                                             vmem_limit_bytes=EXPERT_VMEM_LIMIT),
        name="expert_mlp",
    )(block_expert, nused, base, limit, order, h_packed, w_gate, w_up, w_down)


def _combine_kernel(last_layer, slot_ref, h_ref, r_ref, ys_hbm, g_ref, b_ref, *rest):
    if last_layer:
        yp_ref, ysamp_ref, ybuf, sem = rest
    else:
        o_ref, ybuf, sem = rest
    i = pl.program_id(0)
    n = pl.num_programs(0)

    def start(step, buf):
        base = step * (TOP_K * TB)
        _tile_gather_start(ys_hbm, ybuf.at[buf], sem.at[buf], lambda r: slot_ref[base + r], TOP_K * TB,
                           dst_tile=lambda r: (r % TOP_K) * TB + r // TOP_K)

    @pl.when(i == 0)
    def _():
        start(0, 0)

    buf = i % 2
    _tile_gather_wait(ys_hbm, ybuf.at[buf], sem.at[buf], TOP_K * TB)
    yb = ybuf.at[buf]
    y0 = _tile_load(yb.at[pl.ds(0, TB * SUBLANES), :], TB)
    y1 = _tile_load(yb.at[pl.ds(TB * SUBLANES, TB * SUBLANES), :], TB)
    h = h_ref[...]
    r = r_ref[...]

    start(jnp.minimum(i + 1, n - 1), 1 - buf)

    f = r[:, 0:1] * y0 + r[:, 1:2] * y1
    out = _layer_norm(ALPHA * h + f, g_ref[...], b_ref[...])
    if last_layer:
        yp_ref[...] = out

        @pl.when(i == 0)
        def _():
            ysamp_ref[...] = out[N_HEAD:TB, :]
    else:
        o_ref[...] = out

    @pl.when(i == n - 1)
    def _():
        _tile_gather_wait(ys_hbm, ybuf.at[1 - buf], sem.at[1 - buf], TOP_K * TB)


def _combine_layer(last_layer, slot, h_tiles, r_all, ys, ln_g, ln_b):
    if last_layer:
        out_specs = [pl.BlockSpec((TB, D), lambda i, sl: (jnp.maximum(i - 1, 0), 0)),
                     pl.BlockSpec((DEC_BATCH, D), lambda i, sl: (0, 0))]
        out_shape = [jax.ShapeDtypeStruct((BATCH * SEQ, D), jnp.float32),
                     jax.ShapeDtypeStruct((DEC_BATCH, D), jnp.float32)]
    else:
        out_specs = pl.BlockSpec((TB, D), lambda i, sl: (i, 0))
        out_shape = jax.ShapeDtypeStruct((NTOK, D), jnp.float32)
    grid_spec = pltpu.PrefetchScalarGridSpec(
        num_scalar_prefetch=1,
        grid=(N_STEP,),
        in_specs=[pl.BlockSpec((TB, D), lambda i, sl: (i, 0)),
                  pl.BlockSpec((TB, LANES), lambda i, sl: (i, 0)),
                  pl.BlockSpec(memory_space=pl.ANY),
                  pl.BlockSpec((1, D), lambda i, sl: (0, 0)),
                  pl.BlockSpec((1, D), lambda i, sl: (0, 0))],
        out_specs=out_specs,
        scratch_shapes=[pltpu.VMEM((2, TOP_K * TB * SUBLANES, LANES), jnp.float32),
                        pltpu.SemaphoreType.DMA((2,))],
    )
    return pl.pallas_call(
        functools.partial(_combine_kernel, last_layer),
        grid_spec=grid_spec,
        out_shape=out_shape,
        compiler_params=pltpu.CompilerParams(dimension_semantics=("arbitrary",),
                                             vmem_limit_bytes=VMEM_LIMIT),
        name="moe_combine",
    )(slot, h_tiles, r_all, ys, ln_g, ln_b)


def _routing_metadata(rt, cnt):
    experts = jnp.arange(N_EXPERTS, dtype=jnp.int32)
    counts = cnt[0, :N_EXPERTS].astype(jnp.int32)
    nblk = (counts + BM - 1) // BM
    blk_end = jnp.cumsum(nblk)
    pstart = (blk_end - nblk) * BM
    starts = jnp.cumsum(counts) - counts
    nused = blk_end[-1]
    blk = jnp.arange(NB, dtype=jnp.int32)
    block_expert = jnp.sum((blk[:, None] >= blk_end[None, :]).astype(jnp.int32), axis=1)
    last_expert = jnp.max(jnp.where(nblk > 0, experts, 0))
    block_expert = jnp.where(blk < nused, block_expert, last_expert).astype(jnp.int32)
    of_block = block_expert[:, None] == experts[None, :]
    base = blk * BM + jnp.sum(jnp.where(of_block, (starts - pstart)[None, :], 0), axis=1)
    limit = jnp.sum(jnp.where(of_block, (starts + counts - 1)[None, :], 0), axis=1)
    rows = rt.reshape(N_STEP, SUBLANES, TB)
    expert = jnp.transpose(rows[:, 2:2 + TOP_K, :], (0, 2, 1)).reshape(N_ASSIGN).astype(jnp.int32)
    rank = jnp.transpose(rows[:, 2 + TOP_K:2 + 2 * TOP_K, :], (0, 2, 1)).reshape(N_ASSIGN).astype(jnp.int32)
    slot = rank + jnp.sum(jnp.where(expert[:, None] == experts[None, :], pstart[None, :], 0), axis=1)
    order = jnp.argsort(expert, stable=True).astype(jnp.int32)
    return (block_expert, nused.reshape(1).astype(jnp.int32), base.astype(jnp.int32),
            limit.astype(jnp.int32), order, slot.astype(jnp.int32))


def kernel(x_prompt, x_sample, state_pool, state_conv, meta, w_pool, pool_scale, w_glu, b_glu, w_dw, b_dw, conv_ln_g, conv_ln_b, w_pw, b_pw, ln_mix_g, ln_mix_b, ln_ffn_g, ln_ffn_b, w_router_group, w_router_expert, w_gate, w_up, w_down):
    f32 = jnp.float32
    x_samp2d = x_sample.reshape(DEC_BATCH, D)
    head = jnp.concatenate([jnp.tile(meta.astype(f32), (BATCH, 1)), x_samp2d], axis=0)
    x_in = x_prompt.reshape(BATCH * SEQ, D)

    k_idx = jnp.arange(POOL_STATE)[:, None]
    win = jnp.repeat(jnp.asarray(POOL_WINDOWS), POOL_GROUP_DIM)[None, :]
    pool_coef = (k_idx >= (POOL_STATE + 1 - win)).astype(f32)

    new_pool_p, new_pool_s, new_conv_p, new_conv_s = [], [], [], []
    for i in range(DEPTH):
        j = i // 2
        first = i == 0
        wr32 = jnp.concatenate([w_router_group[i], w_router_expert[i],
                                jnp.zeros((D, LANES - N_GROUPS - N_EXPERTS), f32)], axis=1)
        wr_hi = wr32.astype(jnp.bfloat16)
        wr = jnp.concatenate([wr_hi, (wr32 - wr_hi.astype(f32)).astype(jnp.bfloat16)], axis=1)
        ln_g, ln_b = ln_mix_g[i].reshape(1, D), ln_mix_b[i].reshape(1, D)
        if i % 2 == 0:
            if first:
                tail = x_prompt[:, SEQ - POOL_STATE:]
                x_s = x_samp2d
            else:
                tail = jnp.stack([x_in[TB + (b + 1) * SEQ - POOL_STATE:TB + (b + 1) * SEQ]
                                  for b in range(BATCH)])
                x_s = x_in[N_HEAD:TB]
            new_pool_p.append(tail)
            new_pool_s.append(jnp.concatenate([state_pool[j][:, 1:], x_s[:, None, :]], axis=1))
            ps = _state_presum(j, state_pool, pool_coef)
            h1, r_all, cnt, h1p, rt = _pool_layer(first, head, x_in, ps, w_pool[j].astype(jnp.bfloat16),
                                              pool_scale[j].reshape(1, D), ln_g, ln_b, wr)
        else:
            vs = _state_presum(j, state_conv, w_dw[j][:CONV_STATE])
            h1, r_all, cnt, h1p, rt, ust, u_s = _conv_layer(
                first, head, x_in, vs, w_glu[j].astype(jnp.bfloat16), b_glu[j].reshape(1, 2 * D), w_dw[j],
                b_dw[j].reshape(1, D), conv_ln_g[j].reshape(1, D), conv_ln_b[j].reshape(1, D),
                w_pw[j].astype(jnp.bfloat16), b_pw[j].reshape(1, D), ln_g, ln_b, wr)
            new_conv_p.append(ust)
            new_conv_s.append(jnp.concatenate([state_conv[j][:, 1:], u_s[:, None, :]], axis=1))
        block_expert, nused, base, limit, order, slot = _routing_metadata(rt, cnt)
        ys = _expert_layer(i, block_expert, nused, base, limit, order, h1p, w_gate, w_up, w_down)
        x_in = _combine_layer(i == DEPTH - 1, slot, h1, r_all, ys,
                              ln_ffn_g[i].reshape(1, D), ln_ffn_b[i].reshape(1, D))

    y_prompt, y_samp = x_in
    return (y_prompt.reshape(BATCH, SEQ, D), y_samp.reshape(DEC_BATCH, 1, D),
            jnp.stack(new_pool_p), jnp.stack(new_pool_s), jnp.stack(new_conv_p), jnp.stack(new_conv_s))
```

```python
import functools

import jax
import jax.numpy as jnp
from jax import lax
from jax.experimental import pallas as pl
from jax.experimental.pallas import tpu as pltpu

D = 1024
BATCH = 8
SEQ = 2048
DEPTH = 4
DEC_BATCH = 128
N_META = 16
POOL_WINDOWS = (2, 4, 8, 16)
POOL_GROUP_DIM = D // len(POOL_WINDOWS)
POOL_STATE = max(POOL_WINDOWS) - 1
CONV_WIDTH = 31
CONV_STATE = CONV_WIDTH - 1
N_GROUPS = 4
EXPERTS_PER_GROUP = 8
N_EXPERTS = N_GROUPS * EXPERTS_PER_GROUP
TOP_K = 2
D_EXPERT = D // 2
ALPHA = (2.0 * DEPTH) ** 0.25
LN_EPS = 1e-5

LANES = 128
SUBLANES = 8
N_HEAD = BATCH * N_META
TB = 256
NTOK = N_HEAD + DEC_BATCH + BATCH * SEQ
N_STEP = NTOK // TB
CHUNKS = SEQ // TB
POOL_CARRY = 16
CONV_CARRY = 32
CONV_RB = 128
CONV_CB = 128
BM = 512
N_ASSIGN = NTOK * TOP_K
NB = N_ASSIGN // BM + N_EXPERTS
N_SLOT = NB * BM
VMEM_LIMIT = 48 * 1024 * 1024
EXPERT_VMEM_LIMIT = 56 * 1024 * 1024
NEG = -1e30

assert D == SUBLANES * LANES and N_HEAD + DEC_BATCH == TB and SEQ % TB == 0
assert N_ASSIGN % BM == 0 and TB % CONV_RB == 0 and N_META == POOL_CARRY and CONV_STATE <= TB


def _tile_load(ref, n):
    return jnp.concatenate([ref[pl.ds(s, n, stride=SUBLANES), :] for s in range(SUBLANES)], axis=1)


def _tile_store(ref, val, n):
    for s in range(SUBLANES):
        ref[pl.ds(s, n, stride=SUBLANES), :] = val[:, s * LANES:(s + 1) * LANES]


def _layer_norm(x, g, b):
    mu = jnp.mean(x, axis=-1, keepdims=True)
    xc = x - mu
    var = jnp.mean(xc * xc, axis=-1, keepdims=True)
    return xc * lax.rsqrt(var + LN_EPS) * g + b


PACK_ROWS = D // 2 // LANES


def _pack_bf16(h_bf16, hp_ref, r0):
    n = h_bf16.shape[0]
    bits = pltpu.bitcast(h_bf16.astype(jnp.float32), jnp.uint32)
    words = jnp.bitwise_or(lax.shift_right_logical(bits[:, :D // 2], jnp.uint32(16)),
                           jnp.bitwise_and(bits[:, D // 2:], jnp.uint32(0xFFFF0000)))
    for s in range(PACK_ROWS):
        hp_ref[r0:r0 + n, s, :] = words[:, s * LANES:(s + 1) * LANES]


def _unpack_bf16(words):
    lo = pltpu.bitcast(lax.shift_left(words, jnp.uint32(16)), jnp.float32)
    hi = pltpu.bitcast(jnp.bitwise_and(words, jnp.uint32(0xFFFF0000)), jnp.float32)
    return jnp.concatenate([lo, hi], axis=1).astype(jnp.bfloat16)


HB = TB


def _route_init(cnt_ref, ltri_ref):
    @pl.when(pl.program_id(0) == 0)
    def _():
        cnt_ref[...] = jnp.zeros((1, LANES), jnp.float32)
        row = lax.broadcasted_iota(jnp.int32, (HB, HB), 0)
        col = lax.broadcasted_iota(jnp.int32, (HB, HB), 1)
        ltri_ref[...] = jnp.where(col < row, 1.0, 0.0).astype(jnp.bfloat16)


def _residual_route(x, m, r0, g_ref, b_ref, wr_ref, h_ref, r_ref, cnt_out_ref, hp_ref, rt_ref, cnt_ref,
                    ltri_ref):
    h = _layer_norm(ALPHA * x + m, g_ref[...], b_ref[...])
    h_ref[r0:r0 + HB, :] = h
    h_hi = h.astype(jnp.bfloat16)
    _pack_bf16(h_hi, hp_ref, r0)
    h_lo = (h - h_hi.astype(jnp.float32)).astype(jnp.bfloat16)
    hi_both = jnp.dot(h_hi, wr_ref[...], preferred_element_type=jnp.float32)
    lo_hi = jnp.dot(h_lo, wr_ref[:, 0:LANES], preferred_element_type=jnp.float32)
    logits = hi_both[:, 0:LANES] + (hi_both[:, LANES:2 * LANES] + lo_hi)
    lane = lax.broadcasted_iota(jnp.int32, logits.shape, 1)
    lanef = lane.astype(jnp.float32)
    big = jnp.float32(1e9)
    lg = jnp.where(lane < N_GROUPS, logits, NEG)
    mg = jnp.max(lg, axis=1, keepdims=True)
    gidx = jnp.min(jnp.where(lg == mg, lanef, big), axis=1, keepdims=True)
    p_grp = 1.0 / jnp.sum(jnp.where(lane < N_GROUPS, jnp.exp(lg - mg), 0.0), axis=1, keepdims=True)
    lo = N_GROUPS + EXPERTS_PER_GROUP * gidx
    le = jnp.where(lanef >= lo, jnp.where(lanef < lo + EXPERTS_PER_GROUP, logits, NEG), NEG)
    m1 = jnp.max(le, axis=1, keepdims=True)
    i1 = jnp.min(jnp.where(le == m1, lanef, big), axis=1, keepdims=True)
    le2 = jnp.where(lanef == i1, NEG, le)
    m2 = jnp.max(le2, axis=1, keepdims=True)
    i2 = jnp.min(jnp.where(le2 == m2, lanef, big), axis=1, keepdims=True)
    ratio = jnp.exp(m2 - m1)
    g1 = p_grp / (1.0 + ratio)
    g2 = g1 * ratio
    e1 = i1 - N_GROUPS
    e2 = i2 - N_GROUPS

    is1 = lanef == e1
    is2 = lanef == e2
    onehot = jnp.where(is1, 1.0, jnp.where(is2, 1.0, 0.0))
    before = jnp.dot(ltri_ref[...], onehot.astype(jnp.bfloat16),
                     preferred_element_type=jnp.float32) + cnt_ref[...]
    rank1 = jnp.sum(jnp.where(is1, before, 0.0), axis=1, keepdims=True)
    rank2 = jnp.sum(jnp.where(is2, before, 0.0), axis=1, keepdims=True)
    cnt = cnt_ref[...] + jnp.sum(onehot, axis=0, keepdims=True)
    cnt_ref[...] = cnt
    cnt_out_ref[...] = cnt

    packed = jnp.where(lane == 0, g1,
                       jnp.where(lane == 1, g2,
                                 jnp.where(lane == 2, e1,
                                           jnp.where(lane == 3, e2,
                                                     jnp.where(lane == 4, rank1,
                                                               jnp.where(lane == 5, rank2, 0.0))))))
    r_ref[r0:r0 + HB, :] = packed
    rt_ref[:, r0:r0 + HB] = jnp.transpose(packed)[0:SUBLANES, :]


def _seq_of_step(s):
    sm1 = jnp.maximum(s - 1, 0)
    return lax.shift_right_logical(sm1, CHUNKS.bit_length() - 1), jnp.bitwise_and(sm1, CHUNKS - 1)


assert CHUNKS & (CHUNKS - 1) == 0


PRESUM_ROWS = 32


def _presum_kernel(st_ref, coef_ref, o_ref):
    coef = coef_ref[...]
    for n in range(PRESUM_ROWS):
        o_ref[n:n + 1, :] = jnp.sum(st_ref[0, n] * coef, axis=0, keepdims=True)


def _state_presum(j, state, coef):
    _, n, k, _ = state.shape
    return pl.pallas_call(
        _presum_kernel,
        grid=(n // PRESUM_ROWS,),
        in_specs=[pl.BlockSpec((1, PRESUM_ROWS, k, D), lambda i: (j, i, 0, 0)),
                  pl.BlockSpec((k, D), lambda i: (0, 0))],
        out_specs=pl.BlockSpec((PRESUM_ROWS, D), lambda i: (i, 0)),
        out_shape=jax.ShapeDtypeStruct((n, D), jnp.float32),
        compiler_params=pltpu.CompilerParams(dimension_semantics=("arbitrary",),
                                             vmem_limit_bytes=VMEM_LIMIT),
        name="state_presum",
    )(state, coef)


def _pool_kernel(first_layer, head_ref, x_ref, ps_ref, wp_ref, sc_ref, g_ref, b_ref, wr_ref,
                 h_ref, r_ref, cnt_out_ref, hp_ref, rt_ref, ext_ref, m_ref, carry_ref, cnt_ref, ltri_ref):
    s = pl.program_id(0)
    seq, chunk = _seq_of_step(s)
    _route_init(cnt_ref, ltri_ref)
    route_refs = (g_ref, b_ref, wr_ref, h_ref, r_ref, cnt_out_ref, hp_ref, rt_ref, cnt_ref, ltri_ref)

    @pl.when(s == 0)
    def _():
        x = head_ref[...] if first_layer else x_ref[...]
        ext_ref[0:POOL_CARRY, :] = jnp.zeros((POOL_CARRY, D), jnp.float32)
        ext_ref[POOL_CARRY:POOL_CARRY + TB, :] = x
        carry_ref[...] = x[0:N_HEAD, :]
        pos = jnp.bitwise_and(lax.broadcasted_iota(jnp.int32, (N_HEAD, 1), 0), N_META - 1)
        for g, w in enumerate(POOL_WINDOWS):
            lo, hi = g * POOL_GROUP_DIM, (g + 1) * POOL_GROUP_DIM
            xm = x[0:N_HEAD, lo:hi]
            acc = xm
            for k in range(1, w):
                acc = acc + jnp.where(pos >= k, ext_ref[POOL_CARRY - k:POOL_CARRY - k + N_HEAD, lo:hi], 0.0)
            cnt = jnp.minimum(pos + 1, w).astype(jnp.float32)
            d_meta = acc / cnt - xm
            xs = x[N_HEAD:TB, lo:hi]
            d_samp = (ps_ref[:, lo:hi] + xs) / float(w) - xs
            diff = jnp.concatenate([d_meta, d_samp], axis=0)
            y = jnp.dot(diff.astype(jnp.bfloat16), wp_ref[g], preferred_element_type=jnp.float32)
            m_ref[:, lo:hi] = y * sc_ref[:, lo:hi]
        for r0 in range(0, TB, HB):
            _residual_route(x[r0:r0 + HB], m_ref[r0:r0 + HB, :], r0, *route_refs)

    @pl.when(s > 0)
    def _():
        @pl.when(chunk == 0)
        def _():
            ext_ref[0:POOL_CARRY, :] = carry_ref[pl.ds(pl.multiple_of(seq * N_META, N_META), N_META), :]

        for r0 in range(0, TB, HB):
            x = x_ref[r0:r0 + HB, :]
            ext_ref[POOL_CARRY + r0:POOL_CARRY + r0 + HB, :] = x
            ms = []
            for g, w in enumerate(POOL_WINDOWS):
                lo, hi = g * POOL_GROUP_DIM, (g + 1) * POOL_GROUP_DIM
                xg = x[:, lo:hi]
                acc = xg
                for k in range(1, w):
                    acc = acc + ext_ref[POOL_CARRY + r0 - k:POOL_CARRY + r0 - k + HB, lo:hi]
                diff = acc / float(w) - xg
                y = jnp.dot(diff.astype(jnp.bfloat16), wp_ref[g], preferred_element_type=jnp.float32)
                ms.append(y * sc_ref[:, lo:hi])
            _residual_route(x, jnp.concatenate(ms, axis=1), r0, *route_refs)
        ext_ref[0:POOL_CARRY, :] = ext_ref[TB:TB + POOL_CARRY, :]


assert N_META >= max(POOL_WINDOWS)


def _const_spec(shape):
    nd = len(shape)
    return pl.BlockSpec(shape, lambda s: (0,) * nd)


def _x_spec(first_layer):
    if first_layer:
        return pl.BlockSpec((TB, D), lambda s: (jnp.maximum(s - 1, 0), 0))
    return pl.BlockSpec((TB, D), lambda s: (s, 0))


_MIXER_OUT_SPECS = [pl.BlockSpec((TB, D), lambda s: (s, 0)),
                    pl.BlockSpec((TB, LANES), lambda s: (s, 0)),
                    pl.BlockSpec((1, LANES), lambda s: (0, 0)),
                    pl.BlockSpec((TB, PACK_ROWS, LANES), lambda s: (s, 0, 0)),
                    pl.BlockSpec((SUBLANES, TB), lambda s: (s, 0))]
_MIXER_OUT_SHAPES = [jax.ShapeDtypeStruct((NTOK, D), jnp.float32),
                     jax.ShapeDtypeStruct((NTOK, LANES), jnp.float32),
                     jax.ShapeDtypeStruct((1, LANES), jnp.float32),
                     jax.ShapeDtypeStruct((NTOK, PACK_ROWS, LANES), jnp.uint32),
                     jax.ShapeDtypeStruct((N_STEP * SUBLANES, TB), jnp.float32)]


def _pool_layer(first_layer, head, x, ps, wp_bf, scale, ln_g, ln_b, wr):
    return pl.pallas_call(
        functools.partial(_pool_kernel, first_layer),
        grid=(N_STEP,),
        in_specs=[_const_spec((TB, D)), _x_spec(first_layer),
                  _const_spec((DEC_BATCH, D)),
                  _const_spec((len(POOL_WINDOWS), POOL_GROUP_DIM, POOL_GROUP_DIM)),
                  _const_spec((1, D)), _const_spec((1, D)), _const_spec((1, D)),
                  _const_spec((D, 2 * LANES))],
        out_specs=_MIXER_OUT_SPECS,
        out_shape=_MIXER_OUT_SHAPES,
        scratch_shapes=[pltpu.VMEM((POOL_CARRY + TB, D), jnp.float32),
                        pltpu.VMEM((TB, D), jnp.float32),
                        pltpu.VMEM((N_HEAD, D), jnp.float32),
                        pltpu.VMEM((1, LANES), jnp.float32),
                        pltpu.VMEM((HB, HB), jnp.bfloat16)],
        compiler_params=pltpu.CompilerParams(dimension_semantics=("arbitrary",),
                                             vmem_limit_bytes=VMEM_LIMIT),
        name="pool_mixer",
    )(head, x, ps, wp_bf, scale, ln_g, ln_b, wr)


def _depthwise_conv(ext_ref, wdw_ref, v_ref, r0):
    base = CONV_CARRY - CONV_STATE
    for cb in range(D // CONV_CB):
        lo, hi = cb * CONV_CB, (cb + 1) * CONV_CB
        v = None
        for r in range(SUBLANES):
            qs = [q for q in range((base + CONV_WIDTH) // SUBLANES + 1)
                  if 0 <= SUBLANES * q + r - base < CONV_WIDTH]
            z0 = r0 + SUBLANES * qs[0] + r
            z = ext_ref[z0:r0 + SUBLANES * qs[-1] + r + CONV_RB, lo:hi]
            p = None
            for q in qs:
                k = SUBLANES * q + r - base
                off = SUBLANES * (q - qs[0])
                term = wdw_ref[k:k + 1, lo:hi] * z[off:off + CONV_RB]
                p = term if p is None else p + term
            v = p if v is None else v + p
        v_ref[r0:r0 + CONV_RB, lo:hi] = v


assert HB % CONV_RB == 0


def _conv_kernel(first_layer, head_ref, x_ref, vs_ref, wglu_ref, bglu_ref, wdw_ref, bdw_ref, lg_ref, lb_ref,
                 wpw_ref, bpw_ref, g_ref, b_ref, wr_ref, h_ref, r_ref, cnt_out_ref, hp_ref, rt_ref, ust_ref, us_ref,
                 ext_ref, v_ref, carry_ref, cnt_ref, ltri_ref):
    s = pl.program_id(0)
    seq, chunk = _seq_of_step(s)
    _route_init(cnt_ref, ltri_ref)
    route_refs = (g_ref, b_ref, wr_ref, h_ref, r_ref, cnt_out_ref, hp_ref, rt_ref, cnt_ref, ltri_ref)

    def glu(x):
        hh = jnp.dot(x.astype(jnp.bfloat16), wglu_ref[...], preferred_element_type=jnp.float32) + bglu_ref[...]
        return hh[:, :D] * jax.nn.sigmoid(hh[:, D:])

    def finish(x, v, r0):
        v = _layer_norm(v + bdw_ref[...], lg_ref[...], lb_ref[...])
        v = v * jax.nn.sigmoid(v)
        m = jnp.dot(v.astype(jnp.bfloat16), wpw_ref[...], preferred_element_type=jnp.float32) + bpw_ref[...]
        _residual_route(x, m, r0, *route_refs)

    @pl.when(s == 0)
    def _():
        x = head_ref[...] if first_layer else x_ref[...]
        u = glu(x)
        um = u[0:N_HEAD, :]
        us = u[N_HEAD:TB, :]
        carry_ref[...] = um
        us_ref[...] = us
        ext_ref[0:N_META, :] = jnp.zeros((N_META, D), jnp.float32)
        ext_ref[N_META:N_META + N_HEAD, :] = um
        pos = jnp.bitwise_and(lax.broadcasted_iota(jnp.int32, (N_HEAD, 1), 0), N_META - 1)
        acc = wdw_ref[CONV_WIDTH - 1:CONV_WIDTH, :] * um
        for d in range(1, N_META):
            k = CONV_WIDTH - 1 - d
            acc = acc + wdw_ref[k:k + 1, :] * jnp.where(pos >= d, ext_ref[N_META - d:N_META - d + N_HEAD, :], 0.0)
        v_samp = vs_ref[...] + wdw_ref[CONV_WIDTH - 1:CONV_WIDTH, :] * us
        finish(x, jnp.concatenate([acc, v_samp], axis=0), 0)

    @pl.when(s > 0)
    def _():
        @pl.when(chunk == 0)
        def _():
            ext_ref[0:CONV_CARRY - N_META, :] = jnp.zeros((CONV_CARRY - N_META, D), jnp.float32)
            ext_ref[CONV_CARRY - N_META:CONV_CARRY, :] = carry_ref[
                pl.ds(pl.multiple_of(seq * N_META, N_META), N_META), :]

        for r0 in range(0, TB, HB):
            x = x_ref[r0:r0 + HB, :]
            ext_ref[CONV_CARRY + r0:CONV_CARRY + r0 + HB, :] = glu(x)
            for rb in range(r0, r0 + HB, CONV_RB):
                _depthwise_conv(ext_ref, wdw_ref, v_ref, rb)
            finish(x, v_ref[r0:r0 + HB, :], r0)
        ext_ref[0:CONV_CARRY, :] = ext_ref[TB:TB + CONV_CARRY, :]

        @pl.when(chunk == CHUNKS - 1)
        def _():
            ust_ref[0] = ext_ref[CONV_CARRY + TB - CONV_STATE:CONV_CARRY + TB, :]


assert CONV_CARRY - N_META + N_META >= CONV_STATE and N_META <= CONV_STATE


def _conv_layer(first_layer, head, x, vs, wglu_bf, bglu, wdw, bdw, cln_g, cln_b, wpw_bf, bpw, ln_g, ln_b, wr):
    return pl.pallas_call(
        functools.partial(_conv_kernel, first_layer),
        grid=(N_STEP,),
        in_specs=[_const_spec((TB, D)), _x_spec(first_layer),
                  _const_spec((DEC_BATCH, D)),
                  _const_spec((D, 2 * D)), _const_spec((1, 2 * D)),
                  _const_spec((CONV_WIDTH, D)), _const_spec((1, D)),
                  _const_spec((1, D)), _const_spec((1, D)),
                  _const_spec((D, D)), _const_spec((1, D)),
                  _const_spec((1, D)), _const_spec((1, D)),
                  _const_spec((D, 2 * LANES))],
        out_specs=_MIXER_OUT_SPECS + [
            pl.BlockSpec((1, CONV_STATE, D), lambda s: (jnp.maximum(s - 1, 0) // CHUNKS, 0, 0)),
            _const_spec((DEC_BATCH, D))],
        out_shape=_MIXER_OUT_SHAPES + [
            jax.ShapeDtypeStruct((BATCH, CONV_STATE, D), jnp.float32),
            jax.ShapeDtypeStruct((DEC_BATCH, D), jnp.float32)],
        scratch_shapes=[pltpu.VMEM((CONV_CARRY + TB, D), jnp.float32),
                        pltpu.VMEM((TB, D), jnp.float32),
                        pltpu.VMEM((N_HEAD, D), jnp.float32),
                        pltpu.VMEM((1, LANES), jnp.float32),
                        pltpu.VMEM((HB, HB), jnp.bfloat16)],
        compiler_params=pltpu.CompilerParams(dimension_semantics=("arbitrary",),
                                             vmem_limit_bytes=VMEM_LIMIT),
        name="conv_mixer",
    )(head, x, vs, wglu_bf, bglu, wdw, bdw, cln_g, cln_b, wpw_bf, bpw, ln_g, ln_b, wr)


def _tile_gather_start(src_hbm, dst, sem, row_index, n_rows, dst_tile=lambda r: r):
    for r in range(n_rows):
        tok = row_index(r)
        pltpu.make_async_copy(src_hbm.at[pl.ds(pl.multiple_of(tok * SUBLANES, SUBLANES), SUBLANES), :],
                              dst.at[pl.ds(dst_tile(r) * SUBLANES, SUBLANES), :], sem).start(priority=r % 2)


def _tile_gather_wait(src_hbm, dst, sem, n_rows):
    pltpu.make_async_copy(src_hbm.at[pl.ds(0, n_rows * SUBLANES), :], dst, sem).wait()


assert TOP_K == 2


def _expert_kernel(be_ref, nused_ref, base_ref, limit_ref, order_ref, hp_ref, wg_ref, wu_ref, wd_ref, ys_ref,
                   xbuf_even, xbuf_odd, wgb, wub, wdb):
    i = pl.program_id(0)
    nused = nused_ref[0]

    def gather(blk, xbuf):
        b0 = base_ref[blk]
        lim = limit_ref[blk]
        for r in range(BM):
            tok = lax.shift_right_logical(order_ref[jnp.minimum(b0 + r, lim)], 1)
            xbuf[r] = hp_ref[tok]

    def block(cur, nxt):
        gather(jnp.minimum(i + 1, jnp.maximum(nused - 1, 0)), nxt)
        words = jnp.concatenate([cur[:, s, :] for s in range(PACK_ROWS)], axis=1)
        xb = _unpack_bf16(words)
        gate = jnp.dot(xb, wgb[...], preferred_element_type=jnp.float32)
        up = jnp.dot(xb, wub[...], preferred_element_type=jnp.float32)
        hid = gate * jax.nn.sigmoid(gate) * up
        y = jnp.dot(hid.astype(jnp.bfloat16), wdb[...], preferred_element_type=jnp.float32)
        _tile_store(ys_ref, y, BM)

    @pl.when(i == 0)
    def _():
        gather(0, xbuf_even)

    @pl.when(jnp.logical_and(i < nused, jnp.logical_or(i == 0, be_ref[i] != be_ref[jnp.maximum(i - 1, 0)])))
    def _():
        wgb[...] = wg_ref[0, 0].astype(jnp.bfloat16)
        wub[...] = wu_ref[0, 0].astype(jnp.bfloat16)
        wdb[...] = wd_ref[0, 0].astype(jnp.bfloat16)

    odd = jnp.bitwise_and(i, 1) == 1

    @pl.when(jnp.logical_and(i < nused, jnp.logical_not(odd)))
    def _():
        block(xbuf_even, xbuf_odd)

    @pl.when(jnp.logical_and(i < nused, odd))
    def _():
        block(xbuf_odd, xbuf_even)

    @pl.when(i >= nused)
    def _():
        ys_ref[...] = jnp.zeros((BM * SUBLANES, LANES), jnp.float32)


def _expert_layer(layer, block_expert, nused, base, limit, order, h_packed, w_gate, w_up, w_down):
    def w_spec(shape):
        return pl.BlockSpec((1, 1) + shape, lambda i, be, *_: (layer, be[i], 0, 0))
    grid_spec = pltpu.PrefetchScalarGridSpec(
        num_scalar_prefetch=5,
        grid=(NB,),
        in_specs=[pl.BlockSpec((NTOK, PACK_ROWS, LANES), lambda i, *_: (0, 0, 0)),
                  w_spec((D, D_EXPERT)), w_spec((D, D_EXPERT)), w_spec((D_EXPERT, D))],
        out_specs=pl.BlockSpec((BM * SUBLANES, LANES), lambda i, *_: (i, 0)),
        scratch_shapes=[pltpu.VMEM((BM, PACK_ROWS, LANES), jnp.uint32),
                        pltpu.VMEM((BM, PACK_ROWS, LANES), jnp.uint32),
                        pltpu.VMEM((D, D_EXPERT), jnp.bfloat16),
                        pltpu.VMEM((D, D_EXPERT), jnp.bfloat16),
                        pltpu.VMEM((D_EXPERT, D), jnp.bfloat16)],
    )
    return pl.pallas_call(
        _expert_kernel,
        grid_spec=grid_spec,
        out_shape=jax.ShapeDtypeStruct((N_SLOT * SUBLANES, LANES), jnp.float32),
        compiler_params=pltpu.CompilerParams(dimension_semantics=("arbitrary",),
                                             vmem_limit_bytes=EXPERT_VMEM_LIMIT),
        name="expert_mlp",
    )(block_expert, nused, base, limit, order, h_packed, w_gate, w_up, w_down)


def _combine_kernel(last_layer, slot_ref, h_ref, r_ref, ys_hbm, g_ref, b_ref, *rest):
    if last_layer:
        yp_ref, ysamp_ref, ybuf, sem = rest
    else:
        o_ref, ybuf, sem = rest
    i = pl.program_id(0)
    n = pl.num_programs(0)

    def start(step, buf):
        base = step * (TOP_K * TB)
        _tile_gather_start(ys_hbm, ybuf.at[buf], sem.at[buf], lambda r: slot_ref[base + r], TOP_K * TB,
                           dst_tile=lambda r: (r % TOP_K) * TB + r // TOP_K)

    @pl.when(i == 0)
    def _():
        start(0, 0)

    buf = i % 2
    _tile_gather_wait(ys_hbm, ybuf.at[buf], sem.at[buf], TOP_K * TB)
    yb = ybuf.at[buf]
    y0 = _tile_load(yb.at[pl.ds(0, TB * SUBLANES), :], TB)
    y1 = _tile_load(yb.at[pl.ds(TB * SUBLANES, TB * SUBLANES), :], TB)
    h = h_ref[...]
    r = r_ref[...]

    start(jnp.minimum(i + 1, n - 1), 1 - buf)

    f = r[:, 0:1] * y0 + r[:, 1:2] * y1
    out = _layer_norm(ALPHA * h + f, g_ref[...], b_ref[...])
    if last_layer:
        yp_ref[...] = out

        @pl.when(i == 0)
        def _():
            ysamp_ref[...] = out[N_HEAD:TB, :]
    else:
        o_ref[...] = out

    @pl.when(i == n - 1)
    def _():
        _tile_gather_wait(ys_hbm, ybuf.at[1 - buf], sem.at[1 - buf], TOP_K * TB)


def _combine_layer(last_layer, slot, h_tiles, r_all, ys, ln_g, ln_b):
    if last_layer:
        out_specs = [pl.BlockSpec((TB, D), lambda i, sl: (jnp.maximum(i - 1, 0), 0)),
                     pl.BlockSpec((DEC_BATCH, D), lambda i, sl: (0, 0))]
        out_shape = [jax.ShapeDtypeStruct((BATCH * SEQ, D), jnp.float32),
                     jax.ShapeDtypeStruct((DEC_BATCH, D), jnp.float32)]
    else:
        out_specs = pl.BlockSpec((TB, D), lambda i, sl: (i, 0))
        out_shape = jax.ShapeDtypeStruct((NTOK, D), jnp.float32)
    grid_spec = pltpu.PrefetchScalarGridSpec(
        num_scalar_prefetch=1,
        grid=(N_STEP,),
        in_specs=[pl.BlockSpec((TB, D), lambda i, sl: (i, 0)),
                  pl.BlockSpec((TB, LANES), lambda i, sl: (i, 0)),
                  pl.BlockSpec(memory_space=pl.ANY),
                  pl.BlockSpec((1, D), lambda i, sl: (0, 0)),
                  pl.BlockSpec((1, D), lambda i, sl: (0, 0))],
        out_specs=out_specs,
        scratch_shapes=[pltpu.VMEM((2, TOP_K * TB * SUBLANES, LANES), jnp.float32),
                        pltpu.SemaphoreType.DMA((2,))],
    )
    return pl.pallas_call(
        functools.partial(_combine_kernel, last_layer),
        grid_spec=grid_spec,
        out_shape=out_shape,
        compiler_params=pltpu.CompilerParams(dimension_semantics=("arbitrary",),
                                             vmem_limit_bytes=VMEM_LIMIT),
        name="moe_combine",
    )(slot, h_tiles, r_all, ys, ln_g, ln_b)


def _routing_metadata(rt, cnt):
    experts = jnp.arange(N_EXPERTS, dtype=jnp.int32)
    counts = cnt[0, :N_EXPERTS].astype(jnp.int32)
    nblk = (counts + BM - 1) // BM
    blk_end = jnp.cumsum(nblk)
    pstart = (blk_end - nblk) * BM
    starts = jnp.cumsum(counts) - counts
    nused = blk_end[-1]
    blk = jnp.arange(NB, dtype=jnp.int32)
    block_expert = jnp.sum((blk[:, None] >= blk_end[None, :]).astype(jnp.int32), axis=1)
    last_expert = jnp.max(jnp.where(nblk > 0, experts, 0))
    block_expert = jnp.where(blk < nused, block_expert, last_expert).astype(jnp.int32)
    of_block = block_expert[:, None] == experts[None, :]
    base = blk * BM + jnp.sum(jnp.where(of_block, (starts - pstart)[None, :], 0), axis=1)
    limit = jnp.sum(jnp.where(of_block, (starts + counts - 1)[None, :], 0), axis=1)
    rows = rt.reshape(N_STEP, SUBLANES, TB)
    expert = jnp.transpose(rows[:, 2:2 + TOP_K, :], (0, 2, 1)).reshape(N_ASSIGN).astype(jnp.int32)
    rank = jnp.transpose(rows[:, 2 + TOP_K:2 + 2 * TOP_K, :], (0, 2, 1)).reshape(N_ASSIGN).astype(jnp.int32)
    slot = rank + jnp.sum(jnp.where(expert[:, None] == experts[None, :], pstart[None, :], 0), axis=1)
    order = jnp.argsort(expert, stable=True).astype(jnp.int32)
    return (block_expert, nused.reshape(1).astype(jnp.int32), base.astype(jnp.int32),
            limit.astype(jnp.int32), order, slot.astype(jnp.int32))


def kernel(x_prompt, x_sample, state_pool, state_conv, meta, w_pool, pool_scale, w_glu, b_glu, w_dw, b_dw, conv_ln_g, conv_ln_b, w_pw, b_pw, ln_mix_g, ln_mix_b, ln_ffn_g, ln_ffn_b, w_router_group, w_router_expert, w_gate, w_up, w_down):
    f32 = jnp.float32
    x_samp2d = x_sample.reshape(DEC_BATCH, D)
    head = jnp.concatenate([jnp.tile(meta.astype(f32), (BATCH, 1)), x_samp2d], axis=0)
    x_in = x_prompt.reshape(BATCH * SEQ, D)

    k_idx = jnp.arange(POOL_STATE)[:, None]
    win = jnp.repeat(jnp.asarray(POOL_WINDOWS), POOL_GROUP_DIM)[None, :]
    pool_coef = (k_idx >= (POOL_STATE + 1 - win)).astype(f32)

    new_pool_p, new_pool_s, new_conv_p, new_conv_s = [], [], [], []
    for i in range(DEPTH):
        j = i // 2
        first = i == 0
        wr32 = jnp.concatenate([w_router_group[i], w_router_expert[i],
                                jnp.zeros((D, LANES - N_GROUPS - N_EXPERTS), f32)], axis=1)
        wr_hi = wr32.astype(jnp.bfloat16)
        wr = jnp.concatenate([wr_hi, (wr32 - wr_hi.astype(f32)).astype(jnp.bfloat16)], axis=1)
        ln_g, ln_b = ln_mix_g[i].reshape(1, D), ln_mix_b[i].reshape(1, D)
        if i % 2 == 0:
            if first:
                tail = x_prompt[:, SEQ - POOL_STATE:]
                x_s = x_samp2d
            else:
                tail = jnp.stack([x_in[TB + (b + 1) * SEQ - POOL_STATE:TB + (b + 1) * SEQ]
                                  for b in range(BATCH)])
                x_s = x_in[N_HEAD:TB]
            new_pool_p.append(tail)
            new_pool_s.append(jnp.concatenate([state_pool[j][:, 1:], x_s[:, None, :]], axis=1))
            ps = _state_presum(j, state_pool, pool_coef)
            h1, r_all, cnt, h1p, rt = _pool_layer(first, head, x_in, ps, w_pool[j].astype(jnp.bfloat16),
                                              pool_scale[j].reshape(1, D), ln_g, ln_b, wr)
        else:
            vs = _state_presum(j, state_conv, w_dw[j][:CONV_STATE])
            h1, r_all, cnt, h1p, rt, ust, u_s = _conv_layer(
                first, head, x_in, vs, w_glu[j].astype(jnp.bfloat16), b_glu[j].reshape(1, 2 * D), w_dw[j],
                b_dw[j].reshape(1, D), conv_ln_g[j].reshape(1, D), conv_ln_b[j].reshape(1, D),
                w_pw[j].astype(jnp.bfloat16), b_pw[j].reshape(1, D), ln_g, ln_b, wr)
            new_conv_p.append(ust)
            new_conv_s.append(jnp.concatenate([state_conv[j][:, 1:], u_s[:, None, :]], axis=1))
        block_expert, nused, base, limit, order, slot = _routing_metadata(rt, cnt)
        ys = _expert_layer(i, block_expert, nused, base, limit, order, h1p, w_gate, w_up, w_down)
        x_in = _combine_layer(i == DEPTH - 1, slot, h1, r_all, ys,
                              ln_ffn_g[i].reshape(1, D), ln_ffn_b[i].reshape(1, D))

    y_prompt, y_samp = x_in
    return (y_prompt.reshape(BATCH, SEQ, D), y_samp.reshape(DEC_BATCH, 1, D),
            jnp.stack(new_pool_p), jnp.stack(new_pool_s), jnp.stack(new_conv_p), jnp.stack(new_conv_s))
```

```python
import functools

import jax
import jax.numpy as jnp
from jax import lax
from jax.experimental import pallas as pl
from jax.experimental.pallas import tpu as pltpu

D = 1024
BATCH = 8
SEQ = 2048
DEPTH = 4
DEC_BATCH = 128
N_META = 16
POOL_WINDOWS = (2, 4, 8, 16)
POOL_GROUP_DIM = D // len(POOL_WINDOWS)
POOL_STATE = max(POOL_WINDOWS) - 1
CONV_WIDTH = 31
CONV_STATE = CONV_WIDTH - 1
N_GROUPS = 4
EXPERTS_PER_GROUP = 8
N_EXPERTS = N_GROUPS * EXPERTS_PER_GROUP
TOP_K = 2
D_EXPERT = D // 2
ALPHA = (2.0 * DEPTH) ** 0.25
LN_EPS = 1e-5

LANES = 128
SUBLANES = 8
N_HEAD = BATCH * N_META
TB = 256
NTOK = N_HEAD + DEC_BATCH + BATCH * SEQ
N_STEP = NTOK // TB
CHUNKS = SEQ // TB
POOL_CARRY = 16
CONV_CARRY = 32
CONV_RB = 128
CONV_CB = 128
BM = 512
N_ASSIGN = NTOK * TOP_K
NB = N_ASSIGN // BM + N_EXPERTS
N_SLOT = NB * BM
VMEM_LIMIT = 48 * 1024 * 1024
EXPERT_VMEM_LIMIT = 56 * 1024 * 1024
NEG = -1e30

assert D == SUBLANES * LANES and N_HEAD + DEC_BATCH == TB and SEQ % TB == 0
assert N_ASSIGN % BM == 0 and TB % CONV_RB == 0 and N_META == POOL_CARRY and CONV_STATE <= TB


def _tile_load(ref, n):
    return jnp.concatenate([ref[pl.ds(s, n, stride=SUBLANES), :] for s in range(SUBLANES)], axis=1)


def _tile_store(ref, val, n):
    for s in range(SUBLANES):
        ref[pl.ds(s, n, stride=SUBLANES), :] = val[:, s * LANES:(s + 1) * LANES]


def _layer_norm(x, g, b):
    mu = jnp.mean(x, axis=-1, keepdims=True)
    xc = x - mu
    var = jnp.mean(xc * xc, axis=-1, keepdims=True)
    return xc * lax.rsqrt(var + LN_EPS) * g + b


PACK_ROWS = D // 2 // LANES


def _pack_bf16(h_bf16, hp_ref, r0):
    n = h_bf16.shape[0]
    bits = pltpu.bitcast(h_bf16.astype(jnp.float32), jnp.uint32)
    words = jnp.bitwise_or(lax.shift_right_logical(bits[:, :D // 2], jnp.uint32(16)),
                           jnp.bitwise_and(bits[:, D // 2:], jnp.uint32(0xFFFF0000)))
    for s in range(PACK_ROWS):
        hp_ref[r0:r0 + n, s, :] = words[:, s * LANES:(s + 1) * LANES]


def _unpack_bf16(words):
    lo = pltpu.bitcast(lax.shift_left(words, jnp.uint32(16)), jnp.float32)
    hi = pltpu.bitcast(jnp.bitwise_and(words, jnp.uint32(0xFFFF0000)), jnp.float32)
    return jnp.concatenate([lo, hi], axis=1).astype(jnp.bfloat16)


HB = TB


def _route_init(cnt_ref, ltri_ref):
    @pl.when(pl.program_id(0) == 0)
    def _():
        cnt_ref[...] = jnp.zeros((1, LANES), jnp.float32)
        row = lax.broadcasted_iota(jnp.int32, (HB, HB), 0)
        col = lax.broadcasted_iota(jnp.int32, (HB, HB), 1)
        ltri_ref[...] = jnp.where(col < row, 1.0, 0.0).astype(jnp.bfloat16)


def _residual_route(x, m, r0, g_ref, b_ref, wr_ref, h_ref, r_ref, cnt_out_ref, hp_ref, rt_ref, cnt_ref,
                    ltri_ref):
    h = _layer_norm(ALPHA * x + m, g_ref[...], b_ref[...])
    h_ref[r0:r0 + HB, :] = h
    h_hi = h.astype(jnp.bfloat16)
    _pack_bf16(h_hi, hp_ref, r0)
    h_lo = (h - h_hi.astype(jnp.float32)).astype(jnp.bfloat16)
    hi_both = jnp.dot(h_hi, wr_ref[...], preferred_element_type=jnp.float32)
    lo_hi = jnp.dot(h_lo, wr_ref[:, 0:LANES], preferred_element_type=jnp.float32)
    logits = hi_both[:, 0:LANES] + (hi_both[:, LANES:2 * LANES] + lo_hi)
    lane = lax.broadcasted_iota(jnp.int32, logits.shape, 1)
    lanef = lane.astype(jnp.float32)
    big = jnp.float32(1e9)
    lg = jnp.where(lane < N_GROUPS, logits, NEG)
    mg = jnp.max(lg, axis=1, keepdims=True)
    gidx = jnp.min(jnp.where(lg == mg, lanef, big), axis=1, keepdims=True)
    p_grp = 1.0 / jnp.sum(jnp.where(lane < N_GROUPS, jnp.exp(lg - mg), 0.0), axis=1, keepdims=True)
    lo = N_GROUPS + EXPERTS_PER_GROUP * gidx
    le = jnp.where(lanef >= lo, jnp.where(lanef < lo + EXPERTS_PER_GROUP, logits, NEG), NEG)
    m1 = jnp.max(le, axis=1, keepdims=True)
    i1 = jnp.min(jnp.where(le == m1, lanef, big), axis=1, keepdims=True)
    le2 = jnp.where(lanef == i1, NEG, le)
    m2 = jnp.max(le2, axis=1, keepdims=True)
    i2 = jnp.min(jnp.where(le2 == m2, lanef, big), axis=1, keepdims=True)
    ratio = jnp.exp(m2 - m1)
    g1 = p_grp / (1.0 + ratio)
    g2 = g1 * ratio
    e1 = i1 - N_GROUPS
    e2 = i2 - N_GROUPS

    is1 = lanef == e1
    is2 = lanef == e2
    onehot = jnp.where(is1, 1.0, jnp.where(is2, 1.0, 0.0))
    before = jnp.dot(ltri_ref[...], onehot.astype(jnp.bfloat16),
                     preferred_element_type=jnp.float32) + cnt_ref[...]
    rank1 = jnp.sum(jnp.where(is1, before, 0.0), axis=1, keepdims=True)
    rank2 = jnp.sum(jnp.where(is2, before, 0.0), axis=1, keepdims=True)
    cnt = cnt_ref[...] + jnp.sum(onehot, axis=0, keepdims=True)
    cnt_ref[...] = cnt
    cnt_out_ref[...] = cnt

    packed = jnp.where(lane == 0, g1,
                       jnp.where(lane == 1, g2,
                                 jnp.where(lane == 2, e1,
                                           jnp.where(lane == 3, e2,
                                                     jnp.where(lane == 4, rank1,
                                                               jnp.where(lane == 5, rank2, 0.0))))))
    r_ref[r0:r0 + HB, :] = packed
    rt_ref[:, r0:r0 + HB] = jnp.transpose(packed)[0:SUBLANES, :]


def _seq_of_step(s):
    sm1 = jnp.maximum(s - 1, 0)
    return lax.shift_right_logical(sm1, CHUNKS.bit_length() - 1), jnp.bitwise_and(sm1, CHUNKS - 1)


assert CHUNKS & (CHUNKS - 1) == 0


PRESUM_ROWS = 32


def _presum_kernel(st_ref, coef_ref, o_ref):
    coef = coef_ref[...]
    for n in range(PRESUM_ROWS):
        o_ref[n:n + 1, :] = jnp.sum(st_ref[0, n] * coef, axis=0, keepdims=True)


def _state_presum(j, state, coef):
    _, n, k, _ = state.shape
    return pl.pallas_call(
        _presum_kernel,
        grid=(n // PRESUM_ROWS,),
        in_specs=[pl.BlockSpec((1, PRESUM_ROWS, k, D), lambda i: (j, i, 0, 0)),
                  pl.BlockSpec((k, D), lambda i: (0, 0))],
        out_specs=pl.BlockSpec((PRESUM_ROWS, D), lambda i: (i, 0)),
        out_shape=jax.ShapeDtypeStruct((n, D), jnp.float32),
        compiler_params=pltpu.CompilerParams(dimension_semantics=("arbitrary",),
                                             vmem_limit_bytes=VMEM_LIMIT),
        name="state_presum",
    )(state, coef)


def _pool_kernel(first_layer, head_ref, x_ref, ps_ref, wp_ref, sc_ref, g_ref, b_ref, wr_ref,
                 h_ref, r_ref, cnt_out_ref, hp_ref, rt_ref, ext_ref, m_ref, carry_ref, cnt_ref, ltri_ref,
                 start_next=lambda: None):
    s = pl.program_id(0)
    seq, chunk = _seq_of_step(s)
    _route_init(cnt_ref, ltri_ref)
    route_refs = (g_ref, b_ref, wr_ref, h_ref, r_ref, cnt_out_ref, hp_ref, rt_ref, cnt_ref, ltri_ref)

    @pl.when(s == 0)
    def _():
        x = head_ref[...] if first_layer else x_ref[...]
        start_next()
        ext_ref[0:POOL_CARRY, :] = jnp.zeros((POOL_CARRY, D), jnp.float32)
        ext_ref[POOL_CARRY:POOL_CARRY + TB, :] = x
        carry_ref[...] = x[0:N_HEAD, :]
        pos = jnp.bitwise_and(lax.broadcasted_iota(jnp.int32, (N_HEAD, 1), 0), N_META - 1)
        for g, w in enumerate(POOL_WINDOWS):
            lo, hi = g * POOL_GROUP_DIM, (g + 1) * POOL_GROUP_DIM
            xm = x[0:N_HEAD, lo:hi]
            acc = xm
            for k in range(1, w):
                acc = acc + jnp.where(pos >= k, ext_ref[POOL_CARRY - k:POOL_CARRY - k + N_HEAD, lo:hi], 0.0)
            cnt = jnp.minimum(pos + 1, w).astype(jnp.float32)
            d_meta = acc / cnt - xm
            xs = x[N_HEAD:TB, lo:hi]
            d_samp = (ps_ref[:, lo:hi] + xs) / float(w) - xs
            diff = jnp.concatenate([d_meta, d_samp], axis=0)
            y = jnp.dot(diff.astype(jnp.bfloat16), wp_ref[g], preferred_element_type=jnp.float32)
            m_ref[:, lo:hi] = y * sc_ref[:, lo:hi]
        for r0 in range(0, TB, HB):
            _residual_route(x[r0:r0 + HB], m_ref[r0:r0 + HB, :], r0, *route_refs)

    @pl.when(s > 0)
    def _():
        @pl.when(chunk == 0)
        def _():
            ext_ref[0:POOL_CARRY, :] = carry_ref[pl.ds(pl.multiple_of(seq * N_META, N_META), N_META), :]

        for r0 in range(0, TB, HB):
            x = x_ref[r0:r0 + HB, :]
            start_next()
            ext_ref[POOL_CARRY + r0:POOL_CARRY + r0 + HB, :] = x
            ms = []
            for g, w in enumerate(POOL_WINDOWS):
                lo, hi = g * POOL_GROUP_DIM, (g + 1) * POOL_GROUP_DIM
                xg = x[:, lo:hi]
                acc = xg
                for k in range(1, w):
                    acc = acc + ext_ref[POOL_CARRY + r0 - k:POOL_CARRY + r0 - k + HB, lo:hi]
                diff = acc / float(w) - xg
                y = jnp.dot(diff.astype(jnp.bfloat16), wp_ref[g], preferred_element_type=jnp.float32)
                ms.append(y * sc_ref[:, lo:hi])
            _residual_route(x, jnp.concatenate(ms, axis=1), r0, *route_refs)
        ext_ref[0:POOL_CARRY, :] = ext_ref[TB:TB + POOL_CARRY, :]


assert N_META >= max(POOL_WINDOWS)


def _const_spec(shape):
    nd = len(shape)
    return pl.BlockSpec(shape, lambda s, *_: (0,) * nd)


def _x_spec(first_layer):
    if first_layer:
        return pl.BlockSpec((TB, D), lambda s, *_: (jnp.maximum(s - 1, 0), 0))
    return pl.BlockSpec((TB, D), lambda s, *_: (s, 0))


_MIXER_OUT_SPECS = [pl.BlockSpec((TB, D), lambda s, *_: (s, 0)),
                    pl.BlockSpec((TB, LANES), lambda s, *_: (s, 0)),
                    pl.BlockSpec((1, LANES), lambda s, *_: (0, 0)),
                    pl.BlockSpec((TB, PACK_ROWS, LANES), lambda s, *_: (s, 0, 0)),
                    pl.BlockSpec((SUBLANES, TB), lambda s, *_: (s, 0))]
_MIXER_OUT_SHAPES = [jax.ShapeDtypeStruct((NTOK, D), jnp.float32),
                     jax.ShapeDtypeStruct((NTOK, LANES), jnp.float32),
                     jax.ShapeDtypeStruct((1, LANES), jnp.float32),
                     jax.ShapeDtypeStruct((NTOK, PACK_ROWS, LANES), jnp.uint32),
                     jax.ShapeDtypeStruct((N_STEP * SUBLANES, TB), jnp.float32)]


def _fused_mixer_kernel(body, n_in, slot_ref, hprev_ref, rprev_ref, ys_hbm, fg_ref, fb_ref, *rest):
    mixer_in = rest[:n_in]
    xs_out_ref, tail_ref = rest[n_in:n_in + 2]
    x_scr, ybuf, sem = rest[-3:]
    i = pl.program_id(0)
    n = pl.num_programs(0)

    def start(step, buf):
        base = step * (TOP_K * TB)
        _tile_gather_start(ys_hbm, ybuf.at[buf], sem.at[buf], lambda r: slot_ref[base + r], TOP_K * TB,
                           dst_tile=lambda r: (r % TOP_K) * TB + r // TOP_K)

    @pl.when(i == 0)
    def _():
        start(0, 0)

    buf = i % 2
    _tile_gather_wait(ys_hbm, ybuf.at[buf], sem.at[buf], TOP_K * TB)
    yb = ybuf.at[buf]
    y0 = _tile_load(yb.at[pl.ds(0, TB * SUBLANES), :], TB)
    y1 = _tile_load(yb.at[pl.ds(TB * SUBLANES, TB * SUBLANES), :], TB)
    r = rprev_ref[...]
    x = _layer_norm(ALPHA * hprev_ref[...] + (r[:, 0:1] * y0 + r[:, 1:2] * y1), fg_ref[...], fb_ref[...])
    x_scr[...] = x

    @pl.when(i == 0)
    def _():
        xs_out_ref[...] = x[N_HEAD:TB, :]

    _, chunk = _seq_of_step(i)

    @pl.when(jnp.logical_and(i > 0, chunk == CHUNKS - 1))
    def _():
        tail_ref[0] = x[TB - POOL_CARRY:TB, :]

    body(False, None, x_scr, *mixer_in, *rest[n_in + 2:-3],
         start_next=lambda: start(jnp.minimum(i + 1, n - 1), 1 - buf))

    @pl.when(i == n - 1)
    def _():
        _tile_gather_wait(ys_hbm, ybuf.at[1 - buf], sem.at[1 - buf], TOP_K * TB)


def _mixer_call(name, body, mixer_in_specs, extra_out_specs, extra_out_shapes, scratch_shapes, mixer_inputs,
                first_inputs=None, fused_inputs=None):
    out_specs = _MIXER_OUT_SPECS + extra_out_specs
    out_shape = _MIXER_OUT_SHAPES + extra_out_shapes
    params = pltpu.CompilerParams(dimension_semantics=("arbitrary",), vmem_limit_bytes=VMEM_LIMIT)
    if fused_inputs is None:
        return pl.pallas_call(
            functools.partial(body, True),
            grid=(N_STEP,),
            in_specs=[_const_spec((TB, D)), _x_spec(True)] + mixer_in_specs,
            out_specs=out_specs, out_shape=out_shape, scratch_shapes=scratch_shapes,
            compiler_params=params, name=name,
        )(*first_inputs, *mixer_inputs)
    seq_block = lambda s, *_: (jnp.maximum(s - 1, 0) // CHUNKS, 0, 0)
    grid_spec = pltpu.PrefetchScalarGridSpec(
        num_scalar_prefetch=1,
        grid=(N_STEP,),
        in_specs=[pl.BlockSpec((TB, D), lambda s, *_: (s, 0)),
                  pl.BlockSpec((TB, LANES), lambda s, *_: (s, 0)),
                  pl.BlockSpec(memory_space=pl.ANY),
                  _const_spec((1, D)), _const_spec((1, D))] + mixer_in_specs,
        out_specs=[_const_spec((DEC_BATCH, D)), pl.BlockSpec((1, POOL_CARRY, D), seq_block)] + out_specs,
        scratch_shapes=scratch_shapes + [pltpu.VMEM((TB, D), jnp.float32),
                                         pltpu.VMEM((2, TOP_K * TB * SUBLANES, LANES), jnp.float32),
                                         pltpu.SemaphoreType.DMA((2,))],
    )
    return pl.pallas_call(
        functools.partial(_fused_mixer_kernel, body, len(mixer_in_specs)),
        grid_spec=grid_spec,
        out_shape=[jax.ShapeDtypeStruct((DEC_BATCH, D), jnp.float32),
                   jax.ShapeDtypeStruct((BATCH, POOL_CARRY, D), jnp.float32)] + out_shape,
        compiler_params=params, name=name + "_fused",
    )(*fused_inputs, *mixer_inputs)


def _pool_layer(ps, wp_bf, scale, ln_g, ln_b, wr, **inputs):
    return _mixer_call(
        "pool_mixer", _pool_kernel,
        [_const_spec((DEC_BATCH, D)),
         _const_spec((len(POOL_WINDOWS), POOL_GROUP_DIM, POOL_GROUP_DIM)),
         _const_spec((1, D)), _const_spec((1, D)), _const_spec((1, D)),
         _const_spec((D, 2 * LANES))],
        [], [],
        [pltpu.VMEM((POOL_CARRY + TB, D), jnp.float32),
         pltpu.VMEM((TB, D), jnp.float32),
         pltpu.VMEM((N_HEAD, D), jnp.float32),
         pltpu.VMEM((1, LANES), jnp.float32),
         pltpu.VMEM((HB, HB), jnp.bfloat16)],
        (ps, wp_bf, scale, ln_g, ln_b, wr), **inputs)


def _depthwise_conv(ext_ref, wdw_ref, v_ref, r0):
    base = CONV_CARRY - CONV_STATE
    for cb in range(D // CONV_CB):
        lo, hi = cb * CONV_CB, (cb + 1) * CONV_CB
        v = None
        for r in range(SUBLANES):
            qs = [q for q in range((base + CONV_WIDTH) // SUBLANES + 1)
                  if 0 <= SUBLANES * q + r - base < CONV_WIDTH]
            z0 = r0 + SUBLANES * qs[0] + r
            z = ext_ref[z0:r0 + SUBLANES * qs[-1] + r + CONV_RB, lo:hi]
            p = None
            for q in qs:
                k = SUBLANES * q + r - base
                off = SUBLANES * (q - qs[0])
                term = wdw_ref[k:k + 1, lo:hi] * z[off:off + CONV_RB]
                p = term if p is None else p + term
            v = p if v is None else v + p
        v_ref[r0:r0 + CONV_RB, lo:hi] = v


assert HB % CONV_RB == 0


def _conv_kernel(first_layer, head_ref, x_ref, vs_ref, wglu_ref, bglu_ref, wdw_ref, bdw_ref, lg_ref, lb_ref,
                 wpw_ref, bpw_ref, g_ref, b_ref, wr_ref, h_ref, r_ref, cnt_out_ref, hp_ref, rt_ref, ust_ref, us_ref,
                 ext_ref, v_ref, carry_ref, cnt_ref, ltri_ref, start_next=lambda: None):
    s = pl.program_id(0)
    seq, chunk = _seq_of_step(s)
    _route_init(cnt_ref, ltri_ref)
    route_refs = (g_ref, b_ref, wr_ref, h_ref, r_ref, cnt_out_ref, hp_ref, rt_ref, cnt_ref, ltri_ref)

    def glu(x):
        hh = jnp.dot(x.astype(jnp.bfloat16), wglu_ref[...], preferred_element_type=jnp.float32) + bglu_ref[...]
        return hh[:, :D] * jax.nn.sigmoid(hh[:, D:])

    def finish(x, v, r0):
        v = _layer_norm(v + bdw_ref[...], lg_ref[...], lb_ref[...])
        v = v * jax.nn.sigmoid(v)
        m = jnp.dot(v.astype(jnp.bfloat16), wpw_ref[...], preferred_element_type=jnp.float32) + bpw_ref[...]
        _residual_route(x, m, r0, *route_refs)

    @pl.when(s == 0)
    def _():
        x = head_ref[...] if first_layer else x_ref[...]
        start_next()
        u = glu(x)
        um = u[0:N_HEAD, :]
        us = u[N_HEAD:TB, :]
        carry_ref[...] = um
        us_ref[...] = us
        ext_ref[0:N_META, :] = jnp.zeros((N_META, D), jnp.float32)
        ext_ref[N_META:N_META + N_HEAD, :] = um
        pos = jnp.bitwise_and(lax.broadcasted_iota(jnp.int32, (N_HEAD, 1), 0), N_META - 1)
        acc = wdw_ref[CONV_WIDTH - 1:CONV_WIDTH, :] * um
        for d in range(1, N_META):
            k = CONV_WIDTH - 1 - d
            acc = acc + wdw_ref[k:k + 1, :] * jnp.where(pos >= d, ext_ref[N_META - d:N_META - d + N_HEAD, :], 0.0)
        v_samp = vs_ref[...] + wdw_ref[CONV_WIDTH - 1:CONV_WIDTH, :] * us
        finish(x, jnp.concatenate([acc, v_samp], axis=0), 0)

    @pl.when(s > 0)
    def _():
        @pl.when(chunk == 0)
        def _():
            ext_ref[0:CONV_CARRY - N_META, :] = jnp.zeros((CONV_CARRY - N_META, D), jnp.float32)
            ext_ref[CONV_CARRY - N_META:CONV_CARRY, :] = carry_ref[
                pl.ds(pl.multiple_of(seq * N_META, N_META), N_META), :]

        for r0 in range(0, TB, HB):
            x = x_ref[r0:r0 + HB, :]
            start_next()
            ext_ref[CONV_CARRY + r0:CONV_CARRY + r0 + HB, :] = glu(x)
            for rb in range(r0, r0 + HB, CONV_RB):
                _depthwise_conv(ext_ref, wdw_ref, v_ref, rb)
            finish(x, v_ref[r0:r0 + HB, :], r0)
        ext_ref[0:CONV_CARRY, :] = ext_ref[TB:TB + CONV_CARRY, :]

        @pl.when(chunk == CHUNKS - 1)
        def _():
            ust_ref[0] = ext_ref[CONV_CARRY + TB - CONV_STATE:CONV_CARRY + TB, :]


assert CONV_CARRY - N_META + N_META >= CONV_STATE and N_META <= CONV_STATE


def _conv_layer(vs, wglu_bf, bglu, wdw, bdw, cln_g, cln_b, wpw_bf, bpw, ln_g, ln_b, wr, **inputs):
    return _mixer_call(
        "conv_mixer", _conv_kernel,
        [_const_spec((DEC_BATCH, D)),
         _const_spec((D, 2 * D)), _const_spec((1, 2 * D)),
         _const_spec((CONV_WIDTH, D)), _const_spec((1, D)),
         _const_spec((1, D)), _const_spec((1, D)),
         _const_spec((D, D)), _const_spec((1, D)),
         _const_spec((1, D)), _const_spec((1, D)),
         _const_spec((D, 2 * LANES))],
        [pl.BlockSpec((1, CONV_STATE, D), lambda s, *_: (jnp.maximum(s - 1, 0) // CHUNKS, 0, 0)),
         _const_spec((DEC_BATCH, D))],
        [jax.ShapeDtypeStruct((BATCH, CONV_STATE, D), jnp.float32),
         jax.ShapeDtypeStruct((DEC_BATCH, D), jnp.float32)],
        [pltpu.VMEM((CONV_CARRY + TB, D), jnp.float32),
         pltpu.VMEM((TB, D), jnp.float32),
         pltpu.VMEM((N_HEAD, D), jnp.float32),
         pltpu.VMEM((1, LANES), jnp.float32),
         pltpu.VMEM((HB, HB), jnp.bfloat16)],
        (vs, wglu_bf, bglu, wdw, bdw, cln_g, cln_b, wpw_bf, bpw, ln_g, ln_b, wr), **inputs)


def _tile_gather_start(src_hbm, dst, sem, row_index, n_rows, dst_tile=lambda r: r):
    for r in range(n_rows):
        tok = row_index(r)
        pltpu.make_async_copy(src_hbm.at[pl.ds(pl.multiple_of(tok * SUBLANES, SUBLANES), SUBLANES), :],
                              dst.at[pl.ds(dst_tile(r) * SUBLANES, SUBLANES), :], sem).start(priority=r % 2)


def _tile_gather_wait(src_hbm, dst, sem, n_rows):
    pltpu.make_async_copy(src_hbm.at[pl.ds(0, n_rows * SUBLANES), :], dst, sem).wait()


assert TOP_K == 2


def _expert_kernel(be_ref, nused_ref, base_ref, limit_ref, order_ref, hp_ref, wg_ref, wu_ref, wd_ref, ys_ref,
                   xbuf_even, xbuf_odd, wgb, wub, wdb):
    i = pl.program_id(0)
    nused = nused_ref[0]

    def gather(blk, xbuf):
        b0 = base_ref[blk]
        lim = limit_ref[blk]
        for r in range(BM):
            tok = lax.shift_right_logical(order_ref[jnp.minimum(b0 + r, lim)], 1)
            xbuf[r] = hp_ref[tok]

    def block(cur, nxt):
        gather(jnp.minimum(i + 1, jnp.maximum(nused - 1, 0)), nxt)
        words = jnp.concatenate([cur[:, s, :] for s in range(PACK_ROWS)], axis=1)
        xb = _unpack_bf16(words)
        gate = jnp.dot(xb, wgb[...], preferred_element_type=jnp.float32)
        up = jnp.dot(xb, wub[...], preferred_element_type=jnp.float32)
        hid = gate * jax.nn.sigmoid(gate) * up
        y = jnp.dot(hid.astype(jnp.bfloat16), wdb[...], preferred_element_type=jnp.float32)
        _tile_store(ys_ref, y, BM)

    @pl.when(i == 0)
    def _():
        gather(0, xbuf_even)

    @pl.when(jnp.logical_and(i < nused, jnp.logical_or(i == 0, be_ref[i] != be_ref[jnp.maximum(i - 1, 0)])))
    def _():
        wgb[...] = wg_ref[0, 0].astype(jnp.bfloat16)
        wub[...] = wu_ref[0, 0].astype(jnp.bfloat16)
        wdb[...] = wd_ref[0, 0].astype(jnp.bfloat16)

    odd = jnp.bitwise_and(i, 1) == 1

    @pl.when(jnp.logical_and(i < nused, jnp.logical_not(odd)))
    def _():
        block(xbuf_even, xbuf_odd)

    @pl.when(jnp.logical_and(i < nused, odd))
    def _():
        block(xbuf_odd, xbuf_even)

    @pl.when(i >= nused)
    def _():
        ys_ref[...] = jnp.zeros((BM * SUBLANES, LANES), jnp.float32)


def _expert_layer(layer, block_expert, nused, base, limit, order, h_packed, w_gate, w_up, w_down):
    def w_spec(shape):
        return pl.BlockSpec((1, 1) + shape, lambda i, be, *_: (layer, be[i], 0, 0))
    grid_spec = pltpu.PrefetchScalarGridSpec(
        num_scalar_prefetch=5,
        grid=(NB,),
        in_specs=[pl.BlockSpec((NTOK, PACK_ROWS, LANES), lambda i, *_: (0, 0, 0)),
                  w_spec((D, D_EXPERT)), w_spec((D, D_EXPERT)), w_spec((D_EXPERT, D))],
        out_specs=pl.BlockSpec((BM * SUBLANES, LANES), lambda i, *_: (i, 0)),
        scratch_shapes=[pltpu.VMEM((BM, PACK_ROWS, LANES), jnp.uint32),
                        pltpu.VMEM((BM, PACK_ROWS, LANES), jnp.uint32),
                        pltpu.VMEM((D, D_EXPERT), jnp.bfloat16),
                        pltpu.VMEM((D, D_EXPERT), jnp.bfloat16),
                        pltpu.VMEM((D_EXPERT, D), jnp.bfloat16)],
    )
    return pl.pallas_call(
        _expert_kernel,
        grid_spec=grid_spec,
        out_shape=jax.ShapeDtypeStruct((N_SLOT * SUBLANES, LANES), jnp.float32),
        compiler_params=pltpu.CompilerParams(dimension_semantics=("arbitrary",),
                                             vmem_limit_bytes=EXPERT_VMEM_LIMIT),
        name="expert_mlp",
    )(block_expert, nused, base, limit, order, h_packed, w_gate, w_up, w_down)


def _combine_kernel(last_layer, slot_ref, h_ref, r_ref, ys_hbm, g_ref, b_ref, *rest):
    if last_layer:
        yp_ref, ysamp_ref, ybuf, sem = rest
    else:
        o_ref, ybuf, sem = rest
    i = pl.program_id(0)
    n = pl.num_programs(0)

    def start(step, buf):
        base = step * (TOP_K * TB)
        _tile_gather_start(ys_hbm, ybuf.at[buf], sem.at[buf], lambda r: slot_ref[base + r], TOP_K * TB,
                           dst_tile=lambda r: (r % TOP_K) * TB + r // TOP_K)

    @pl.when(i == 0)
    def _():
        start(0, 0)

    buf = i % 2
    _tile_gather_wait(ys_hbm, ybuf.at[buf], sem.at[buf], TOP_K * TB)
    yb = ybuf.at[buf]
    y0 = _tile_load(yb.at[pl.ds(0, TB * SUBLANES), :], TB)
    y1 = _tile_load(yb.at[pl.ds(TB * SUBLANES, TB * SUBLANES), :], TB)
    h = h_ref[...]
    r = r_ref[...]

    start(jnp.minimum(i + 1, n - 1), 1 - buf)

    f = r[:, 0:1] * y0 + r[:, 1:2] * y1
    out = _layer_norm(ALPHA * h + f, g_ref[...], b_ref[...])
    if last_layer:
        yp_ref[...] = out

        @pl.when(i == 0)
        def _():
            ysamp_ref[...] = out[N_HEAD:TB, :]
    else:
        o_ref[...] = out

    @pl.when(i == n - 1)
    def _():
        _tile_gather_wait(ys_hbm, ybuf.at[1 - buf], sem.at[1 - buf], TOP_K * TB)


def _combine_layer(last_layer, slot, h_tiles, r_all, ys, ln_g, ln_b):
    if last_layer:
        out_specs = [pl.BlockSpec((TB, D), lambda i, sl: (jnp.maximum(i - 1, 0), 0)),
                     pl.BlockSpec((DEC_BATCH, D), lambda i, sl: (0, 0))]
        out_shape = [jax.ShapeDtypeStruct((BATCH * SEQ, D), jnp.float32),
                     jax.ShapeDtypeStruct((DEC_BATCH, D), jnp.float32)]
    else:
        out_specs = pl.BlockSpec((TB, D), lambda i, sl: (i, 0))
        out_shape = jax.ShapeDtypeStruct((NTOK, D), jnp.float32)
    grid_spec = pltpu.PrefetchScalarGridSpec(
        num_scalar_prefetch=1,
        grid=(N_STEP,),
        in_specs=[pl.BlockSpec((TB, D), lambda i, sl: (i, 0)),
                  pl.BlockSpec((TB, LANES), lambda i, sl: (i, 0)),
                  pl.BlockSpec(memory_space=pl.ANY),
                  pl.BlockSpec((1, D), lambda i, sl: (0, 0)),
                  pl.BlockSpec((1, D), lambda i, sl: (0, 0))],
        out_specs=out_specs,
        scratch_shapes=[pltpu.VMEM((2, TOP_K * TB * SUBLANES, LANES), jnp.float32),
                        pltpu.SemaphoreType.DMA((2,))],
    )
    return pl.pallas_call(
        functools.partial(_combine_kernel, last_layer),
        grid_spec=grid_spec,
        out_shape=out_shape,
        compiler_params=pltpu.CompilerParams(dimension_semantics=("arbitrary",),
                                             vmem_limit_bytes=VMEM_LIMIT),
        name="moe_combine",
    )(slot, h_tiles, r_all, ys, ln_g, ln_b)


def _routing_metadata(rt, cnt):
    experts = jnp.arange(N_EXPERTS, dtype=jnp.int32)
    counts = cnt[0, :N_EXPERTS].astype(jnp.int32)
    nblk = (counts + BM - 1) // BM
    blk_end = jnp.cumsum(nblk)
    pstart = (blk_end - nblk) * BM
    starts = jnp.cumsum(counts) - counts
    nused = blk_end[-1]
    blk = jnp.arange(NB, dtype=jnp.int32)
    block_expert = jnp.sum((blk[:, None] >= blk_end[None, :]).astype(jnp.int32), axis=1)
    last_expert = jnp.max(jnp.where(nblk > 0, experts, 0))
    block_expert = jnp.where(blk < nused, block_expert, last_expert).astype(jnp.int32)
    of_block = block_expert[:, None] == experts[None, :]
    base = blk * BM + jnp.sum(jnp.where(of_block, (starts - pstart)[None, :], 0), axis=1)
    limit = jnp.sum(jnp.where(of_block, (starts + counts - 1)[None, :], 0), axis=1)
    rows = rt.reshape(N_STEP, SUBLANES, TB)
    expert = jnp.transpose(rows[:, 2:2 + TOP_K, :], (0, 2, 1)).reshape(N_ASSIGN).astype(jnp.int32)
    rank = jnp.transpose(rows[:, 2 + TOP_K:2 + 2 * TOP_K, :], (0, 2, 1)).reshape(N_ASSIGN).astype(jnp.int32)
    slot = rank + jnp.sum(jnp.where(expert[:, None] == experts[None, :], pstart[None, :], 0), axis=1)
    order = jnp.argsort(expert, stable=True).astype(jnp.int32)
    return (block_expert, nused.reshape(1).astype(jnp.int32), base.astype(jnp.int32),
            limit.astype(jnp.int32), order, slot.astype(jnp.int32))


def kernel(x_prompt, x_sample, state_pool, state_conv, meta, w_pool, pool_scale, w_glu, b_glu, w_dw, b_dw, conv_ln_g, conv_ln_b, w_pw, b_pw, ln_mix_g, ln_mix_b, ln_ffn_g, ln_ffn_b, w_router_group, w_router_expert, w_gate, w_up, w_down):
    f32 = jnp.float32
    x_samp2d = x_sample.reshape(DEC_BATCH, D)
    head = jnp.concatenate([jnp.tile(meta.astype(f32), (BATCH, 1)), x_samp2d], axis=0)
    k_idx = jnp.arange(POOL_STATE)[:, None]
    win = jnp.repeat(jnp.asarray(POOL_WINDOWS), POOL_GROUP_DIM)[None, :]
    pool_coef = (k_idx >= (POOL_STATE + 1 - win)).astype(f32)

    new_pool_p, new_pool_s, new_conv_p, new_conv_s = [], [], [], []
    for i in range(DEPTH):
        j = i // 2
        wr32 = jnp.concatenate([w_router_group[i], w_router_expert[i],
                                jnp.zeros((D, LANES - N_GROUPS - N_EXPERTS), f32)], axis=1)
        wr_hi = wr32.astype(jnp.bfloat16)
        wr = jnp.concatenate([wr_hi, (wr32 - wr_hi.astype(f32)).astype(jnp.bfloat16)], axis=1)
        ln_g, ln_b = ln_mix_g[i].reshape(1, D), ln_mix_b[i].reshape(1, D)
        if i == 0:
            inputs = dict(first_inputs=(head, x_prompt.reshape(BATCH * SEQ, D)))
        else:
            inputs = dict(fused_inputs=(slot, h1, r_all, ys, ln_ffn_g[i - 1].reshape(1, D),
                                        ln_ffn_b[i - 1].reshape(1, D)))
        if i % 2 == 0:
            ps = _state_presum(j, state_pool, pool_coef)
            outs = _pool_layer(ps, w_pool[j].astype(jnp.bfloat16), pool_scale[j].reshape(1, D),
                               ln_g, ln_b, wr, **inputs)
            if i == 0:
                tail, x_s = x_prompt[:, SEQ - POOL_STATE:], x_samp2d
            else:
                x_s, tail16, outs = outs[0], outs[1], outs[2:]
                tail = tail16[:, POOL_CARRY - POOL_STATE:]
            h1, r_all, cnt, h1p, rt = outs
            new_pool_p.append(tail)
            new_pool_s.append(jnp.concatenate([state_pool[j][:, 1:], x_s[:, None, :]], axis=1))
        else:
            vs = _state_presum(j, state_conv, w_dw[j][:CONV_STATE])
            outs = _conv_layer(
                vs, w_glu[j].astype(jnp.bfloat16), b_glu[j].reshape(1, 2 * D), w_dw[j],
                b_dw[j].reshape(1, D), conv_ln_g[j].reshape(1, D), conv_ln_b[j].reshape(1, D),
                w_pw[j].astype(jnp.bfloat16), b_pw[j].reshape(1, D), ln_g, ln_b, wr, **inputs)
            h1, r_all, cnt, h1p, rt, ust, u_s = outs if i == 0 else outs[2:]
            new_conv_p.append(ust)
            new_conv_s.append(jnp.concatenate([state_conv[j][:, 1:], u_s[:, None, :]], axis=1))
        block_expert, nused, base, limit, order, slot = _routing_metadata(rt, cnt)
        ys = _expert_layer(i, block_expert, nused, base, limit, order, h1p, w_gate, w_up, w_down)

    y_prompt, y_samp = _combine_layer(True, slot, h1, r_all, ys, ln_ffn_g[DEPTH - 1].reshape(1, D),
                                      ln_ffn_b[DEPTH - 1].reshape(1, D))
    return (y_prompt.reshape(BATCH, SEQ, D), y_samp.reshape(DEC_BATCH, 1, D),
            jnp.stack(new_pool_p), jnp.stack(new_pool_s), jnp.stack(new_conv_p), jnp.stack(new_conv_s))
```

```python
import functools

import jax
import jax.numpy as jnp
from jax import lax
from jax.experimental import pallas as pl
from jax.experimental.pallas import tpu as pltpu

D = 1024
BATCH = 8
SEQ = 2048
DEPTH = 4
DEC_BATCH = 128
N_META = 16
POOL_WINDOWS = (2, 4, 8, 16)
POOL_GROUP_DIM = D // len(POOL_WINDOWS)
POOL_STATE = max(POOL_WINDOWS) - 1
CONV_WIDTH = 31
CONV_STATE = CONV_WIDTH - 1
N_GROUPS = 4
EXPERTS_PER_GROUP = 8
N_EXPERTS = N_GROUPS * EXPERTS_PER_GROUP
TOP_K = 2
D_EXPERT = D // 2
ALPHA = (2.0 * DEPTH) ** 0.25
LN_EPS = 1e-5

LANES = 128
SUBLANES = 8
N_HEAD = BATCH * N_META
TB = 256
NTOK = N_HEAD + DEC_BATCH + BATCH * SEQ
N_STEP = NTOK // TB
CHUNKS = SEQ // TB
POOL_CARRY = 16
CONV_CARRY = 32
CONV_RB = 128
CONV_CB = 128
BM = 512
N_ASSIGN = NTOK * TOP_K
RANK_RANGE = 1 << 16
NB = N_ASSIGN // BM + N_EXPERTS
N_SLOT = NB * BM
VMEM_LIMIT = 48 * 1024 * 1024
EXPERT_VMEM_LIMIT = 56 * 1024 * 1024
NEG = -1e30

assert D == SUBLANES * LANES and N_HEAD + DEC_BATCH == TB and SEQ % TB == 0
assert N_ASSIGN % BM == 0 and TB % CONV_RB == 0 and N_META == POOL_CARRY and CONV_STATE <= TB


def _tile_load(ref, n):
    return jnp.concatenate([ref[pl.ds(s, n, stride=SUBLANES), :] for s in range(SUBLANES)], axis=1)


def _tile_store(ref, val, n):
    for s in range(SUBLANES):
        ref[pl.ds(s, n, stride=SUBLANES), :] = val[:, s * LANES:(s + 1) * LANES]


def _layer_norm(x, g, b):
    mu = jnp.mean(x, axis=-1, keepdims=True)
    xc = x - mu
    var = jnp.mean(xc * xc, axis=-1, keepdims=True)
    return xc * lax.rsqrt(var + LN_EPS) * g + b


PACK_ROWS = D // 2 // LANES


def _pack_bf16(h_bf16, hp_ref, r0):
    n = h_bf16.shape[0]
    bits = pltpu.bitcast(h_bf16.astype(jnp.float32), jnp.uint32)
    words = jnp.bitwise_or(lax.shift_right_logical(bits[:, :D // 2], jnp.uint32(16)),
                           jnp.bitwise_and(bits[:, D // 2:], jnp.uint32(0xFFFF0000)))
    for s in range(PACK_ROWS):
        hp_ref[r0:r0 + n, s, :] = words[:, s * LANES:(s + 1) * LANES]


def _unpack_bf16(words):
    lo = pltpu.bitcast(lax.shift_left(words, jnp.uint32(16)), jnp.float32)
    hi = pltpu.bitcast(jnp.bitwise_and(words, jnp.uint32(0xFFFF0000)), jnp.float32)
    return jnp.concatenate([lo, hi], axis=1).astype(jnp.bfloat16)


HB = TB


def _route_init(cnt_ref, ltri_ref):
    @pl.when(pl.program_id(0) == 0)
    def _():
        cnt_ref[...] = jnp.zeros((1, LANES), jnp.float32)
        row = lax.broadcasted_iota(jnp.int32, (HB, HB), 0)
        col = lax.broadcasted_iota(jnp.int32, (HB, HB), 1)
        ltri_ref[...] = jnp.where(col < row, 1.0, 0.0).astype(jnp.bfloat16)


def _residual_route(x, m, r0, g_ref, b_ref, wr_ref, h_ref, r_ref, cnt_out_ref, hp_ref, rt_ref, cnt_ref,
                    ltri_ref):
    h = _layer_norm(ALPHA * x + m, g_ref[...], b_ref[...])
    h_ref[r0:r0 + HB, :] = h
    h_hi = h.astype(jnp.bfloat16)
    _pack_bf16(h_hi, hp_ref, r0)
    h_lo = (h - h_hi.astype(jnp.float32)).astype(jnp.bfloat16)
    hi_both = jnp.dot(h_hi, wr_ref[...], preferred_element_type=jnp.float32)
    lo_hi = jnp.dot(h_lo, wr_ref[:, 0:LANES], preferred_element_type=jnp.float32)
    logits = hi_both[:, 0:LANES] + (hi_both[:, LANES:2 * LANES] + lo_hi)
    lane = lax.broadcasted_iota(jnp.int32, logits.shape, 1)
    lanef = lane.astype(jnp.float32)
    big = jnp.float32(1e9)
    lg = jnp.where(lane < N_GROUPS, logits, NEG)
    mg = jnp.max(lg, axis=1, keepdims=True)
    gidx = jnp.min(jnp.where(lg == mg, lanef, big), axis=1, keepdims=True)
    p_grp = 1.0 / jnp.sum(jnp.where(lane < N_GROUPS, jnp.exp(lg - mg), 0.0), axis=1, keepdims=True)
    lo = N_GROUPS + EXPERTS_PER_GROUP * gidx
    le = jnp.where(lanef >= lo, jnp.where(lanef < lo + EXPERTS_PER_GROUP, logits, NEG), NEG)
    m1 = jnp.max(le, axis=1, keepdims=True)
    i1 = jnp.min(jnp.where(le == m1, lanef, big), axis=1, keepdims=True)
    le2 = jnp.where(lanef == i1, NEG, le)
    m2 = jnp.max(le2, axis=1, keepdims=True)
    i2 = jnp.min(jnp.where(le2 == m2, lanef, big), axis=1, keepdims=True)
    ratio = jnp.exp(m2 - m1)
    g1 = p_grp / (1.0 + ratio)
    g2 = g1 * ratio
    e1 = i1 - N_GROUPS
    e2 = i2 - N_GROUPS

    is1 = lanef == e1
    is2 = lanef == e2
    onehot = jnp.where(is1, 1.0, jnp.where(is2, 1.0, 0.0))
    before = jnp.dot(ltri_ref[...], onehot.astype(jnp.bfloat16),
                     preferred_element_type=jnp.float32) + cnt_ref[...]
    rank1 = jnp.sum(jnp.where(is1, before, 0.0), axis=1, keepdims=True)
    rank2 = jnp.sum(jnp.where(is2, before, 0.0), axis=1, keepdims=True)
    cnt = cnt_ref[...] + jnp.sum(onehot, axis=0, keepdims=True)
    cnt_ref[...] = cnt
    cnt_out_ref[...] = cnt

    packed = jnp.where(lane == 0, g1,
                       jnp.where(lane == 1, g2,
                                 jnp.where(lane == 2, e1,
                                           jnp.where(lane == 3, e2,
                                                     jnp.where(lane == 4, rank1,
                                                               jnp.where(lane == 5, rank2, 0.0))))))
    r_ref[r0:r0 + HB, :] = packed
    rt_ref[:, r0:r0 + HB] = jnp.transpose(packed)[0:SUBLANES, :]


def _seq_of_step(s):
    sm1 = jnp.maximum(s - 1, 0)
    return lax.shift_right_logical(sm1, CHUNKS.bit_length() - 1), jnp.bitwise_and(sm1, CHUNKS - 1)


assert CHUNKS & (CHUNKS - 1) == 0


PRESUM_ROWS = 32


def _presum_kernel(st_ref, coef_ref, o_ref):
    coef = coef_ref[...]
    for n in range(PRESUM_ROWS):
        o_ref[n:n + 1, :] = jnp.sum(st_ref[0, n] * coef, axis=0, keepdims=True)


def _state_presum(j, state, coef):
    _, n, k, _ = state.shape
    return pl.pallas_call(
        _presum_kernel,
        grid=(n // PRESUM_ROWS,),
        in_specs=[pl.BlockSpec((1, PRESUM_ROWS, k, D), lambda i: (j, i, 0, 0)),
                  pl.BlockSpec((k, D), lambda i: (0, 0))],
        out_specs=pl.BlockSpec((PRESUM_ROWS, D), lambda i: (i, 0)),
        out_shape=jax.ShapeDtypeStruct((n, D), jnp.float32),
        compiler_params=pltpu.CompilerParams(dimension_semantics=("arbitrary",),
                                             vmem_limit_bytes=VMEM_LIMIT),
        name="state_presum",
    )(state, coef)


def _pool_kernel(first_layer, head_ref, x_ref, ps_ref, wp_ref, sc_ref, g_ref, b_ref, wr_ref,
                 h_ref, r_ref, cnt_out_ref, hp_ref, rt_ref, ext_ref, m_ref, carry_ref, cnt_ref, ltri_ref,
                 start_next=lambda: None):
    s = pl.program_id(0)
    seq, chunk = _seq_of_step(s)
    _route_init(cnt_ref, ltri_ref)
    route_refs = (g_ref, b_ref, wr_ref, h_ref, r_ref, cnt_out_ref, hp_ref, rt_ref, cnt_ref, ltri_ref)

    @pl.when(s == 0)
    def _():
        x = head_ref[...] if first_layer else x_ref[...]
        start_next()
        ext_ref[0:POOL_CARRY, :] = jnp.zeros((POOL_CARRY, D), jnp.float32)
        ext_ref[POOL_CARRY:POOL_CARRY + TB, :] = x
        carry_ref[...] = x[0:N_HEAD, :]
        pos = jnp.bitwise_and(lax.broadcasted_iota(jnp.int32, (N_HEAD, 1), 0), N_META - 1)
        for g, w in enumerate(POOL_WINDOWS):
            lo, hi = g * POOL_GROUP_DIM, (g + 1) * POOL_GROUP_DIM
            xm = x[0:N_HEAD, lo:hi]
            acc = xm
            for k in range(1, w):
                acc = acc + jnp.where(pos >= k, ext_ref[POOL_CARRY - k:POOL_CARRY - k + N_HEAD, lo:hi], 0.0)
            cnt = jnp.minimum(pos + 1, w).astype(jnp.float32)
            d_meta = acc / cnt - xm
            xs = x[N_HEAD:TB, lo:hi]
            d_samp = (ps_ref[:, lo:hi] + xs) / float(w) - xs
            diff = jnp.concatenate([d_meta, d_samp], axis=0)
            y = jnp.dot(diff.astype(jnp.bfloat16), wp_ref[g], preferred_element_type=jnp.float32)
            m_ref[:, lo:hi] = y * sc_ref[:, lo:hi]
        for r0 in range(0, TB, HB):
            _residual_route(x[r0:r0 + HB], m_ref[r0:r0 + HB, :], r0, *route_refs)

    @pl.when(s > 0)
    def _():
        @pl.when(chunk == 0)
        def _():
            ext_ref[0:POOL_CARRY, :] = carry_ref[pl.ds(pl.multiple_of(seq * N_META, N_META), N_META), :]

        for r0 in range(0, TB, HB):
            x = x_ref[r0:r0 + HB, :]
            start_next()
            ext_ref[POOL_CARRY + r0:POOL_CARRY + r0 + HB, :] = x
            ms = []
            for g, w in enumerate(POOL_WINDOWS):
                lo, hi = g * POOL_GROUP_DIM, (g + 1) * POOL_GROUP_DIM
                xg = x[:, lo:hi]
                acc = xg
                for k in range(1, w):
                    acc = acc + ext_ref[POOL_CARRY + r0 - k:POOL_CARRY + r0 - k + HB, lo:hi]
                diff = acc / float(w) - xg
                y = jnp.dot(diff.astype(jnp.bfloat16), wp_ref[g], preferred_element_type=jnp.float32)
                ms.append(y * sc_ref[:, lo:hi])
            _residual_route(x, jnp.concatenate(ms, axis=1), r0, *route_refs)
        ext_ref[0:POOL_CARRY, :] = ext_ref[TB:TB + POOL_CARRY, :]


assert N_META >= max(POOL_WINDOWS)


def _const_spec(shape):
    nd = len(shape)
    return pl.BlockSpec(shape, lambda s, *_: (0,) * nd)


def _x_spec(first_layer):
    if first_layer:
        return pl.BlockSpec((TB, D), lambda s, *_: (jnp.maximum(s - 1, 0), 0))
    return pl.BlockSpec((TB, D), lambda s, *_: (s, 0))


_MIXER_OUT_SPECS = [pl.BlockSpec((TB, D), lambda s, *_: (s, 0)),
                    pl.BlockSpec((TB, LANES), lambda s, *_: (s, 0)),
                    pl.BlockSpec((1, LANES), lambda s, *_: (0, 0)),
                    pl.BlockSpec((TB, PACK_ROWS, LANES), lambda s, *_: (s, 0, 0)),
                    pl.BlockSpec((SUBLANES, TB), lambda s, *_: (s, 0))]
_MIXER_OUT_SHAPES = [jax.ShapeDtypeStruct((NTOK, D), jnp.float32),
                     jax.ShapeDtypeStruct((NTOK, LANES), jnp.float32),
                     jax.ShapeDtypeStruct((1, LANES), jnp.float32),
                     jax.ShapeDtypeStruct((NTOK, PACK_ROWS, LANES), jnp.uint32),
                     jax.ShapeDtypeStruct((N_STEP * SUBLANES, TB), jnp.float32)]


def _fused_mixer_kernel(body, n_in, slot_ref, hprev_ref, rprev_ref, ys_hbm, fg_ref, fb_ref, *rest):
    mixer_in = rest[:n_in]
    xs_out_ref, tail_ref = rest[n_in:n_in + 2]
    x_scr, ybuf, sem = rest[-3:]
    i = pl.program_id(0)
    n = pl.num_programs(0)

    def start(step, buf):
        base = step * TB
        _tile_gather_start(ys_hbm, ybuf.at[buf], sem.at[buf],
                           lambda r: slot_ref[(r % TOP_K) * NTOK + base + r // TOP_K], TOP_K * TB,
                           dst_tile=lambda r: (r % TOP_K) * TB + r // TOP_K)

    @pl.when(i == 0)
    def _():
        start(0, 0)

    buf = i % 2
    _tile_gather_wait(ys_hbm, ybuf.at[buf], sem.at[buf], TOP_K * TB)
    yb = ybuf.at[buf]
    y0 = _tile_load(yb.at[pl.ds(0, TB * SUBLANES), :], TB)
    y1 = _tile_load(yb.at[pl.ds(TB * SUBLANES, TB * SUBLANES), :], TB)
    r = rprev_ref[...]
    x = _layer_norm(ALPHA * hprev_ref[...] + (r[:, 0:1] * y0 + r[:, 1:2] * y1), fg_ref[...], fb_ref[...])
    x_scr[...] = x

    @pl.when(i == 0)
    def _():
        xs_out_ref[...] = x[N_HEAD:TB, :]

    _, chunk = _seq_of_step(i)

    @pl.when(jnp.logical_and(i > 0, chunk == CHUNKS - 1))
    def _():
        tail_ref[0] = x[TB - POOL_CARRY:TB, :]

    body(False, None, x_scr, *mixer_in, *rest[n_in + 2:-3],
         start_next=lambda: start(jnp.minimum(i + 1, n - 1), 1 - buf))

    @pl.when(i == n - 1)
    def _():
        _tile_gather_wait(ys_hbm, ybuf.at[1 - buf], sem.at[1 - buf], TOP_K * TB)


def _mixer_call(name, body, mixer_in_specs, extra_out_specs, extra_out_shapes, scratch_shapes, mixer_inputs,
                first_inputs=None, fused_inputs=None):
    out_specs = _MIXER_OUT_SPECS + extra_out_specs
    out_shape = _MIXER_OUT_SHAPES + extra_out_shapes
    params = pltpu.CompilerParams(dimension_semantics=("arbitrary",), vmem_limit_bytes=VMEM_LIMIT)
    if fused_inputs is None:
        return pl.pallas_call(
            functools.partial(body, True),
            grid=(N_STEP,),
            in_specs=[_const_spec((TB, D)), _x_spec(True)] + mixer_in_specs,
            out_specs=out_specs, out_shape=out_shape, scratch_shapes=scratch_shapes,
            compiler_params=params, name=name,
        )(*first_inputs, *mixer_inputs)
    seq_block = lambda s, *_: (jnp.maximum(s - 1, 0) // CHUNKS, 0, 0)
    grid_spec = pltpu.PrefetchScalarGridSpec(
        num_scalar_prefetch=1,
        grid=(N_STEP,),
        in_specs=[pl.BlockSpec((TB, D), lambda s, *_: (s, 0)),
                  pl.BlockSpec((TB, LANES), lambda s, *_: (s, 0)),
                  pl.BlockSpec(memory_space=pl.ANY),
                  _const_spec((1, D)), _const_spec((1, D))] + mixer_in_specs,
        out_specs=[_const_spec((DEC_BATCH, D)), pl.BlockSpec((1, POOL_CARRY, D), seq_block)] + out_specs,
        scratch_shapes=scratch_shapes + [pltpu.VMEM((TB, D), jnp.float32),
                                         pltpu.VMEM((2, TOP_K * TB * SUBLANES, LANES), jnp.float32),
                                         pltpu.SemaphoreType.DMA((2,))],
    )
    return pl.pallas_call(
        functools.partial(_fused_mixer_kernel, body, len(mixer_in_specs)),
        grid_spec=grid_spec,
        out_shape=[jax.ShapeDtypeStruct((DEC_BATCH, D), jnp.float32),
                   jax.ShapeDtypeStruct((BATCH, POOL_CARRY, D), jnp.float32)] + out_shape,
        compiler_params=params, name=name + "_fused",
    )(*fused_inputs, *mixer_inputs)


def _pool_layer(ps, wp_bf, scale, ln_g, ln_b, wr, **inputs):
    return _mixer_call(
        "pool_mixer", _pool_kernel,
        [_const_spec((DEC_BATCH, D)),
         _const_spec((len(POOL_WINDOWS), POOL_GROUP_DIM, POOL_GROUP_DIM)),
         _const_spec((1, D)), _const_spec((1, D)), _const_spec((1, D)),
         _const_spec((D, 2 * LANES))],
        [], [],
        [pltpu.VMEM((POOL_CARRY + TB, D), jnp.float32),
         pltpu.VMEM((TB, D), jnp.float32),
         pltpu.VMEM((N_HEAD, D), jnp.float32),
         pltpu.VMEM((1, LANES), jnp.float32),
         pltpu.VMEM((HB, HB), jnp.bfloat16)],
        (ps, wp_bf, scale, ln_g, ln_b, wr), **inputs)


def _depthwise_conv(ext_ref, wdw_ref, v_ref, r0):
    base = CONV_CARRY - CONV_STATE
    for cb in range(D // CONV_CB):
        lo, hi = cb * CONV_CB, (cb + 1) * CONV_CB
        v = None
        for r in range(SUBLANES):
            qs = [q for q in range((base + CONV_WIDTH) // SUBLANES + 1)
                  if 0 <= SUBLANES * q + r - base < CONV_WIDTH]
            z0 = r0 + SUBLANES * qs[0] + r
            z = ext_ref[z0:r0 + SUBLANES * qs[-1] + r + CONV_RB, lo:hi]
            p = None
            for q in qs:
                k = SUBLANES * q + r - base
                off = SUBLANES * (q - qs[0])
                term = wdw_ref[k:k + 1, lo:hi] * z[off:off + CONV_RB]
                p = term if p is None else p + term
            v = p if v is None else v + p
        v_ref[r0:r0 + CONV_RB, lo:hi] = v


assert HB % CONV_RB == 0


def _conv_kernel(first_layer, head_ref, x_ref, vs_ref, wglu_ref, bglu_ref, wdw_ref, bdw_ref, lg_ref, lb_ref,
                 wpw_ref, bpw_ref, g_ref, b_ref, wr_ref, h_ref, r_ref, cnt_out_ref, hp_ref, rt_ref, ust_ref, us_ref,
                 ext_ref, v_ref, carry_ref, cnt_ref, ltri_ref, start_next=lambda: None):
    s = pl.program_id(0)
    seq, chunk = _seq_of_step(s)
    _route_init(cnt_ref, ltri_ref)
    route_refs = (g_ref, b_ref, wr_ref, h_ref, r_ref, cnt_out_ref, hp_ref, rt_ref, cnt_ref, ltri_ref)

    def glu(x):
        hh = jnp.dot(x.astype(jnp.bfloat16), wglu_ref[...], preferred_element_type=jnp.float32) + bglu_ref[...]
        return hh[:, :D] * jax.nn.sigmoid(hh[:, D:])

    def finish(x, v, r0):
        v = _layer_norm(v + bdw_ref[...], lg_ref[...], lb_ref[...])
        v = v * jax.nn.sigmoid(v)
        m = jnp.dot(v.astype(jnp.bfloat16), wpw_ref[...], preferred_element_type=jnp.float32) + bpw_ref[...]
        _residual_route(x, m, r0, *route_refs)

    @pl.when(s == 0)
    def _():
        x = head_ref[...] if first_layer else x_ref[...]
        start_next()
        u = glu(x)
        um = u[0:N_HEAD, :]
        us = u[N_HEAD:TB, :]
        carry_ref[...] = um
        us_ref[...] = us
        ext_ref[0:N_META, :] = jnp.zeros((N_META, D), jnp.float32)
        ext_ref[N_META:N_META + N_HEAD, :] = um
        pos = jnp.bitwise_and(lax.broadcasted_iota(jnp.int32, (N_HEAD, 1), 0), N_META - 1)
        acc = wdw_ref[CONV_WIDTH - 1:CONV_WIDTH, :] * um
        for d in range(1, N_META):
            k = CONV_WIDTH - 1 - d
            acc = acc + wdw_ref[k:k + 1, :] * jnp.where(pos >= d, ext_ref[N_META - d:N_META - d + N_HEAD, :], 0.0)
        v_samp = vs_ref[...] + wdw_ref[CONV_WIDTH - 1:CONV_WIDTH, :] * us
        finish(x, jnp.concatenate([acc, v_samp], axis=0), 0)

    @pl.when(s > 0)
    def _():
        @pl.when(chunk == 0)
        def _():
            ext_ref[0:CONV_CARRY - N_META, :] = jnp.zeros((CONV_CARRY - N_META, D), jnp.float32)
            ext_ref[CONV_CARRY - N_META:CONV_CARRY, :] = carry_ref[
                pl.ds(pl.multiple_of(seq * N_META, N_META), N_META), :]

        for r0 in range(0, TB, HB):
            x = x_ref[r0:r0 + HB, :]
            start_next()
            ext_ref[CONV_CARRY + r0:CONV_CARRY + r0 + HB, :] = glu(x)
            for rb in range(r0, r0 + HB, CONV_RB):
                _depthwise_conv(ext_ref, wdw_ref, v_ref, rb)
            finish(x, v_ref[r0:r0 + HB, :], r0)
        ext_ref[0:CONV_CARRY, :] = ext_ref[TB:TB + CONV_CARRY, :]

        @pl.when(chunk == CHUNKS - 1)
        def _():
            ust_ref[0] = ext_ref[CONV_CARRY + TB - CONV_STATE:CONV_CARRY + TB, :]


assert CONV_CARRY - N_META + N_META >= CONV_STATE and N_META <= CONV_STATE


def _conv_layer(vs, wglu_bf, bglu, wdw, bdw, cln_g, cln_b, wpw_bf, bpw, ln_g, ln_b, wr, **inputs):
    return _mixer_call(
        "conv_mixer", _conv_kernel,
        [_const_spec((DEC_BATCH, D)),
         _const_spec((D, 2 * D)), _const_spec((1, 2 * D)),
         _const_spec((CONV_WIDTH, D)), _const_spec((1, D)),
         _const_spec((1, D)), _const_spec((1, D)),
         _const_spec((D, D)), _const_spec((1, D)),
         _const_spec((1, D)), _const_spec((1, D)),
         _const_spec((D, 2 * LANES))],
        [pl.BlockSpec((1, CONV_STATE, D), lambda s, *_: (jnp.maximum(s - 1, 0) // CHUNKS, 0, 0)),
         _const_spec((DEC_BATCH, D))],
        [jax.ShapeDtypeStruct((BATCH, CONV_STATE, D), jnp.float32),
         jax.ShapeDtypeStruct((DEC_BATCH, D), jnp.float32)],
        [pltpu.VMEM((CONV_CARRY + TB, D), jnp.float32),
         pltpu.VMEM((TB, D), jnp.float32),
         pltpu.VMEM((N_HEAD, D), jnp.float32),
         pltpu.VMEM((1, LANES), jnp.float32),
         pltpu.VMEM((HB, HB), jnp.bfloat16)],
        (vs, wglu_bf, bglu, wdw, bdw, cln_g, cln_b, wpw_bf, bpw, ln_g, ln_b, wr), **inputs)


def _tile_gather_start(src_hbm, dst, sem, row_index, n_rows, dst_tile=lambda r: r):
    for r in range(n_rows):
        tok = row_index(r)
        pltpu.make_async_copy(src_hbm.at[pl.ds(pl.multiple_of(tok * SUBLANES, SUBLANES), SUBLANES), :],
                              dst.at[pl.ds(dst_tile(r) * SUBLANES, SUBLANES), :], sem).start(priority=r % 2)


def _tile_gather_wait(src_hbm, dst, sem, n_rows):
    pltpu.make_async_copy(src_hbm.at[pl.ds(0, n_rows * SUBLANES), :], dst, sem).wait()


assert TOP_K == 2


def _expert_kernel(be_ref, nused_ref, base_ref, limit_ref, order_ref, hp_ref, wg_ref, wu_ref, wd_ref, ys_ref,
                   xbuf_even, xbuf_odd, wgb, wub, wdb):
    i = pl.program_id(0)
    nused = nused_ref[0]

    def gather(blk, xbuf):
        b0 = base_ref[blk]
        lim = limit_ref[blk]
        for r in range(BM):
            a = order_ref[jnp.minimum(b0 + r, lim)]
            xbuf[r] = hp_ref[jnp.where(a >= NTOK, a - NTOK, a)]

    def mlp_rows(cur, h0, n):
        words = jnp.concatenate([cur[h0:h0 + n, s, :] for s in range(PACK_ROWS)], axis=1)
        xb = _unpack_bf16(words)
        gate = jnp.dot(xb, wgb[...], preferred_element_type=jnp.float32)
        up = jnp.dot(xb, wub[...], preferred_element_type=jnp.float32)
        hid = gate * jax.nn.sigmoid(gate) * up
        y = jnp.dot(hid.astype(jnp.bfloat16), wdb[...], preferred_element_type=jnp.float32)
        _tile_store(ys_ref.at[pl.ds(h0 * SUBLANES, n * SUBLANES), :], y, n)

    def block(cur, nxt):
        nxt_blk = jnp.minimum(i + 1, jnp.maximum(nused - 1, 0))
        half = BM // 2
        has_second = limit_ref[i] - base_ref[i] >= half

        @pl.when(has_second)
        def _():
            gather(nxt_blk, nxt)
            mlp_rows(cur, 0, BM)

        @pl.when(jnp.logical_not(has_second))
        def _():
            gather(nxt_blk, nxt)
            mlp_rows(cur, 0, half)
            ys_ref[pl.ds(half * SUBLANES, half * SUBLANES), :] = jnp.zeros((half * SUBLANES, LANES), jnp.float32)

    @pl.when(i == 0)
    def _():
        gather(0, xbuf_even)

    @pl.when(jnp.logical_and(i < nused, jnp.logical_or(i == 0, be_ref[i] != be_ref[jnp.maximum(i - 1, 0)])))
    def _():
        wgb[...] = wg_ref[0, 0].astype(jnp.bfloat16)
        wub[...] = wu_ref[0, 0].astype(jnp.bfloat16)
        wdb[...] = wd_ref[0, 0].astype(jnp.bfloat16)

    odd = jnp.bitwise_and(i, 1) == 1

    @pl.when(jnp.logical_and(i < nused, jnp.logical_not(odd)))
    def _():
        block(xbuf_even, xbuf_odd)

    @pl.when(jnp.logical_and(i < nused, odd))
    def _():
        block(xbuf_odd, xbuf_even)

    @pl.when(i >= nused)
    def _():
        ys_ref[...] = jnp.zeros((BM * SUBLANES, LANES), jnp.float32)


def _expert_layer(layer, block_expert, nused, base, limit, order, h_packed, w_gate, w_up, w_down):
    def w_spec(shape):
        return pl.BlockSpec((1, 1) + shape, lambda i, be, *_: (layer, be[i], 0, 0))
    grid_spec = pltpu.PrefetchScalarGridSpec(
        num_scalar_prefetch=5,
        grid=(NB,),
        in_specs=[pl.BlockSpec((NTOK, PACK_ROWS, LANES), lambda i, *_: (0, 0, 0)),
                  w_spec((D, D_EXPERT)), w_spec((D, D_EXPERT)), w_spec((D_EXPERT, D))],
        out_specs=pl.BlockSpec((BM * SUBLANES, LANES), lambda i, *_: (i, 0)),
        scratch_shapes=[pltpu.VMEM((BM, PACK_ROWS, LANES), jnp.uint32),
                        pltpu.VMEM((BM, PACK_ROWS, LANES), jnp.uint32),
                        pltpu.VMEM((D, D_EXPERT), jnp.bfloat16),
                        pltpu.VMEM((D, D_EXPERT), jnp.bfloat16),
                        pltpu.VMEM((D_EXPERT, D), jnp.bfloat16)],
    )
    return pl.pallas_call(
        _expert_kernel,
        grid_spec=grid_spec,
        out_shape=jax.ShapeDtypeStruct((N_SLOT * SUBLANES, LANES), jnp.float32),
        compiler_params=pltpu.CompilerParams(dimension_semantics=("arbitrary",),
                                             vmem_limit_bytes=EXPERT_VMEM_LIMIT),
        name="expert_mlp",
    )(block_expert, nused, base, limit, order, h_packed, w_gate, w_up, w_down)


def _combine_kernel(last_layer, slot_ref, h_ref, r_ref, ys_hbm, g_ref, b_ref, *rest):
    if last_layer:
        yp_ref, ysamp_ref, ybuf, sem = rest
    else:
        o_ref, ybuf, sem = rest
    i = pl.program_id(0)
    n = pl.num_programs(0)

    def start(step, buf):
        base = step * TB
        _tile_gather_start(ys_hbm, ybuf.at[buf], sem.at[buf],
                           lambda r: slot_ref[(r % TOP_K) * NTOK + base + r // TOP_K], TOP_K * TB,
                           dst_tile=lambda r: (r % TOP_K) * TB + r // TOP_K)

    @pl.when(i == 0)
    def _():
        start(0, 0)

    buf = i % 2
    _tile_gather_wait(ys_hbm, ybuf.at[buf], sem.at[buf], TOP_K * TB)
    yb = ybuf.at[buf]
    y0 = _tile_load(yb.at[pl.ds(0, TB * SUBLANES), :], TB)
    y1 = _tile_load(yb.at[pl.ds(TB * SUBLANES, TB * SUBLANES), :], TB)
    h = h_ref[...]
    r = r_ref[...]

    start(jnp.minimum(i + 1, n - 1), 1 - buf)

    f = r[:, 0:1] * y0 + r[:, 1:2] * y1
    out = _layer_norm(ALPHA * h + f, g_ref[...], b_ref[...])
    if last_layer:
        yp_ref[...] = out

        @pl.when(i == 0)
        def _():
            ysamp_ref[...] = out[N_HEAD:TB, :]
    else:
        o_ref[...] = out

    @pl.when(i == n - 1)
    def _():
        _tile_gather_wait(ys_hbm, ybuf.at[1 - buf], sem.at[1 - buf], TOP_K * TB)


def _combine_layer(last_layer, slot, h_tiles, r_all, ys, ln_g, ln_b):
    if last_layer:
        out_specs = [pl.BlockSpec((TB, D), lambda i, sl: (jnp.maximum(i - 1, 0), 0)),
                     pl.BlockSpec((DEC_BATCH, D), lambda i, sl: (0, 0))]
        out_shape = [jax.ShapeDtypeStruct((BATCH * SEQ, D), jnp.float32),
                     jax.ShapeDtypeStruct((DEC_BATCH, D), jnp.float32)]
    else:
        out_specs = pl.BlockSpec((TB, D), lambda i, sl: (i, 0))
        out_shape = jax.ShapeDtypeStruct((NTOK, D), jnp.float32)
    grid_spec = pltpu.PrefetchScalarGridSpec(
        num_scalar_prefetch=1,
        grid=(N_STEP,),
        in_specs=[pl.BlockSpec((TB, D), lambda i, sl: (i, 0)),
                  pl.BlockSpec((TB, LANES), lambda i, sl: (i, 0)),
                  pl.BlockSpec(memory_space=pl.ANY),
                  pl.BlockSpec((1, D), lambda i, sl: (0, 0)),
                  pl.BlockSpec((1, D), lambda i, sl: (0, 0))],
        out_specs=out_specs,
        scratch_shapes=[pltpu.VMEM((2, TOP_K * TB * SUBLANES, LANES), jnp.float32),
                        pltpu.SemaphoreType.DMA((2,))],
    )
    return pl.pallas_call(
        functools.partial(_combine_kernel, last_layer),
        grid_spec=grid_spec,
        out_shape=out_shape,
        compiler_params=pltpu.CompilerParams(dimension_semantics=("arbitrary",),
                                             vmem_limit_bytes=VMEM_LIMIT),
        name="moe_combine",
    )(slot, h_tiles, r_all, ys, ln_g, ln_b)


def _routing_metadata(rt, cnt):
    experts = jnp.arange(N_EXPERTS, dtype=jnp.int32)
    counts = cnt[0, :N_EXPERTS].astype(jnp.int32)
    nblk = (counts + BM - 1) // BM
    blk_end = jnp.cumsum(nblk)
    pstart = (blk_end - nblk) * BM
    starts = jnp.cumsum(counts) - counts
    nused = blk_end[-1]
    blk = jnp.arange(NB, dtype=jnp.int32)
    block_expert = jnp.sum((blk[:, None] >= blk_end[None, :]).astype(jnp.int32), axis=1)
    last_expert = jnp.max(jnp.where(nblk > 0, experts, 0))
    block_expert = jnp.where(blk < nused, block_expert, last_expert).astype(jnp.int32)
    of_block = block_expert[:, None] == experts[None, :]
    base = blk * BM + jnp.sum(jnp.where(of_block, (starts - pstart)[None, :], 0), axis=1)
    limit = jnp.sum(jnp.where(of_block, (starts + counts - 1)[None, :], 0), axis=1)
    rows = rt.reshape(N_STEP, SUBLANES, TB)
    expert = jnp.transpose(rows[:, 2:2 + TOP_K, :], (1, 0, 2)).reshape(N_ASSIGN).astype(jnp.int32)
    rank = jnp.transpose(rows[:, 2 + TOP_K:2 + 2 * TOP_K, :], (1, 0, 2)).reshape(N_ASSIGN).astype(jnp.int32)
    slot = rank + jnp.sum(jnp.where(expert[:, None] == experts[None, :], pstart[None, :], 0), axis=1)
    order = jnp.argsort(expert * RANK_RANGE + rank).astype(jnp.int32)
    return (block_expert, nused.reshape(1).astype(jnp.int32), base.astype(jnp.int32),
            limit.astype(jnp.int32), order, slot.astype(jnp.int32))


def kernel(x_prompt, x_sample, state_pool, state_conv, meta, w_pool, pool_scale, w_glu, b_glu, w_dw, b_dw, conv_ln_g, conv_ln_b, w_pw, b_pw, ln_mix_g, ln_mix_b, ln_ffn_g, ln_ffn_b, w_router_group, w_router_expert, w_gate, w_up, w_down):
    f32 = jnp.float32
    x_samp2d = x_sample.reshape(DEC_BATCH, D)
    head = jnp.concatenate([jnp.tile(meta.astype(f32), (BATCH, 1)), x_samp2d], axis=0)
    k_idx = jnp.arange(POOL_STATE)[:, None]
    win = jnp.repeat(jnp.asarray(POOL_WINDOWS), POOL_GROUP_DIM)[None, :]
    pool_coef = (k_idx >= (POOL_STATE + 1 - win)).astype(f32)

    new_pool_p, new_pool_s, new_conv_p, new_conv_s = [], [], [], []
    for i in range(DEPTH):
        j = i // 2
        wr32 = jnp.concatenate([w_router_group[i], w_router_expert[i],
                                jnp.zeros((D, LANES - N_GROUPS - N_EXPERTS), f32)], axis=1)
        wr_hi = wr32.astype(jnp.bfloat16)
        wr = jnp.concatenate([wr_hi, (wr32 - wr_hi.astype(f32)).astype(jnp.bfloat16)], axis=1)
        ln_g, ln_b = ln_mix_g[i].reshape(1, D), ln_mix_b[i].reshape(1, D)
        if i == 0:
            inputs = dict(first_inputs=(head, x_prompt.reshape(BATCH * SEQ, D)))
        else:
            inputs = dict(fused_inputs=(slot, h1, r_all, ys, ln_ffn_g[i - 1].reshape(1, D),
                                        ln_ffn_b[i - 1].reshape(1, D)))
        if i % 2 == 0:
            ps = _state_presum(j, state_pool, pool_coef)
            outs = _pool_layer(ps, w_pool[j].astype(jnp.bfloat16), pool_scale[j].reshape(1, D),
                               ln_g, ln_b, wr, **inputs)
            if i == 0:
                tail, x_s = x_prompt[:, SEQ - POOL_STATE:], x_samp2d
            else:
                x_s, tail16, outs = outs[0], outs[1], outs[2:]
                tail = tail16[:, POOL_CARRY - POOL_STATE:]
            h1, r_all, cnt, h1p, rt = outs
            new_pool_p.append(tail)
            new_pool_s.append(jnp.concatenate([state_pool[j][:, 1:], x_s[:, None, :]], axis=1))
        else:
            vs = _state_presum(j, state_conv, w_dw[j][:CONV_STATE])
            outs = _conv_layer(
                vs, w_glu[j].astype(jnp.bfloat16), b_glu[j].reshape(1, 2 * D), w_dw[j],
                b_dw[j].reshape(1, D), conv_ln_g[j].reshape(1, D), conv_ln_b[j].reshape(1, D),
                w_pw[j].astype(jnp.bfloat16), b_pw[j].reshape(1, D), ln_g, ln_b, wr, **inputs)
            h1, r_all, cnt, h1p, rt, ust, u_s = outs if i == 0 else outs[2:]
            new_conv_p.append(ust)
            new_conv_s.append(jnp.concatenate([state_conv[j][:, 1:], u_s[:, None, :]], axis=1))
        block_expert, nused, base, limit, order, slot = _routing_metadata(rt, cnt)
        ys = _expert_layer(i, block_expert, nused, base, limit, order, h1p, w_gate, w_up, w_down)

    y_prompt, y_samp = _combine_layer(True, slot, h1, r_all, ys, ln_ffn_g[DEPTH - 1].reshape(1, D),
                                      ln_ffn_b[DEPTH - 1].reshape(1, D))
    return (y_prompt.reshape(BATCH, SEQ, D), y_samp.reshape(DEC_BATCH, 1, D),
            jnp.stack(new_pool_p), jnp.stack(new_pool_s), jnp.stack(new_conv_p), jnp.stack(new_conv_s))
```

```python
import functools

import jax
import jax.numpy as jnp
from jax import lax
from jax.experimental import pallas as pl
from jax.experimental.pallas import tpu as pltpu

D = 1024
BATCH = 8
SEQ = 2048
DEPTH = 4
DEC_BATCH = 128
N_META = 16
POOL_WINDOWS = (2, 4, 8, 16)
POOL_GROUP_DIM = D // len(POOL_WINDOWS)
POOL_STATE = max(POOL_WINDOWS) - 1
CONV_WIDTH = 31
CONV_STATE = CONV_WIDTH - 1
N_GROUPS = 4
EXPERTS_PER_GROUP = 8
N_EXPERTS = N_GROUPS * EXPERTS_PER_GROUP
TOP_K = 2
D_EXPERT = D // 2
ALPHA = (2.0 * DEPTH) ** 0.25
LN_EPS = 1e-5

LANES = 128
SUBLANES = 8
N_HEAD = BATCH * N_META
TB = 256
NTOK = N_HEAD + DEC_BATCH + BATCH * SEQ
N_STEP = NTOK // TB
CHUNKS = SEQ // TB
POOL_CARRY = 16
CONV_CARRY = 32
CONV_RB = 128
CONV_CB = 128
BM = 512
N_ASSIGN = NTOK * TOP_K
RANK_RANGE = 1 << 16
NB = N_ASSIGN // BM + N_EXPERTS
N_SLOT = NB * BM
VMEM_LIMIT = 48 * 1024 * 1024
EXPERT_VMEM_LIMIT = 56 * 1024 * 1024
NEG = -1e30

assert D == SUBLANES * LANES and N_HEAD + DEC_BATCH == TB and SEQ % TB == 0
assert N_ASSIGN % BM == 0 and TB % CONV_RB == 0 and N_META == POOL_CARRY and CONV_STATE <= TB


def _tile_load(ref, n):
    return jnp.concatenate([ref[pl.ds(s, n, stride=SUBLANES), :] for s in range(SUBLANES)], axis=1)


def _tile_store(ref, val, n):
    for s in range(SUBLANES):
        ref[pl.ds(s, n, stride=SUBLANES), :] = val[:, s * LANES:(s + 1) * LANES]


def _layer_norm(x, g, b):
    mu = jnp.mean(x, axis=-1, keepdims=True)
    xc = x - mu
    var = jnp.mean(xc * xc, axis=-1, keepdims=True)
    return xc * lax.rsqrt(var + LN_EPS) * g + b


PACK_ROWS = D // 2 // LANES


def _pack_bf16(h_bf16, hp_ref, r0):
    n = h_bf16.shape[0]
    bits = pltpu.bitcast(h_bf16.astype(jnp.float32), jnp.uint32)
    words = jnp.bitwise_or(lax.shift_right_logical(bits[:, :D // 2], jnp.uint32(16)),
                           jnp.bitwise_and(bits[:, D // 2:], jnp.uint32(0xFFFF0000)))
    for s in range(PACK_ROWS):
        hp_ref[r0:r0 + n, s, :] = words[:, s * LANES:(s + 1) * LANES]


def _unpack_bf16(words):
    lo = pltpu.bitcast(lax.shift_left(words, jnp.uint32(16)), jnp.float32)
    hi = pltpu.bitcast(jnp.bitwise_and(words, jnp.uint32(0xFFFF0000)), jnp.float32)
    return jnp.concatenate([lo, hi], axis=1).astype(jnp.bfloat16)


HB = TB


def _route_init(cnt_ref, ltri_ref):
    @pl.when(pl.program_id(0) == 0)
    def _():
        cnt_ref[...] = jnp.zeros((1, LANES), jnp.float32)
        row = lax.broadcasted_iota(jnp.int32, (HB, HB), 0)
        col = lax.broadcasted_iota(jnp.int32, (HB, HB), 1)
        ltri_ref[...] = jnp.where(col < row, 1.0, 0.0).astype(jnp.bfloat16)


def _residual_route(x, m, r0, g_ref, b_ref, wr_ref, h_ref, r_ref, cnt_out_ref, hp_ref, rt_ref, cnt_ref,
                    ltri_ref):
    h = _layer_norm(ALPHA * x + m, g_ref[...], b_ref[...])
    h_ref[r0:r0 + HB, :] = h
    h_hi = h.astype(jnp.bfloat16)
    _pack_bf16(h_hi, hp_ref, r0)
    h_lo = (h - h_hi.astype(jnp.float32)).astype(jnp.bfloat16)
    hi_both = jnp.dot(h_hi, wr_ref[...], preferred_element_type=jnp.float32)
    lo_hi = jnp.dot(h_lo, wr_ref[:, 0:LANES], preferred_element_type=jnp.float32)
    logits = hi_both[:, 0:LANES] + (hi_both[:, LANES:2 * LANES] + lo_hi)
    lane = lax.broadcasted_iota(jnp.int32, logits.shape, 1)
    lanef = lane.astype(jnp.float32)
    big = jnp.float32(1e9)
    lg = jnp.where(lane < N_GROUPS, logits, NEG)
    mg = jnp.max(lg, axis=1, keepdims=True)
    gidx = jnp.min(jnp.where(lg == mg, lanef, big), axis=1, keepdims=True)
    p_grp = 1.0 / jnp.sum(jnp.where(lane < N_GROUPS, jnp.exp(lg - mg), 0.0), axis=1, keepdims=True)
    lo = N_GROUPS + EXPERTS_PER_GROUP * gidx
    le = jnp.where(lanef >= lo, jnp.where(lanef < lo + EXPERTS_PER_GROUP, logits, NEG), NEG)
    m1 = jnp.max(le, axis=1, keepdims=True)
    i1 = jnp.min(jnp.where(le == m1, lanef, big), axis=1, keepdims=True)
    le2 = jnp.where(lanef == i1, NEG, le)
    m2 = jnp.max(le2, axis=1, keepdims=True)
    i2 = jnp.min(jnp.where(le2 == m2, lanef, big), axis=1, keepdims=True)
    ratio = jnp.exp(m2 - m1)
    g1 = p_grp / (1.0 + ratio)
    g2 = g1 * ratio
    e1 = i1 - N_GROUPS
    e2 = i2 - N_GROUPS

    is1 = lanef == e1
    is2 = lanef == e2
    onehot = jnp.where(is1, 1.0, jnp.where(is2, 1.0, 0.0))
    before = jnp.dot(ltri_ref[...], onehot.astype(jnp.bfloat16),
                     preferred_element_type=jnp.float32) + cnt_ref[...]
    rank1 = jnp.sum(jnp.where(is1, before, 0.0), axis=1, keepdims=True)
    rank2 = jnp.sum(jnp.where(is2, before, 0.0), axis=1, keepdims=True)
    cnt = cnt_ref[...] + jnp.sum(onehot, axis=0, keepdims=True)
    cnt_ref[...] = cnt
    cnt_out_ref[...] = cnt

    packed = jnp.where(lane == 0, g1,
                       jnp.where(lane == 1, g2,
                                 jnp.where(lane == 2, e1,
                                           jnp.where(lane == 3, e2,
                                                     jnp.where(lane == 4, rank1,
                                                               jnp.where(lane == 5, rank2, 0.0))))))
    r_ref[r0:r0 + HB, :] = packed
    rt_ref[:, r0:r0 + HB] = jnp.transpose(packed)[0:SUBLANES, :]


def _seq_of_step(s):
    sm1 = jnp.maximum(s - 1, 0)
    return lax.shift_right_logical(sm1, CHUNKS.bit_length() - 1), jnp.bitwise_and(sm1, CHUNKS - 1)


assert CHUNKS & (CHUNKS - 1) == 0


PRESUM_ROWS = 32


def _presum_kernel(st_ref, coef_ref, o_ref):
    coef = coef_ref[...]
    for n in range(PRESUM_ROWS):
        o_ref[n:n + 1, :] = jnp.sum(st_ref[0, n] * coef, axis=0, keepdims=True)


def _state_presum(j, state, coef):
    _, n, k, _ = state.shape
    return pl.pallas_call(
        _presum_kernel,
        grid=(n // PRESUM_ROWS,),
        in_specs=[pl.BlockSpec((1, PRESUM_ROWS, k, D), lambda i: (j, i, 0, 0)),
                  pl.BlockSpec((k, D), lambda i: (0, 0))],
        out_specs=pl.BlockSpec((PRESUM_ROWS, D), lambda i: (i, 0)),
        out_shape=jax.ShapeDtypeStruct((n, D), jnp.float32),
        compiler_params=pltpu.CompilerParams(dimension_semantics=("arbitrary",),
                                             vmem_limit_bytes=VMEM_LIMIT),
        name="state_presum",
    )(state, coef)


def _pool_kernel(first_layer, head_ref, x_ref, ps_ref, wp_ref, sc_ref, g_ref, b_ref, wr_ref,
                 h_ref, r_ref, cnt_out_ref, hp_ref, rt_ref, ext_ref, m_ref, carry_ref, cnt_ref, ltri_ref,
                 start_next=lambda: None):
    s = pl.program_id(0)
    seq, chunk = _seq_of_step(s)
    _route_init(cnt_ref, ltri_ref)
    route_refs = (g_ref, b_ref, wr_ref, h_ref, r_ref, cnt_out_ref, hp_ref, rt_ref, cnt_ref, ltri_ref)

    @pl.when(s == 0)
    def _():
        x = head_ref[...] if first_layer else x_ref[...]
        start_next()
        ext_ref[0:POOL_CARRY, :] = jnp.zeros((POOL_CARRY, D), jnp.float32)
        ext_ref[POOL_CARRY:POOL_CARRY + TB, :] = x
        carry_ref[...] = x[0:N_HEAD, :]
        pos = jnp.bitwise_and(lax.broadcasted_iota(jnp.int32, (N_HEAD, 1), 0), N_META - 1)
        for g, w in enumerate(POOL_WINDOWS):
            lo, hi = g * POOL_GROUP_DIM, (g + 1) * POOL_GROUP_DIM
            xm = x[0:N_HEAD, lo:hi]
            acc = xm
            for k in range(1, w):
                acc = acc + jnp.where(pos >= k, ext_ref[POOL_CARRY - k:POOL_CARRY - k + N_HEAD, lo:hi], 0.0)
            cnt = jnp.minimum(pos + 1, w).astype(jnp.float32)
            d_meta = acc / cnt - xm
            xs = x[N_HEAD:TB, lo:hi]
            d_samp = (ps_ref[:, lo:hi] + xs) / float(w) - xs
            diff = jnp.concatenate([d_meta, d_samp], axis=0)
            y = jnp.dot(diff.astype(jnp.bfloat16), wp_ref[g], preferred_element_type=jnp.float32)
            m_ref[:, lo:hi] = y * sc_ref[:, lo:hi]
        for r0 in range(0, TB, HB):
            _residual_route(x[r0:r0 + HB], m_ref[r0:r0 + HB, :], r0, *route_refs)

    @pl.when(s > 0)
    def _():
        @pl.when(chunk == 0)
        def _():
            ext_ref[0:POOL_CARRY, :] = carry_ref[pl.ds(pl.multiple_of(seq * N_META, N_META), N_META), :]

        for r0 in range(0, TB, HB):
            x = x_ref[r0:r0 + HB, :]
            start_next()
            ext_ref[POOL_CARRY + r0:POOL_CARRY + r0 + HB, :] = x
            ms = []
            for g, w in enumerate(POOL_WINDOWS):
                lo, hi = g * POOL_GROUP_DIM, (g + 1) * POOL_GROUP_DIM
                xg = x[:, lo:hi]
                acc = xg
                for k in range(1, w):
                    acc = acc + ext_ref[POOL_CARRY + r0 - k:POOL_CARRY + r0 - k + HB, lo:hi]
                diff = acc / float(w) - xg
                y = jnp.dot(diff.astype(jnp.bfloat16), wp_ref[g], preferred_element_type=jnp.float32)
                ms.append(y * sc_ref[:, lo:hi])
            _residual_route(x, jnp.concatenate(ms, axis=1), r0, *route_refs)
        ext_ref[0:POOL_CARRY, :] = ext_ref[TB:TB + POOL_CARRY, :]


assert N_META >= max(POOL_WINDOWS)


def _const_spec(shape):
    nd = len(shape)
    return pl.BlockSpec(shape, lambda s, *_: (0,) * nd)


def _x_spec(first_layer):
    if first_layer:
        return pl.BlockSpec((TB, D), lambda s, *_: (jnp.maximum(s - 1, 0), 0))
    return pl.BlockSpec((TB, D), lambda s, *_: (s, 0))


_MIXER_OUT_SPECS = [pl.BlockSpec((TB, D), lambda s, *_: (s, 0)),
                    pl.BlockSpec((TB, LANES), lambda s, *_: (s, 0)),
                    pl.BlockSpec((1, LANES), lambda s, *_: (0, 0)),
                    pl.BlockSpec((TB, PACK_ROWS, LANES), lambda s, *_: (s, 0, 0)),
                    pl.BlockSpec((SUBLANES, TB), lambda s, *_: (s, 0))]
_MIXER_OUT_SHAPES = [jax.ShapeDtypeStruct((NTOK, D), jnp.float32),
                     jax.ShapeDtypeStruct((NTOK, LANES), jnp.float32),
                     jax.ShapeDtypeStruct((1, LANES), jnp.float32),
                     jax.ShapeDtypeStruct((NTOK, PACK_ROWS, LANES), jnp.uint32),
                     jax.ShapeDtypeStruct((N_STEP * SUBLANES, TB), jnp.float32)]


def _fused_mixer_kernel(body, n_in, slot_ref, hprev_ref, rprev_ref, ys_hbm, fg_ref, fb_ref, *rest):
    mixer_in = rest[:n_in]
    xs_out_ref, tail_ref = rest[n_in:n_in + 2]
    x_scr, ybuf, sem = rest[-3:]
    i = pl.program_id(0)
    n = pl.num_programs(0)

    def start(step, buf):
        base = step * TB
        _tile_gather_start(ys_hbm, ybuf.at[buf], sem.at[buf],
                           lambda r: slot_ref[(r % TOP_K) * NTOK + base + r // TOP_K], TOP_K * TB,
                           dst_tile=lambda r: (r % TOP_K) * TB + r // TOP_K)

    @pl.when(i == 0)
    def _():
        start(0, 0)

    buf = i % 2
    _tile_gather_wait(ys_hbm, ybuf.at[buf], sem.at[buf], TOP_K * TB)
    yb = ybuf.at[buf]
    y0 = _tile_load(yb.at[pl.ds(0, TB * SUBLANES), :], TB)
    y1 = _tile_load(yb.at[pl.ds(TB * SUBLANES, TB * SUBLANES), :], TB)
    r = rprev_ref[...]
    x = _layer_norm(ALPHA * hprev_ref[...] + (r[:, 0:1] * y0 + r[:, 1:2] * y1), fg_ref[...], fb_ref[...])
    x_scr[...] = x

    @pl.when(i == 0)
    def _():
        xs_out_ref[...] = x[N_HEAD:TB, :]

    _, chunk = _seq_of_step(i)

    @pl.when(jnp.logical_and(i > 0, chunk == CHUNKS - 1))
    def _():
        tail_ref[0] = x[TB - POOL_CARRY:TB, :]

    body(False, None, x_scr, *mixer_in, *rest[n_in + 2:-3],
         start_next=lambda: start(jnp.minimum(i + 1, n - 1), 1 - buf))

    @pl.when(i == n - 1)
    def _():
        _tile_gather_wait(ys_hbm, ybuf.at[1 - buf], sem.at[1 - buf], TOP_K * TB)


def _mixer_call(name, body, mixer_in_specs, extra_out_specs, extra_out_shapes, scratch_shapes, mixer_inputs,
                first_inputs=None, fused_inputs=None):
    out_specs = _MIXER_OUT_SPECS + extra_out_specs
    out_shape = _MIXER_OUT_SHAPES + extra_out_shapes
    params = pltpu.CompilerParams(dimension_semantics=("arbitrary",), vmem_limit_bytes=VMEM_LIMIT)
    if fused_inputs is None:
        return pl.pallas_call(
            functools.partial(body, True),
            grid=(N_STEP,),
            in_specs=[_const_spec((TB, D)), _x_spec(True)] + mixer_in_specs,
            out_specs=out_specs, out_shape=out_shape, scratch_shapes=scratch_shapes,
            compiler_params=params, name=name,
        )(*first_inputs, *mixer_inputs)
    seq_block = lambda s, *_: (jnp.maximum(s - 1, 0) // CHUNKS, 0, 0)
    grid_spec = pltpu.PrefetchScalarGridSpec(
        num_scalar_prefetch=1,
        grid=(N_STEP,),
        in_specs=[pl.BlockSpec((TB, D), lambda s, *_: (s, 0)),
                  pl.BlockSpec((TB, LANES), lambda s, *_: (s, 0)),
                  pl.BlockSpec(memory_space=pl.ANY),
                  _const_spec((1, D)), _const_spec((1, D))] + mixer_in_specs,
        out_specs=[_const_spec((DEC_BATCH, D)), pl.BlockSpec((1, POOL_CARRY, D), seq_block)] + out_specs,
        scratch_shapes=scratch_shapes + [pltpu.VMEM((TB, D), jnp.float32),
                                         pltpu.VMEM((2, TOP_K * TB * SUBLANES, LANES), jnp.float32),
                                         pltpu.SemaphoreType.DMA((2,))],
    )
    return pl.pallas_call(
        functools.partial(_fused_mixer_kernel, body, len(mixer_in_specs)),
        grid_spec=grid_spec,
        out_shape=[jax.ShapeDtypeStruct((DEC_BATCH, D), jnp.float32),
                   jax.ShapeDtypeStruct((BATCH, POOL_CARRY, D), jnp.float32)] + out_shape,
        compiler_params=params, name=name + "_fused",
    )(*fused_inputs, *mixer_inputs)


def _pool_layer(ps, wp_bf, scale, ln_g, ln_b, wr, **inputs):
    return _mixer_call(
        "pool_mixer", _pool_kernel,
        [_const_spec((DEC_BATCH, D)),
         _const_spec((len(POOL_WINDOWS), POOL_GROUP_DIM, POOL_GROUP_DIM)),
         _const_spec((1, D)), _const_spec((1, D)), _const_spec((1, D)),
         _const_spec((D, 2 * LANES))],
        [], [],
        [pltpu.VMEM((POOL_CARRY + TB, D), jnp.float32),
         pltpu.VMEM((TB, D), jnp.float32),
         pltpu.VMEM((N_HEAD, D), jnp.float32),
         pltpu.VMEM((1, LANES), jnp.float32),
         pltpu.VMEM((HB, HB), jnp.bfloat16)],
        (ps, wp_bf, scale, ln_g, ln_b, wr), **inputs)


def _depthwise_conv(ext_ref, wdw_ref, v_ref):
    base = CONV_CARRY - CONV_STATE
    for cb in range(D // CONV_CB):
        lo, hi = cb * CONV_CB, (cb + 1) * CONV_CB

        def body(i, carry, lo=lo, hi=hi):
            r0 = pl.multiple_of(i * CONV_RB, CONV_RB)
            sub = ext_ref.at[pl.ds(r0, CONV_RB + CONV_CARRY), :]
            v = None
            for r in range(SUBLANES):
                qs = [q for q in range((base + CONV_WIDTH) // SUBLANES + 1)
                      if 0 <= SUBLANES * q + r - base < CONV_WIDTH]
                z0 = SUBLANES * qs[0] + r
                z = sub[z0:SUBLANES * qs[-1] + r + CONV_RB, lo:hi]
                p = None
                for q in qs:
                    k = SUBLANES * q + r - base
                    off = SUBLANES * (q - qs[0])
                    term = wdw_ref[k:k + 1, lo:hi] * z[off:off + CONV_RB]
                    p = term if p is None else p + term
                v = p if v is None else v + p
            v_ref[pl.ds(r0, CONV_RB), lo:hi] = v
            return carry
        lax.fori_loop(0, TB // CONV_RB, body, 0)


assert TB % CONV_RB == 0 and HB == TB


def _conv_kernel(first_layer, head_ref, x_ref, vs_ref, wglu_ref, bglu_ref, wdw_ref, bdw_ref, lg_ref, lb_ref,
                 wpw_ref, bpw_ref, g_ref, b_ref, wr_ref, h_ref, r_ref, cnt_out_ref, hp_ref, rt_ref, ust_ref, us_ref,
                 ext_ref, v_ref, carry_ref, cnt_ref, ltri_ref, start_next=lambda: None):
    s = pl.program_id(0)
    seq, chunk = _seq_of_step(s)
    _route_init(cnt_ref, ltri_ref)
    route_refs = (g_ref, b_ref, wr_ref, h_ref, r_ref, cnt_out_ref, hp_ref, rt_ref, cnt_ref, ltri_ref)

    def glu(x):
        hh = jnp.dot(x.astype(jnp.bfloat16), wglu_ref[...], preferred_element_type=jnp.float32) + bglu_ref[...]
        return hh[:, :D] * jax.nn.sigmoid(hh[:, D:])

    def finish(x, v, r0):
        v = _layer_norm(v + bdw_ref[...], lg_ref[...], lb_ref[...])
        v = v * jax.nn.sigmoid(v)
        m = jnp.dot(v.astype(jnp.bfloat16), wpw_ref[...], preferred_element_type=jnp.float32) + bpw_ref[...]
        _residual_route(x, m, r0, *route_refs)

    @pl.when(s == 0)
    def _():
        x = head_ref[...] if first_layer else x_ref[...]
        start_next()
        u = glu(x)
        um = u[0:N_HEAD, :]
        us = u[N_HEAD:TB, :]
        carry_ref[...] = um
        us_ref[...] = us
        ext_ref[0:N_META, :] = jnp.zeros((N_META, D), jnp.float32)
        ext_ref[N_META:N_META + N_HEAD, :] = um
        pos = jnp.bitwise_and(lax.broadcasted_iota(jnp.int32, (N_HEAD, 1), 0), N_META - 1)
        acc = wdw_ref[CONV_WIDTH - 1:CONV_WIDTH, :] * um
        for d in range(1, N_META):
            k = CONV_WIDTH - 1 - d
            acc = acc + wdw_ref[k:k + 1, :] * jnp.where(pos >= d, ext_ref[N_META - d:N_META - d + N_HEAD, :], 0.0)
        v_samp = vs_ref[...] + wdw_ref[CONV_WIDTH - 1:CONV_WIDTH, :] * us
        finish(x, jnp.concatenate([acc, v_samp], axis=0), 0)

    @pl.when(s > 0)
    def _():
        @pl.when(chunk == 0)
        def _():
            ext_ref[0:CONV_CARRY - N_META, :] = jnp.zeros((CONV_CARRY - N_META, D), jnp.float32)
            ext_ref[CONV_CARRY - N_META:CONV_CARRY, :] = carry_ref[
                pl.ds(pl.multiple_of(seq * N_META, N_META), N_META), :]

        for r0 in range(0, TB, HB):
            x = x_ref[r0:r0 + HB, :]
            start_next()
            ext_ref[CONV_CARRY + r0:CONV_CARRY + r0 + HB, :] = glu(x)
            _depthwise_conv(ext_ref, wdw_ref, v_ref)
            finish(x, v_ref[r0:r0 + HB, :], r0)
        ext_ref[0:CONV_CARRY, :] = ext_ref[TB:TB + CONV_CARRY, :]

        @pl.when(chunk == CHUNKS - 1)
        def _():
            ust_ref[0] = ext_ref[CONV_CARRY + TB - CONV_STATE:CONV_CARRY + TB, :]


assert CONV_CARRY - N_META + N_META >= CONV_STATE and N_META <= CONV_STATE


def _conv_layer(vs, wglu_bf, bglu, wdw, bdw, cln_g, cln_b, wpw_bf, bpw, ln_g, ln_b, wr, **inputs):
    return _mixer_call(
        "conv_mixer", _conv_kernel,
        [_const_spec((DEC_BATCH, D)),
         _const_spec((D, 2 * D)), _const_spec((1, 2 * D)),
         _const_spec((CONV_WIDTH, D)), _const_spec((1, D)),
         _const_spec((1, D)), _const_spec((1, D)),
         _const_spec((D, D)), _const_spec((1, D)),
         _const_spec((1, D)), _const_spec((1, D)),
         _const_spec((D, 2 * LANES))],
        [pl.BlockSpec((1, CONV_STATE, D), lambda s, *_: (jnp.maximum(s - 1, 0) // CHUNKS, 0, 0)),
         _const_spec((DEC_BATCH, D))],
        [jax.ShapeDtypeStruct((BATCH, CONV_STATE, D), jnp.float32),
         jax.ShapeDtypeStruct((DEC_BATCH, D), jnp.float32)],
        [pltpu.VMEM((CONV_CARRY + TB, D), jnp.float32),
         pltpu.VMEM((TB, D), jnp.float32),
         pltpu.VMEM((N_HEAD, D), jnp.float32),
         pltpu.VMEM((1, LANES), jnp.float32),
         pltpu.VMEM((HB, HB), jnp.bfloat16)],
        (vs, wglu_bf, bglu, wdw, bdw, cln_g, cln_b, wpw_bf, bpw, ln_g, ln_b, wr), **inputs)


def _tile_gather_start(src_hbm, dst, sem, row_index, n_rows, dst_tile=lambda r: r):
    for r in range(n_rows):
        tok = row_index(r)
        pltpu.make_async_copy(src_hbm.at[pl.ds(pl.multiple_of(tok * SUBLANES, SUBLANES), SUBLANES), :],
                              dst.at[pl.ds(dst_tile(r) * SUBLANES, SUBLANES), :], sem).start(priority=r % 2)


def _tile_gather_wait(src_hbm, dst, sem, n_rows):
    pltpu.make_async_copy(src_hbm.at[pl.ds(0, n_rows * SUBLANES), :], dst, sem).wait()


assert TOP_K == 2


def _expert_kernel(be_ref, nused_ref, base_ref, limit_ref, order_ref, hp_ref, wg_ref, wu_ref, wd_ref, ys_ref,
                   xbuf_even, xbuf_odd, wgb, wub, wdb):
    i = pl.program_id(0)
    nused = nused_ref[0]

    def gather(blk, xbuf):
        b0 = base_ref[blk]
        lim = limit_ref[blk]
        for r in range(BM):
            a = order_ref[jnp.minimum(b0 + r, lim)]
            xbuf[r] = hp_ref[jnp.where(a >= NTOK, a - NTOK, a)]

    def mlp_rows(cur, h0, n):
        words = jnp.concatenate([cur[h0:h0 + n, s, :] for s in range(PACK_ROWS)], axis=1)
        xb = _unpack_bf16(words)
        gate = jnp.dot(xb, wgb[...], preferred_element_type=jnp.float32)
        up = jnp.dot(xb, wub[...], preferred_element_type=jnp.float32)
        hid = gate * jax.nn.sigmoid(gate) * up
        y = jnp.dot(hid.astype(jnp.bfloat16), wdb[...], preferred_element_type=jnp.float32)
        _tile_store(ys_ref.at[pl.ds(h0 * SUBLANES, n * SUBLANES), :], y, n)

    def block(cur, nxt):
        gather(jnp.minimum(i + 1, jnp.maximum(nused - 1, 0)), nxt)
        mlp_rows(cur, 0, BM)

    @pl.when(i == 0)
    def _():
        gather(0, xbuf_even)

    @pl.when(jnp.logical_and(i < nused, jnp.logical_or(i == 0, be_ref[i] != be_ref[jnp.maximum(i - 1, 0)])))
    def _():
        wgb[...] = wg_ref[0, 0].astype(jnp.bfloat16)
        wub[...] = wu_ref[0, 0].astype(jnp.bfloat16)
        wdb[...] = wd_ref[0, 0].astype(jnp.bfloat16)

    odd = jnp.bitwise_and(i, 1) == 1

    @pl.when(jnp.logical_and(i < nused, jnp.logical_not(odd)))
    def _():
        block(xbuf_even, xbuf_odd)

    @pl.when(jnp.logical_and(i < nused, odd))
    def _():
        block(xbuf_odd, xbuf_even)

    @pl.when(i >= nused)
    def _():
        ys_ref[...] = jnp.zeros((BM * SUBLANES, LANES), jnp.float32)


def _expert_layer(layer, block_expert, nused, base, limit, order, h_packed, w_gate, w_up, w_down):
    def w_spec(shape):
        return pl.BlockSpec((1, 1) + shape, lambda i, be, *_: (layer, be[i], 0, 0))
    grid_spec = pltpu.PrefetchScalarGridSpec(
        num_scalar_prefetch=5,
        grid=(NB,),
        in_specs=[pl.BlockSpec((NTOK, PACK_ROWS, LANES), lambda i, *_: (0, 0, 0)),
                  w_spec((D, D_EXPERT)), w_spec((D, D_EXPERT)), w_spec((D_EXPERT, D))],
        out_specs=pl.BlockSpec((BM * SUBLANES, LANES), lambda i, *_: (i, 0)),
        scratch_shapes=[pltpu.VMEM((BM, PACK_ROWS, LANES), jnp.uint32),
                        pltpu.VMEM((BM, PACK_ROWS, LANES), jnp.uint32),
                        pltpu.VMEM((D, D_EXPERT), jnp.bfloat16),
                        pltpu.VMEM((D, D_EXPERT), jnp.bfloat16),
                        pltpu.VMEM((D_EXPERT, D), jnp.bfloat16)],
    )
    return pl.pallas_call(
        _expert_kernel,
        grid_spec=grid_spec,
        out_shape=jax.ShapeDtypeStruct((N_SLOT * SUBLANES, LANES), jnp.float32),
        compiler_params=pltpu.CompilerParams(dimension_semantics=("arbitrary",),
                                             vmem_limit_bytes=EXPERT_VMEM_LIMIT),
        name="expert_mlp",
    )(block_expert, nused, base, limit, order, h_packed, w_gate, w_up, w_down)


def _combine_kernel(last_layer, slot_ref, h_ref, r_ref, ys_hbm, g_ref, b_ref, *rest):
    if last_layer:
        yp_ref, ysamp_ref, ybuf, sem = rest
    else:
        o_ref, ybuf, sem = rest
    i = pl.program_id(0)
    n = pl.num_programs(0)

    def start(step, buf):
        base = step * TB
        _tile_gather_start(ys_hbm, ybuf.at[buf], sem.at[buf],
                           lambda r: slot_ref[(r % TOP_K) * NTOK + base + r // TOP_K], TOP_K * TB,
                           dst_tile=lambda r: (r % TOP_K) * TB + r // TOP_K)

    @pl.when(i == 0)
    def _():
        start(0, 0)

    buf = i % 2
    _tile_gather_wait(ys_hbm, ybuf.at[buf], sem.at[buf], TOP_K * TB)
    yb = ybuf.at[buf]
    y0 = _tile_load(yb.at[pl.ds(0, TB * SUBLANES), :], TB)
    y1 = _tile_load(yb.at[pl.ds(TB * SUBLANES, TB * SUBLANES), :], TB)
    h = h_ref[...]
    r = r_ref[...]

    start(jnp.minimum(i + 1, n - 1), 1 - buf)

    f = r[:, 0:1] * y0 + r[:, 1:2] * y1
    out = _layer_norm(ALPHA * h + f, g_ref[...], b_ref[...])
    if last_layer:
        yp_ref[...] = out

        @pl.when(i == 0)
        def _():
            ysamp_ref[...] = out[N_HEAD:TB, :]
    else:
        o_ref[...] = out

    @pl.when(i == n - 1)
    def _():
        _tile_gather_wait(ys_hbm, ybuf.at[1 - buf], sem.at[1 - buf], TOP_K * TB)


def _combine_layer(last_layer, slot, h_tiles, r_all, ys, ln_g, ln_b):
    if last_layer:
        out_specs = [pl.BlockSpec((TB, D), lambda i, sl: (jnp.maximum(i - 1, 0), 0)),
                     pl.BlockSpec((DEC_BATCH, D), lambda i, sl: (0, 0))]
        out_shape = [jax.ShapeDtypeStruct((BATCH * SEQ, D), jnp.float32),
                     jax.ShapeDtypeStruct((DEC_BATCH, D), jnp.float32)]
    else:
        out_specs = pl.BlockSpec((TB, D), lambda i, sl: (i, 0))
        out_shape = jax.ShapeDtypeStruct((NTOK, D), jnp.float32)
    grid_spec = pltpu.PrefetchScalarGridSpec(
        num_scalar_prefetch=1,
        grid=(N_STEP,),
        in_specs=[pl.BlockSpec((TB, D), lambda i, sl: (i, 0)),
                  pl.BlockSpec((TB, LANES), lambda i, sl: (i, 0)),
                  pl.BlockSpec(memory_space=pl.ANY),
                  pl.BlockSpec((1, D), lambda i, sl: (0, 0)),
                  pl.BlockSpec((1, D), lambda i, sl: (0, 0))],
        out_specs=out_specs,
        scratch_shapes=[pltpu.VMEM((2, TOP_K * TB * SUBLANES, LANES), jnp.float32),
                        pltpu.SemaphoreType.DMA((2,))],
    )
    return pl.pallas_call(
        functools.partial(_combine_kernel, last_layer),
        grid_spec=grid_spec,
        out_shape=out_shape,
        compiler_params=pltpu.CompilerParams(dimension_semantics=("arbitrary",),
                                             vmem_limit_bytes=VMEM_LIMIT),
        name="moe_combine",
    )(slot, h_tiles, r_all, ys, ln_g, ln_b)


def _routing_metadata(rt, cnt):
    experts = jnp.arange(N_EXPERTS, dtype=jnp.int32)
    counts = cnt[0, :N_EXPERTS].astype(jnp.int32)
    nblk = (counts + BM - 1) // BM
    blk_end = jnp.cumsum(nblk)
    pstart = (blk_end - nblk) * BM
    starts = jnp.cumsum(counts) - counts
    nused = blk_end[-1]
    blk = jnp.arange(NB, dtype=jnp.int32)
    block_expert = jnp.sum((blk[:, None] >= blk_end[None, :]).astype(jnp.int32), axis=1)
    last_expert = jnp.max(jnp.where(nblk > 0, experts, 0))
    block_expert = jnp.where(blk < nused, block_expert, last_expert).astype(jnp.int32)
    of_block = block_expert[:, None] == experts[None, :]
    base = blk * BM + jnp.sum(jnp.where(of_block, (starts - pstart)[None, :], 0), axis=1)
    limit = jnp.sum(jnp.where(of_block, (starts + counts - 1)[None, :], 0), axis=1)
    rows = rt.reshape(N_STEP, SUBLANES, TB)
    expert = jnp.transpose(rows[:, 2:2 + TOP_K, :], (1, 0, 2)).reshape(N_ASSIGN).astype(jnp.int32)
    rank = jnp.transpose(rows[:, 2 + TOP_K:2 + 2 * TOP_K, :], (1, 0, 2)).reshape(N_ASSIGN).astype(jnp.int32)
    slot = rank + jnp.sum(jnp.where(expert[:, None] == experts[None, :], pstart[None, :], 0), axis=1)
    order = jnp.argsort(expert * RANK_RANGE + rank).astype(jnp.int32)
    return (block_expert, nused.reshape(1).astype(jnp.int32), base.astype(jnp.int32),
            limit.astype(jnp.int32), order, slot.astype(jnp.int32))


def kernel(x_prompt, x_sample, state_pool, state_conv, meta, w_pool, pool_scale, w_glu, b_glu, w_dw, b_dw, conv_ln_g, conv_ln_b, w_pw, b_pw, ln_mix_g, ln_mix_b, ln_ffn_g, ln_ffn_b, w_router_group, w_router_expert, w_gate, w_up, w_down):
    f32 = jnp.float32
    x_samp2d = x_sample.reshape(DEC_BATCH, D)
    head = jnp.concatenate([jnp.tile(meta.astype(f32), (BATCH, 1)), x_samp2d], axis=0)
    k_idx = jnp.arange(POOL_STATE)[:, None]
    win = jnp.repeat(jnp.asarray(POOL_WINDOWS), POOL_GROUP_DIM)[None, :]
    pool_coef = (k_idx >= (POOL_STATE + 1 - win)).astype(f32)

    new_pool_p, new_pool_s, new_conv_p, new_conv_s = [], [], [], []
    for i in range(DEPTH):
        j = i // 2
        wr32 = jnp.concatenate([w_router_group[i], w_router_expert[i],
                                jnp.zeros((D, LANES - N_GROUPS - N_EXPERTS), f32)], axis=1)
        wr_hi = wr32.astype(jnp.bfloat16)
        wr = jnp.concatenate([wr_hi, (wr32 - wr_hi.astype(f32)).astype(jnp.bfloat16)], axis=1)
        ln_g, ln_b = ln_mix_g[i].reshape(1, D), ln_mix_b[i].reshape(1, D)
        if i == 0:
            inputs = dict(first_inputs=(head, x_prompt.reshape(BATCH * SEQ, D)))
        else:
            inputs = dict(fused_inputs=(slot, h1, r_all, ys, ln_ffn_g[i - 1].reshape(1, D),
                                        ln_ffn_b[i - 1].reshape(1, D)))
        if i % 2 == 0:
            ps = _state_presum(j, state_pool, pool_coef)
            outs = _pool_layer(ps, w_pool[j].astype(jnp.bfloat16), pool_scale[j].reshape(1, D),
                               ln_g, ln_b, wr, **inputs)
            if i == 0:
                tail, x_s = x_prompt[:, SEQ - POOL_STATE:], x_samp2d
            else:
                x_s, tail16, outs = outs[0], outs[1], outs[2:]
                tail = tail16[:, POOL_CARRY - POOL_STATE:]
            h1, r_all, cnt, h1p, rt = outs
            new_pool_p.append(tail)
            new_pool_s.append(jnp.concatenate([state_pool[j][:, 1:], x_s[:, None, :]], axis=1))
        else:
            vs = _state_presum(j, state_conv, w_dw[j][:CONV_STATE])
            outs = _conv_layer(
                vs, w_glu[j].astype(jnp.bfloat16), b_glu[j].reshape(1, 2 * D), w_dw[j],
                b_dw[j].reshape(1, D), conv_ln_g[j].reshape(1, D), conv_ln_b[j].reshape(1, D),
                w_pw[j].astype(jnp.bfloat16), b_pw[j].reshape(1, D), ln_g, ln_b, wr, **inputs)
            h1, r_all, cnt, h1p, rt, ust, u_s = outs if i == 0 else outs[2:]
            new_conv_p.append(ust)
            new_conv_s.append(jnp.concatenate([state_conv[j][:, 1:], u_s[:, None, :]], axis=1))
        block_expert, nused, base, limit, order, slot = _routing_metadata(rt, cnt)
        ys = _expert_layer(i, block_expert, nused, base, limit, order, h1p, w_gate, w_up, w_down)

    y_prompt, y_samp = _combine_layer(True, slot, h1, r_all, ys, ln_ffn_g[DEPTH - 1].reshape(1, D),
                                      ln_ffn_b[DEPTH - 1].reshape(1, D))
    return (y_prompt.reshape(BATCH, SEQ, D), y_samp.reshape(DEC_BATCH, 1, D),
            jnp.stack(new_pool_p), jnp.stack(new_pool_s), jnp.stack(new_conv_p), jnp.stack(new_conv_s))
```

```python
import functools

import jax
import jax.numpy as jnp
from jax import lax
from jax.experimental import pallas as pl
from jax.experimental.pallas import tpu as pltpu

D = 1024
BATCH = 8
SEQ = 2048
DEPTH = 4
DEC_BATCH = 128
N_META = 16
POOL_WINDOWS = (2, 4, 8, 16)
POOL_GROUP_DIM = D // len(POOL_WINDOWS)
POOL_STATE = max(POOL_WINDOWS) - 1
CONV_WIDTH = 31
CONV_STATE = CONV_WIDTH - 1
N_GROUPS = 4
EXPERTS_PER_GROUP = 8
N_EXPERTS = N_GROUPS * EXPERTS_PER_GROUP
TOP_K = 2
D_EXPERT = D // 2
ALPHA = (2.0 * DEPTH) ** 0.25
LN_EPS = 1e-5

LANES = 128
SUBLANES = 8
N_HEAD = BATCH * N_META
TB = 256
NTOK = N_HEAD + DEC_BATCH + BATCH * SEQ
N_STEP = NTOK // TB
CHUNKS = SEQ // TB
POOL_CARRY = 16
CONV_CARRY = 32
CONV_RB = 128
CONV_CB = 128
BM = 512
N_ASSIGN = NTOK * TOP_K
RANK_RANGE = 1 << 16
NB = N_ASSIGN // BM + N_EXPERTS
N_SLOT = NB * BM
VMEM_LIMIT = 48 * 1024 * 1024
EXPERT_VMEM_LIMIT = 56 * 1024 * 1024
NEG = -1e30

assert D == SUBLANES * LANES and N_HEAD + DEC_BATCH == TB and SEQ % TB == 0
assert N_ASSIGN % BM == 0 and TB % CONV_RB == 0 and N_META == POOL_CARRY and CONV_STATE <= TB


def _tile_load(ref, n):
    return jnp.concatenate([ref[pl.ds(s, n, stride=SUBLANES), :] for s in range(SUBLANES)], axis=1)


def _tile_store(ref, val, n):
    for s in range(SUBLANES):
        ref[pl.ds(s, n, stride=SUBLANES), :] = val[:, s * LANES:(s + 1) * LANES]


def _layer_norm(x, g, b):
    mu = jnp.mean(x, axis=-1, keepdims=True)
    xc = x - mu
    var = jnp.mean(xc * xc, axis=-1, keepdims=True)
    return xc * lax.rsqrt(var + LN_EPS) * g + b


PACK_ROWS = D // 2 // LANES


def _aligned(v, m):
    return v if isinstance(v, int) else pl.multiple_of(v, m)


def _pack_bf16(h_bf16, hp_ref, rows):
    bits = pltpu.bitcast(h_bf16.astype(jnp.float32), jnp.uint32)
    words = jnp.bitwise_or(lax.shift_right_logical(bits[:, :D // 2], jnp.uint32(16)),
                           jnp.bitwise_and(bits[:, D // 2:], jnp.uint32(0xFFFF0000)))
    for s in range(PACK_ROWS):
        hp_ref[rows, s, :] = words[:, s * LANES:(s + 1) * LANES]


def _unpack_bf16(words):
    lo = pltpu.bitcast(lax.shift_left(words, jnp.uint32(16)), jnp.float32)
    hi = pltpu.bitcast(jnp.bitwise_and(words, jnp.uint32(0xFFFF0000)), jnp.float32)
    return jnp.concatenate([lo, hi], axis=1).astype(jnp.bfloat16)


HB = 128
N_TILE = TB // HB

assert TB % HB == 0 and HB % LANES == 0


def _tile_rows(tile):
    return pl.ds(_aligned(tile * HB, HB), HB)


def _route_init(cnt_ref, ltri_ref):
    @pl.when(pl.program_id(0) == 0)
    def _():
        cnt_ref[...] = jnp.zeros((1, LANES), jnp.float32)
        row = lax.broadcasted_iota(jnp.int32, (HB, HB), 0)
        col = lax.broadcasted_iota(jnp.int32, (HB, HB), 1)
        ltri_ref[...] = jnp.where(col < row, 1.0, 0.0).astype(jnp.bfloat16)


def _flush_rt(rt_scr, rt_ref):
    for t in range(N_TILE):
        rt_ref[:, t * HB:(t + 1) * HB] = rt_scr[t]


def _residual_route(x, m, tile, g_ref, b_ref, wr_ref, h_ref, r_ref, cnt_out_ref, hp_ref, rt_scr, cnt_ref,
                    ltri_ref):
    rows = _tile_rows(tile)
    h = _layer_norm(ALPHA * x + m, g_ref[...], b_ref[...])
    h_ref[rows, :] = h
    h_hi = h.astype(jnp.bfloat16)
    _pack_bf16(h_hi, hp_ref, rows)
    h_lo = (h - h_hi.astype(jnp.float32)).astype(jnp.bfloat16)
    hi_both = jnp.dot(h_hi, wr_ref[...], preferred_element_type=jnp.float32)
    lo_hi = jnp.dot(h_lo, wr_ref[:, 0:LANES], preferred_element_type=jnp.float32)
    logits = hi_both[:, 0:LANES] + (hi_both[:, LANES:2 * LANES] + lo_hi)
    lane = lax.broadcasted_iota(jnp.int32, logits.shape, 1)
    lanef = lane.astype(jnp.float32)
    big = jnp.float32(1e9)
    lg = jnp.where(lane < N_GROUPS, logits, NEG)
    mg = jnp.max(lg, axis=1, keepdims=True)
    gidx = jnp.min(jnp.where(lg == mg, lanef, big), axis=1, keepdims=True)
    p_grp = 1.0 / jnp.sum(jnp.where(lane < N_GROUPS, jnp.exp(lg - mg), 0.0), axis=1, keepdims=True)
    lo = N_GROUPS + EXPERTS_PER_GROUP * gidx
    le = jnp.where(lanef >= lo, jnp.where(lanef < lo + EXPERTS_PER_GROUP, logits, NEG), NEG)
    m1 = jnp.max(le, axis=1, keepdims=True)
    i1 = jnp.min(jnp.where(le == m1, lanef, big), axis=1, keepdims=True)
    le2 = jnp.where(lanef == i1, NEG, le)
    m2 = jnp.max(le2, axis=1, keepdims=True)
    i2 = jnp.min(jnp.where(le2 == m2, lanef, big), axis=1, keepdims=True)
    ratio = jnp.exp(m2 - m1)
    g1 = p_grp / (1.0 + ratio)
    g2 = g1 * ratio
    e1 = i1 - N_GROUPS
    e2 = i2 - N_GROUPS

    is1 = lanef == e1
    is2 = lanef == e2
    onehot = jnp.where(is1, 1.0, jnp.where(is2, 1.0, 0.0))
    before = jnp.dot(ltri_ref[...], onehot.astype(jnp.bfloat16),
                     preferred_element_type=jnp.float32) + cnt_ref[...]
    rank1 = jnp.sum(jnp.where(is1, before, 0.0), axis=1, keepdims=True)
    rank2 = jnp.sum(jnp.where(is2, before, 0.0), axis=1, keepdims=True)
    cnt = cnt_ref[...] + jnp.sum(onehot, axis=0, keepdims=True)
    cnt_ref[...] = cnt
    cnt_out_ref[...] = cnt

    packed = jnp.where(lane == 0, g1,
                       jnp.where(lane == 1, g2,
                                 jnp.where(lane == 2, e1,
                                           jnp.where(lane == 3, e2,
                                                     jnp.where(lane == 4, rank1,
                                                               jnp.where(lane == 5, rank2, 0.0))))))
    r_ref[rows, :] = packed
    rt_scr[tile] = jnp.transpose(packed)[0:SUBLANES, :]


def _seq_of_step(s):
    sm1 = jnp.maximum(s - 1, 0)
    return lax.shift_right_logical(sm1, CHUNKS.bit_length() - 1), jnp.bitwise_and(sm1, CHUNKS - 1)


assert CHUNKS & (CHUNKS - 1) == 0


PRESUM_ROWS = 32


def _presum_kernel(st_ref, coef_ref, o_ref):
    coef = coef_ref[...]
    for n in range(PRESUM_ROWS):
        o_ref[n:n + 1, :] = jnp.sum(st_ref[0, n] * coef, axis=0, keepdims=True)


def _state_presum(j, state, coef):
    _, n, k, _ = state.shape
    return pl.pallas_call(
        _presum_kernel,
        grid=(n // PRESUM_ROWS,),
        in_specs=[pl.BlockSpec((1, PRESUM_ROWS, k, D), lambda i: (j, i, 0, 0)),
                  pl.BlockSpec((k, D), lambda i: (0, 0))],
        out_specs=pl.BlockSpec((PRESUM_ROWS, D), lambda i: (i, 0)),
        out_shape=jax.ShapeDtypeStruct((n, D), jnp.float32),
        compiler_params=pltpu.CompilerParams(dimension_semantics=("arbitrary",),
                                             vmem_limit_bytes=VMEM_LIMIT),
        name="state_presum",
    )(state, coef)


def _pool_kernel(first_layer, head_ref, x_ref, ps_ref, wp_ref, sc_ref, g_ref, b_ref, wr_ref,
                 h_ref, r_ref, cnt_out_ref, hp_ref, rt_ref, ext_ref, m_ref, carry_ref, cnt_ref, ltri_ref, rt_scr,
                 start_next=lambda tile: None):
    s = pl.program_id(0)
    seq, chunk = _seq_of_step(s)
    _route_init(cnt_ref, ltri_ref)
    route_refs = (g_ref, b_ref, wr_ref, h_ref, r_ref, cnt_out_ref, hp_ref, rt_scr, cnt_ref, ltri_ref)

    @pl.when(s == 0)
    def _():
        x = head_ref[...] if first_layer else x_ref[...]
        for t in range(N_TILE):
            start_next(t)
        ext_ref[0:POOL_CARRY, :] = jnp.zeros((POOL_CARRY, D), jnp.float32)
        ext_ref[POOL_CARRY:POOL_CARRY + TB, :] = x
        carry_ref[...] = x[0:N_HEAD, :]
        pos = jnp.bitwise_and(lax.broadcasted_iota(jnp.int32, (N_HEAD, 1), 0), N_META - 1)
        for g, w in enumerate(POOL_WINDOWS):
            lo, hi = g * POOL_GROUP_DIM, (g + 1) * POOL_GROUP_DIM
            xm = x[0:N_HEAD, lo:hi]
            acc = xm
            for k in range(1, w):
                acc = acc + jnp.where(pos >= k, ext_ref[POOL_CARRY - k:POOL_CARRY - k + N_HEAD, lo:hi], 0.0)
            cnt = jnp.minimum(pos + 1, w).astype(jnp.float32)
            d_meta = acc / cnt - xm
            xs = x[N_HEAD:TB, lo:hi]
            d_samp = (ps_ref[:, lo:hi] + xs) / float(w) - xs
            diff = jnp.concatenate([d_meta, d_samp], axis=0)
            y = jnp.dot(diff.astype(jnp.bfloat16), wp_ref[g], preferred_element_type=jnp.float32)
            m_ref[:, lo:hi] = y * sc_ref[:, lo:hi]
        for t in range(N_TILE):
            _residual_route(x[t * HB:(t + 1) * HB], m_ref[t * HB:(t + 1) * HB, :], t, *route_refs)
        _flush_rt(rt_scr, rt_ref)

    @pl.when(s > 0)
    def _():
        @pl.when(chunk == 0)
        def _():
            ext_ref[0:POOL_CARRY, :] = carry_ref[pl.ds(pl.multiple_of(seq * N_META, N_META), N_META), :]

        ext_ref[POOL_CARRY:POOL_CARRY + TB, :] = x_ref[...]

        def tile_body(t, carry):
            start_next(t)
            sub = ext_ref.at[pl.ds(pl.multiple_of(t * HB, HB), HB + POOL_CARRY), :]
            x = sub[POOL_CARRY:POOL_CARRY + HB, :]
            ms = []
            for g, w in enumerate(POOL_WINDOWS):
                lo, hi = g * POOL_GROUP_DIM, (g + 1) * POOL_GROUP_DIM
                xg = x[:, lo:hi]
                acc = xg
                for k in range(1, w):
                    acc = acc + sub[POOL_CARRY - k:POOL_CARRY - k + HB, lo:hi]
                diff = acc / float(w) - xg
                y = jnp.dot(diff.astype(jnp.bfloat16), wp_ref[g], preferred_element_type=jnp.float32)
                ms.append(y * sc_ref[:, lo:hi])
            _residual_route(x, jnp.concatenate(ms, axis=1), t, *route_refs)
            return carry
        lax.fori_loop(0, N_TILE, tile_body, 0)
        ext_ref[0:POOL_CARRY, :] = ext_ref[TB:TB + POOL_CARRY, :]
        _flush_rt(rt_scr, rt_ref)


assert N_META >= max(POOL_WINDOWS)


def _const_spec(shape):
    nd = len(shape)
    return pl.BlockSpec(shape, lambda s, *_: (0,) * nd)


def _x_spec(first_layer):
    if first_layer:
        return pl.BlockSpec((TB, D), lambda s, *_: (jnp.maximum(s - 1, 0), 0))
    return pl.BlockSpec((TB, D), lambda s, *_: (s, 0))


_MIXER_OUT_SPECS = [pl.BlockSpec((TB, D), lambda s, *_: (s, 0)),
                    pl.BlockSpec((TB, LANES), lambda s, *_: (s, 0)),
                    pl.BlockSpec((1, LANES), lambda s, *_: (0, 0)),
                    pl.BlockSpec((TB, PACK_ROWS, LANES), lambda s, *_: (s, 0, 0)),
                    pl.BlockSpec((SUBLANES, TB), lambda s, *_: (s, 0))]
_MIXER_OUT_SHAPES = [jax.ShapeDtypeStruct((NTOK, D), jnp.float32),
                     jax.ShapeDtypeStruct((NTOK, LANES), jnp.float32),
                     jax.ShapeDtypeStruct((1, LANES), jnp.float32),
                     jax.ShapeDtypeStruct((NTOK, PACK_ROWS, LANES), jnp.uint32),
                     jax.ShapeDtypeStruct((N_STEP * SUBLANES, TB), jnp.float32)]


def _fused_mixer_kernel(body, n_in, slot_ref, hprev_ref, rprev_ref, ys_hbm, fg_ref, fb_ref, *rest):
    mixer_in = rest[:n_in]
    xs_out_ref, tail_ref = rest[n_in:n_in + 2]
    x_scr, ybuf, sem = rest[-3:]
    i = pl.program_id(0)
    n = pl.num_programs(0)

    def start_tile(step, buf, tile):
        base = step * TB + tile * HB
        _tile_gather_start(ys_hbm, ybuf.at[buf], sem.at[buf],
                           lambda r: slot_ref[(r % TOP_K) * NTOK + base + r // TOP_K], TOP_K * HB,
                           dst_tile=lambda r: (r % TOP_K) * TB + tile * HB + r // TOP_K)

    @pl.when(i == 0)
    def _():
        for t in range(N_TILE):
            start_tile(0, 0, t)

    buf = i % 2
    _tile_gather_wait(ys_hbm, ybuf.at[buf], sem.at[buf], TOP_K * TB)
    yb = ybuf.at[buf]
    y0 = _tile_load(yb.at[pl.ds(0, TB * SUBLANES), :], TB)
    y1 = _tile_load(yb.at[pl.ds(TB * SUBLANES, TB * SUBLANES), :], TB)
    r = rprev_ref[...]
    x = _layer_norm(ALPHA * hprev_ref[...] + (r[:, 0:1] * y0 + r[:, 1:2] * y1), fg_ref[...], fb_ref[...])
    x_scr[...] = x

    @pl.when(i == 0)
    def _():
        xs_out_ref[...] = x[N_HEAD:TB, :]

    _, chunk = _seq_of_step(i)

    @pl.when(jnp.logical_and(i > 0, chunk == CHUNKS - 1))
    def _():
        tail_ref[0] = x[TB - POOL_CARRY:TB, :]

    body(False, None, x_scr, *mixer_in, *rest[n_in + 2:-3],
         start_next=lambda tile: start_tile(jnp.minimum(i + 1, n - 1), 1 - buf, tile))

    @pl.when(i == n - 1)
    def _():
        _tile_gather_wait(ys_hbm, ybuf.at[1 - buf], sem.at[1 - buf], TOP_K * TB)


def _mixer_call(name, body, mixer_in_specs, extra_out_specs, extra_out_shapes, scratch_shapes, mixer_inputs,
                first_inputs=None, fused_inputs=None):
    out_specs = _MIXER_OUT_SPECS + extra_out_specs
    out_shape = _MIXER_OUT_SHAPES + extra_out_shapes
    params = pltpu.CompilerParams(dimension_semantics=("arbitrary",), vmem_limit_bytes=VMEM_LIMIT)
    if fused_inputs is None:
        return pl.pallas_call(
            functools.partial(body, True),
            grid=(N_STEP,),
            in_specs=[_const_spec((TB, D)), _x_spec(True)] + mixer_in_specs,
            out_specs=out_specs, out_shape=out_shape, scratch_shapes=scratch_shapes,
            compiler_params=params, name=name,
        )(*first_inputs, *mixer_inputs)
    seq_block = lambda s, *_: (jnp.maximum(s - 1, 0) // CHUNKS, 0, 0)
    grid_spec = pltpu.PrefetchScalarGridSpec(
        num_scalar_prefetch=1,
        grid=(N_STEP,),
        in_specs=[pl.BlockSpec((TB, D), lambda s, *_: (s, 0)),
                  pl.BlockSpec((TB, LANES), lambda s, *_: (s, 0)),
                  pl.BlockSpec(memory_space=pl.ANY),
                  _const_spec((1, D)), _const_spec((1, D))] + mixer_in_specs,
        out_specs=[_const_spec((DEC_BATCH, D)), pl.BlockSpec((1, POOL_CARRY, D), seq_block)] + out_specs,
        scratch_shapes=scratch_shapes + [pltpu.VMEM((TB, D), jnp.float32),
                                         pltpu.VMEM((2, TOP_K * TB * SUBLANES, LANES), jnp.float32),
                                         pltpu.SemaphoreType.DMA((2,))],
    )
    return pl.pallas_call(
        functools.partial(_fused_mixer_kernel, body, len(mixer_in_specs)),
        grid_spec=grid_spec,
        out_shape=[jax.ShapeDtypeStruct((DEC_BATCH, D), jnp.float32),
                   jax.ShapeDtypeStruct((BATCH, POOL_CARRY, D), jnp.float32)] + out_shape,
        compiler_params=params, name=name + "_fused",
    )(*fused_inputs, *mixer_inputs)


def _pool_layer(ps, wp_bf, scale, ln_g, ln_b, wr, **inputs):
    return _mixer_call(
        "pool_mixer", _pool_kernel,
        [_const_spec((DEC_BATCH, D)),
         _const_spec((len(POOL_WINDOWS), POOL_GROUP_DIM, POOL_GROUP_DIM)),
         _const_spec((1, D)), _const_spec((1, D)), _const_spec((1, D)),
         _const_spec((D, 2 * LANES))],
        [], [],
        [pltpu.VMEM((POOL_CARRY + TB, D), jnp.float32),
         pltpu.VMEM((TB, D), jnp.float32),
         pltpu.VMEM((N_HEAD, D), jnp.float32),
         pltpu.VMEM((1, LANES), jnp.float32),
         pltpu.VMEM((HB, HB), jnp.bfloat16),
         pltpu.VMEM((N_TILE, SUBLANES, HB), jnp.float32)],
        (ps, wp_bf, scale, ln_g, ln_b, wr), **inputs)


def _depthwise_conv(ext_ref, wdw_ref, v_ref):
    base = CONV_CARRY - CONV_STATE
    for cb in range(D // CONV_CB):
        lo, hi = cb * CONV_CB, (cb + 1) * CONV_CB

        def body(i, carry, lo=lo, hi=hi):
            r0 = pl.multiple_of(i * CONV_RB, CONV_RB)
            sub = ext_ref.at[pl.ds(r0, CONV_RB + CONV_CARRY), :]
            v = None
            for r in range(SUBLANES):
                qs = [q for q in range((base + CONV_WIDTH) // SUBLANES + 1)
                      if 0 <= SUBLANES * q + r - base < CONV_WIDTH]
                z0 = SUBLANES * qs[0] + r
                z = sub[z0:SUBLANES * qs[-1] + r + CONV_RB, lo:hi]
                p = None
                for q in qs:
                    k = SUBLANES * q + r - base
                    off = SUBLANES * (q - qs[0])
                    term = wdw_ref[k:k + 1, lo:hi] * z[off:off + CONV_RB]
                    p = term if p is None else p + term
                v = p if v is None else v + p
            v_ref[pl.ds(r0, CONV_RB), lo:hi] = v
            return carry
        lax.fori_loop(0, TB // CONV_RB, body, 0)


assert TB % CONV_RB == 0 and N_HEAD == HB and DEC_BATCH == HB


def _conv_kernel(first_layer, head_ref, x_ref, vs_ref, wglu_ref, bglu_ref, wdw_ref, bdw_ref, lg_ref, lb_ref,
                 wpw_ref, bpw_ref, g_ref, b_ref, wr_ref, h_ref, r_ref, cnt_out_ref, hp_ref, rt_ref, ust_ref, us_ref,
                 ext_ref, v_ref, carry_ref, cnt_ref, ltri_ref, rt_scr, start_next=lambda tile: None):
    s = pl.program_id(0)
    seq, chunk = _seq_of_step(s)
    _route_init(cnt_ref, ltri_ref)
    route_refs = (g_ref, b_ref, wr_ref, h_ref, r_ref, cnt_out_ref, hp_ref, rt_scr, cnt_ref, ltri_ref)

    def glu(x):
        hh = jnp.dot(x.astype(jnp.bfloat16), wglu_ref[...], preferred_element_type=jnp.float32) + bglu_ref[...]
        return hh[:, :D] * jax.nn.sigmoid(hh[:, D:])

    def finish(x, v, tile):
        v = _layer_norm(v + bdw_ref[...], lg_ref[...], lb_ref[...])
        v = v * jax.nn.sigmoid(v)
        m = jnp.dot(v.astype(jnp.bfloat16), wpw_ref[...], preferred_element_type=jnp.float32) + bpw_ref[...]
        _residual_route(x, m, tile, *route_refs)

    @pl.when(s == 0)
    def _():
        x = head_ref[...] if first_layer else x_ref[...]
        for t in range(N_TILE):
            start_next(t)
        u = glu(x)
        um = u[0:N_HEAD, :]
        us = u[N_HEAD:TB, :]
        carry_ref[...] = um
        us_ref[...] = us
        ext_ref[0:N_META, :] = jnp.zeros((N_META, D), jnp.float32)
        ext_ref[N_META:N_META + N_HEAD, :] = um
        pos = jnp.bitwise_and(lax.broadcasted_iota(jnp.int32, (N_HEAD, 1), 0), N_META - 1)
        acc = wdw_ref[CONV_WIDTH - 1:CONV_WIDTH, :] * um
        for d in range(1, N_META):
            k = CONV_WIDTH - 1 - d
            acc = acc + wdw_ref[k:k + 1, :] * jnp.where(pos >= d, ext_ref[N_META - d:N_META - d + N_HEAD, :], 0.0)
        v_samp = vs_ref[...] + wdw_ref[CONV_WIDTH - 1:CONV_WIDTH, :] * us
        finish(x[0:N_HEAD], acc, 0)
        finish(x[N_HEAD:TB], v_samp, 1)
        _flush_rt(rt_scr, rt_ref)

    @pl.when(s > 0)
    def _():
        @pl.when(chunk == 0)
        def _():
            ext_ref[0:CONV_CARRY - N_META, :] = jnp.zeros((CONV_CARRY - N_META, D), jnp.float32)
            ext_ref[CONV_CARRY - N_META:CONV_CARRY, :] = carry_ref[
                pl.ds(pl.multiple_of(seq * N_META, N_META), N_META), :]

        def glu_tile(t, carry):
            start_next(t)
            r0 = pl.multiple_of(t * HB, HB)
            ext_ref[pl.ds(CONV_CARRY + r0, HB), :] = glu(x_ref[pl.ds(r0, HB), :])
            return carry
        lax.fori_loop(0, N_TILE, glu_tile, 0)

        _depthwise_conv(ext_ref, wdw_ref, v_ref)

        def finish_tile(t, carry):
            rows = _tile_rows(t)
            finish(x_ref[rows, :], v_ref[rows, :], t)
            return carry
        lax.fori_loop(0, N_TILE, finish_tile, 0)
        ext_ref[0:CONV_CARRY, :] = ext_ref[TB:TB + CONV_CARRY, :]
        _flush_rt(rt_scr, rt_ref)

        @pl.when(chunk == CHUNKS - 1)
        def _():
            ust_ref[0] = ext_ref[CONV_CARRY + TB - CONV_STATE:CONV_CARRY + TB, :]


assert CONV_CARRY - N_META + N_META >= CONV_STATE and N_META <= CONV_STATE


def _conv_layer(vs, wglu_bf, bglu, wdw, bdw, cln_g, cln_b, wpw_bf, bpw, ln_g, ln_b, wr, **inputs):
    return _mixer_call(
        "conv_mixer", _conv_kernel,
        [_const_spec((DEC_BATCH, D)),
         _const_spec((D, 2 * D)), _const_spec((1, 2 * D)),
         _const_spec((CONV_WIDTH, D)), _const_spec((1, D)),
         _const_spec((1, D)), _const_spec((1, D)),
         _const_spec((D, D)), _const_spec((1, D)),
         _const_spec((1, D)), _const_spec((1, D)),
         _const_spec((D, 2 * LANES))],
        [pl.BlockSpec((1, CONV_STATE, D), lambda s, *_: (jnp.maximum(s - 1, 0) // CHUNKS, 0, 0)),
         _const_spec((DEC_BATCH, D))],
        [jax.ShapeDtypeStruct((BATCH, CONV_STATE, D), jnp.float32),
         jax.ShapeDtypeStruct((DEC_BATCH, D), jnp.float32)],
        [pltpu.VMEM((CONV_CARRY + TB, D), jnp.float32),
         pltpu.VMEM((TB, D), jnp.float32),
         pltpu.VMEM((N_HEAD, D), jnp.float32),
         pltpu.VMEM((1, LANES), jnp.float32),
         pltpu.VMEM((HB, HB), jnp.bfloat16),
         pltpu.VMEM((N_TILE, SUBLANES, HB), jnp.float32)],
        (vs, wglu_bf, bglu, wdw, bdw, cln_g, cln_b, wpw_bf, bpw, ln_g, ln_b, wr), **inputs)


def _tile_gather_start(src_hbm, dst, sem, row_index, n_rows, dst_tile=lambda r: r):
    for r in range(n_rows):
        tok = row_index(r)
        pltpu.make_async_copy(src_hbm.at[pl.ds(pl.multiple_of(tok * SUBLANES, SUBLANES), SUBLANES), :],
                              dst.at[pl.ds(_aligned(dst_tile(r) * SUBLANES, SUBLANES), SUBLANES), :],
                              sem).start(priority=r % 2)


def _tile_gather_wait(src_hbm, dst, sem, n_rows):
    pltpu.make_async_copy(src_hbm.at[pl.ds(0, n_rows * SUBLANES), :], dst, sem).wait()


assert TOP_K == 2


def _expert_kernel(be_ref, nused_ref, base_ref, limit_ref, order_ref, hp_ref, wg_ref, wu_ref, wd_ref, ys_ref,
                   xbuf_even, xbuf_odd, wgb, wub, wdb):
    i = pl.program_id(0)
    nused = nused_ref[0]

    def gather(blk, xbuf):
        b0 = base_ref[blk]
        lim = limit_ref[blk]
        for r in range(BM):
            a = order_ref[jnp.minimum(b0 + r, lim)]
            xbuf[r] = hp_ref[jnp.where(a >= NTOK, a - NTOK, a)]

    def mlp_rows(cur, h0, n):
        words = jnp.concatenate([cur[h0:h0 + n, s, :] for s in range(PACK_ROWS)], axis=1)
        xb = _unpack_bf16(words)
        gate = jnp.dot(xb, wgb[...], preferred_element_type=jnp.float32)
        up = jnp.dot(xb, wub[...], preferred_element_type=jnp.float32)
        hid = gate * jax.nn.sigmoid(gate) * up
        y = jnp.dot(hid.astype(jnp.bfloat16), wdb[...], preferred_element_type=jnp.float32)
        _tile_store(ys_ref.at[pl.ds(h0 * SUBLANES, n * SUBLANES), :], y, n)

    def block(cur, nxt):
        gather(jnp.minimum(i + 1, jnp.maximum(nused - 1, 0)), nxt)
        mlp_rows(cur, 0, BM)

    @pl.when(i == 0)
    def _():
        gather(0, xbuf_even)

    @pl.when(jnp.logical_and(i < nused, jnp.logical_or(i == 0, be_ref[i] != be_ref[jnp.maximum(i - 1, 0)])))
    def _():
        wgb[...] = wg_ref[0, 0].astype(jnp.bfloat16)
        wub[...] = wu_ref[0, 0].astype(jnp.bfloat16)
        wdb[...] = wd_ref[0, 0].astype(jnp.bfloat16)

    odd = jnp.bitwise_and(i, 1) == 1

    @pl.when(jnp.logical_and(i < nused, jnp.logical_not(odd)))
    def _():
        block(xbuf_even, xbuf_odd)

    @pl.when(jnp.logical_and(i < nused, odd))
    def _():
        block(xbuf_odd, xbuf_even)

    @pl.when(i >= nused)
    def _():
        ys_ref[...] = jnp.zeros((BM * SUBLANES, LANES), jnp.float32)


def _expert_layer(layer, block_expert, nused, base, limit, order, h_packed, w_gate, w_up, w_down):
    def w_spec(shape):
        return pl.BlockSpec((1, 1) + shape, lambda i, be, *_: (layer, be[i], 0, 0))
    grid_spec = pltpu.PrefetchScalarGridSpec(
        num_scalar_prefetch=5,
        grid=(NB,),
        in_specs=[pl.BlockSpec((NTOK, PACK_ROWS, LANES), lambda i, *_: (0, 0, 0)),
                  w_spec((D, D_EXPERT)), w_spec((D, D_EXPERT)), w_spec((D_EXPERT, D))],
        out_specs=pl.BlockSpec((BM * SUBLANES, LANES), lambda i, *_: (i, 0)),
        scratch_shapes=[pltpu.VMEM((BM, PACK_ROWS, LANES), jnp.uint32),
                        pltpu.VMEM((BM, PACK_ROWS, LANES), jnp.uint32),
                        pltpu.VMEM((D, D_EXPERT), jnp.bfloat16),
                        pltpu.VMEM((D, D_EXPERT), jnp.bfloat16),
                        pltpu.VMEM((D_EXPERT, D), jnp.bfloat16)],
    )
    return pl.pallas_call(
        _expert_kernel,
        grid_spec=grid_spec,
        out_shape=jax.ShapeDtypeStruct((N_SLOT * SUBLANES, LANES), jnp.float32),
        compiler_params=pltpu.CompilerParams(dimension_semantics=("arbitrary",),
                                             vmem_limit_bytes=EXPERT_VMEM_LIMIT),
        name="expert_mlp",
    )(block_expert, nused, base, limit, order, h_packed, w_gate, w_up, w_down)


def _combine_kernel(last_layer, slot_ref, h_ref, r_ref, ys_hbm, g_ref, b_ref, *rest):
    if last_layer:
        yp_ref, ysamp_ref, ybuf, sem = rest
    else:
        o_ref, ybuf, sem = rest
    i = pl.program_id(0)
    n = pl.num_programs(0)

    def start(step, buf):
        base = step * TB
        _tile_gather_start(ys_hbm, ybuf.at[buf], sem.at[buf],
                           lambda r: slot_ref[(r % TOP_K) * NTOK + base + r // TOP_K], TOP_K * TB,
                           dst_tile=lambda r: (r % TOP_K) * TB + r // TOP_K)

    @pl.when(i == 0)
    def _():
        start(0, 0)

    buf = i % 2
    _tile_gather_wait(ys_hbm, ybuf.at[buf], sem.at[buf], TOP_K * TB)
    yb = ybuf.at[buf]
    y0 = _tile_load(yb.at[pl.ds(0, TB * SUBLANES), :], TB)
    y1 = _tile_load(yb.at[pl.ds(TB * SUBLANES, TB * SUBLANES), :], TB)
    h = h_ref[...]
    r = r_ref[...]

    start(jnp.minimum(i + 1, n - 1), 1 - buf)

    f = r[:, 0:1] * y0 + r[:, 1:2] * y1
    out = _layer_norm(ALPHA * h + f, g_ref[...], b_ref[...])
    if last_layer:
        yp_ref[...] = out

        @pl.when(i == 0)
        def _():
            ysamp_ref[...] = out[N_HEAD:TB, :]
    else:
        o_ref[...] = out

    @pl.when(i == n - 1)
    def _():
        _tile_gather_wait(ys_hbm, ybuf.at[1 - buf], sem.at[1 - buf], TOP_K * TB)


def _combine_layer(last_layer, slot, h_tiles, r_all, ys, ln_g, ln_b):
    if last_layer:
        out_specs = [pl.BlockSpec((TB, D), lambda i, sl: (jnp.maximum(i - 1, 0), 0)),
                     pl.BlockSpec((DEC_BATCH, D), lambda i, sl: (0, 0))]
        out_shape = [jax.ShapeDtypeStruct((BATCH * SEQ, D), jnp.float32),
                     jax.ShapeDtypeStruct((DEC_BATCH, D), jnp.float32)]
    else:
        out_specs = pl.BlockSpec((TB, D), lambda i, sl: (i, 0))
        out_shape = jax.ShapeDtypeStruct((NTOK, D), jnp.float32)
    grid_spec = pltpu.PrefetchScalarGridSpec(
        num_scalar_prefetch=1,
        grid=(N_STEP,),
        in_specs=[pl.BlockSpec((TB, D), lambda i, sl: (i, 0)),
                  pl.BlockSpec((TB, LANES), lambda i, sl: (i, 0)),
                  pl.BlockSpec(memory_space=pl.ANY),
                  pl.BlockSpec((1, D), lambda i, sl: (0, 0)),
                  pl.BlockSpec((1, D), lambda i, sl: (0, 0))],
        out_specs=out_specs,
        scratch_shapes=[pltpu.VMEM((2, TOP_K * TB * SUBLANES, LANES), jnp.float32),
                        pltpu.SemaphoreType.DMA((2,))],
    )
    return pl.pallas_call(
        functools.partial(_combine_kernel, last_layer),
        grid_spec=grid_spec,
        out_shape=out_shape,
        compiler_params=pltpu.CompilerParams(dimension_semantics=("arbitrary",),
                                             vmem_limit_bytes=VMEM_LIMIT),
        name="moe_combine",
    )(slot, h_tiles, r_all, ys, ln_g, ln_b)


def _routing_metadata(rt, cnt):
    experts = jnp.arange(N_EXPERTS, dtype=jnp.int32)
    counts = cnt[0, :N_EXPERTS].astype(jnp.int32)
    nblk = (counts + BM - 1) // BM
    blk_end = jnp.cumsum(nblk)
    pstart = (blk_end - nblk) * BM
    starts = jnp.cumsum(counts) - counts
    nused = blk_end[-1]
    blk = jnp.arange(NB, dtype=jnp.int32)
    block_expert = jnp.sum((blk[:, None] >= blk_end[None, :]).astype(jnp.int32), axis=1)
    last_expert = jnp.max(jnp.where(nblk > 0, experts, 0))
    block_expert = jnp.where(blk < nused, block_expert, last_expert).astype(jnp.int32)
    of_block = block_expert[:, None] == experts[None, :]
    base = blk * BM + jnp.sum(jnp.where(of_block, (starts - pstart)[None, :], 0), axis=1)
    limit = jnp.sum(jnp.where(of_block, (starts + counts - 1)[None, :], 0), axis=1)
    rows = rt.reshape(N_STEP, SUBLANES, TB)
    expert = jnp.transpose(rows[:, 2:2 + TOP_K, :], (1, 0, 2)).reshape(N_ASSIGN).astype(jnp.int32)
    rank = jnp.transpose(rows[:, 2 + TOP_K:2 + 2 * TOP_K, :], (1, 0, 2)).reshape(N_ASSIGN).astype(jnp.int32)
    slot = rank + jnp.sum(jnp.where(expert[:, None] == experts[None, :], pstart[None, :], 0), axis=1)
    order = jnp.argsort(expert * RANK_RANGE + rank).astype(jnp.int32)
    return (block_expert, nused.reshape(1).astype(jnp.int32), base.astype(jnp.int32),
            limit.astype(jnp.int32), order, slot.astype(jnp.int32))


def kernel(x_prompt, x_sample, state_pool, state_conv, meta, w_pool, pool_scale, w_glu, b_glu, w_dw, b_dw, conv_ln_g, conv_ln_b, w_pw, b_pw, ln_mix_g, ln_mix_b, ln_ffn_g, ln_ffn_b, w_router_group, w_router_expert, w_gate, w_up, w_down):
    f32 = jnp.float32
    x_samp2d = x_sample.reshape(DEC_BATCH, D)
    head = jnp.concatenate([jnp.tile(meta.astype(f32), (BATCH, 1)), x_samp2d], axis=0)
    k_idx = jnp.arange(POOL_STATE)[:, None]
    win = jnp.repeat(jnp.asarray(POOL_WINDOWS), POOL_GROUP_DIM)[None, :]
    pool_coef = (k_idx >= (POOL_STATE + 1 - win)).astype(f32)

    new_pool_p, new_pool_s, new_conv_p, new_conv_s = [], [], [], []
    for i in range(DEPTH):
        j = i // 2
        wr32 = jnp.concatenate([w_router_group[i], w_router_expert[i],
                                jnp.zeros((D, LANES - N_GROUPS - N_EXPERTS), f32)], axis=1)
        wr_hi = wr32.astype(jnp.bfloat16)
        wr = jnp.concatenate([wr_hi, (wr32 - wr_hi.astype(f32)).astype(jnp.bfloat16)], axis=1)
        ln_g, ln_b = ln_mix_g[i].reshape(1, D), ln_mix_b[i].reshape(1, D)
        if i == 0:
            inputs = dict(first_inputs=(head, x_prompt.reshape(BATCH * SEQ, D)))
        else:
            inputs = dict(fused_inputs=(slot, h1, r_all, ys, ln_ffn_g[i - 1].reshape(1, D),
                                        ln_ffn_b[i - 1].reshape(1, D)))
        if i % 2 == 0:
            ps = _state_presum(j, state_pool, pool_coef)
            outs = _pool_layer(ps, w_pool[j].astype(jnp.bfloat16), pool_scale[j].reshape(1, D),
                               ln_g, ln_b, wr, **inputs)
            if i == 0:
                tail, x_s = x_prompt[:, SEQ - POOL_STATE:], x_samp2d
            else:
                x_s, tail16, outs = outs[0], outs[1], outs[2:]
                tail = tail16[:, POOL_CARRY - POOL_STATE:]
            h1, r_all, cnt, h1p, rt = outs
            new_pool_p.append(tail)
            new_pool_s.append(jnp.concatenate([state_pool[j][:, 1:], x_s[:, None, :]], axis=1))
        else:
            vs = _state_presum(j, state_conv, w_dw[j][:CONV_STATE])
            outs = _conv_layer(
                vs, w_glu[j].astype(jnp.bfloat16), b_glu[j].reshape(1, 2 * D), w_dw[j],
                b_dw[j].reshape(1, D), conv_ln_g[j].reshape(1, D), conv_ln_b[j].reshape(1, D),
                w_pw[j].astype(jnp.bfloat16), b_pw[j].reshape(1, D), ln_g, ln_b, wr, **inputs)
            h1, r_all, cnt, h1p, rt, ust, u_s = outs if i == 0 else outs[2:]
            new_conv_p.append(ust)
            new_conv_s.append(jnp.concatenate([state_conv[j][:, 1:], u_s[:, None, :]], axis=1))
        block_expert, nused, base, limit, order, slot = _routing_metadata(rt, cnt)
        ys = _expert_layer(i, block_expert, nused, base, limit, order, h1p, w_gate, w_up, w_down)

    y_prompt, y_samp = _combine_layer(True, slot, h1, r_all, ys, ln_ffn_g[DEPTH - 1].reshape(1, D),
                                      ln_ffn_b[DEPTH - 1].reshape(1, D))
    return (y_prompt.reshape(BATCH, SEQ, D), y_samp.reshape(DEC_BATCH, 1, D),
            jnp.stack(new_pool_p), jnp.stack(new_pool_s), jnp.stack(new_conv_p), jnp.stack(new_conv_s))
```

```python
import functools

import jax
import jax.numpy as jnp
from jax import lax
from jax.experimental import pallas as pl
from jax.experimental.pallas import tpu as pltpu

D = 1024
BATCH = 8
SEQ = 2048
DEPTH = 4
DEC_BATCH = 128
N_META = 16
POOL_WINDOWS = (2, 4, 8, 16)
POOL_GROUP_DIM = D // len(POOL_WINDOWS)
POOL_STATE = max(POOL_WINDOWS) - 1
CONV_WIDTH = 31
CONV_STATE = CONV_WIDTH - 1
N_GROUPS = 4
EXPERTS_PER_GROUP = 8
N_EXPERTS = N_GROUPS * EXPERTS_PER_GROUP
TOP_K = 2
D_EXPERT = D // 2
ALPHA = (2.0 * DEPTH) ** 0.25
LN_EPS = 1e-5

LANES = 128
SUBLANES = 8
N_HEAD = BATCH * N_META
TB = 256
NTOK = N_HEAD + DEC_BATCH + BATCH * SEQ
N_STEP = NTOK // TB
CHUNKS = SEQ // TB
POOL_CARRY = 16
POOL_LEAD = 8
CONV_CARRY = 32
CONV_RB = 128
CONV_CB = 128
BM = 512
N_ASSIGN = NTOK * TOP_K
RANK_RANGE = 1 << 16
NB = N_ASSIGN // BM + N_EXPERTS
N_SLOT = NB * BM
VMEM_LIMIT = 48 * 1024 * 1024
EXPERT_VMEM_LIMIT = 56 * 1024 * 1024
NEG = -1e30

assert D == SUBLANES * LANES and N_HEAD + DEC_BATCH == TB and SEQ % TB == 0
assert N_ASSIGN % BM == 0 and TB % CONV_RB == 0 and N_META == POOL_CARRY and CONV_STATE <= TB
assert RANK_RANGE > N_ASSIGN and N_EXPERTS * RANK_RANGE < 2 ** 31 and TOP_K == 2


def _tile_load(ref, n):
    return jnp.concatenate([ref[pl.ds(s, n, stride=SUBLANES), :] for s in range(SUBLANES)], axis=1)


def _tile_store(ref, val, n):
    for s in range(SUBLANES):
        ref[pl.ds(s, n, stride=SUBLANES), :] = val[:, s * LANES:(s + 1) * LANES]


def _layer_norm(x, g, b):
    mu = jnp.mean(x, axis=-1, keepdims=True)
    xc = x - mu
    var = jnp.mean(xc * xc, axis=-1, keepdims=True)
    return xc * lax.rsqrt(var + LN_EPS) * g + b


PACK_ROWS = D // 2 // LANES


def _pack_bf16(h_bf16, hp_ref):
    bits = pltpu.bitcast(h_bf16.astype(jnp.float32), jnp.uint32)
    words = jnp.bitwise_or(lax.shift_right_logical(bits[:, :D // 2], jnp.uint32(16)),
                           jnp.bitwise_and(bits[:, D // 2:], jnp.uint32(0xFFFF0000)))
    for s in range(PACK_ROWS):
        hp_ref[:, s, :] = words[:, s * LANES:(s + 1) * LANES]


def _unpack_bf16(words):
    lo = pltpu.bitcast(lax.shift_left(words, jnp.uint32(16)), jnp.float32)
    hi = pltpu.bitcast(jnp.bitwise_and(words, jnp.uint32(0xFFFF0000)), jnp.float32)
    return jnp.concatenate([lo, hi], axis=1).astype(jnp.bfloat16)


def _route_init(cnt_ref, ltri_ref):
    @pl.when(pl.program_id(0) == 0)
    def _():
        cnt_ref[...] = jnp.zeros((1, LANES), jnp.float32)
        row = lax.broadcasted_iota(jnp.int32, (TB, TB), 0)
        col = lax.broadcasted_iota(jnp.int32, (TB, TB), 1)
        ltri_ref[...] = jnp.where(col < row, 1.0, 0.0).astype(jnp.bfloat16)


def _residual_route(x, m, g_ref, b_ref, wr_ref, h_ref, r_ref, cnt_out_ref, hp_ref, rt_ref, cnt_ref, ltri_ref):
    h = _layer_norm(ALPHA * x + m, g_ref[...], b_ref[...])
    h_ref[...] = h
    h_hi = h.astype(jnp.bfloat16)
    _pack_bf16(h_hi, hp_ref)
    h_lo = (h - h_hi.astype(jnp.float32)).astype(jnp.bfloat16)
    hi_both = jnp.dot(h_hi, wr_ref[...], preferred_element_type=jnp.float32)
    lo_hi = jnp.dot(h_lo, wr_ref[:, 0:LANES], preferred_element_type=jnp.float32)
    logits = hi_both[:, 0:LANES] + (hi_both[:, LANES:2 * LANES] + lo_hi)
    lane = lax.broadcasted_iota(jnp.int32, logits.shape, 1)
    lanef = lane.astype(jnp.float32)
    big = jnp.float32(1e9)
    lg = jnp.where(lane < N_GROUPS, logits, NEG)
    mg = jnp.max(lg, axis=1, keepdims=True)
    gidx = jnp.min(jnp.where(lg == mg, lanef, big), axis=1, keepdims=True)
    p_grp = 1.0 / jnp.sum(jnp.where(lane < N_GROUPS, jnp.exp(lg - mg), 0.0), axis=1, keepdims=True)
    lo = N_GROUPS + EXPERTS_PER_GROUP * gidx
    le = jnp.where(lanef >= lo, jnp.where(lanef < lo + EXPERTS_PER_GROUP, logits, NEG), NEG)
    m1 = jnp.max(le, axis=1, keepdims=True)
    i1 = jnp.min(jnp.where(le == m1, lanef, big), axis=1, keepdims=True)
    le2 = jnp.where(lanef == i1, NEG, le)
    m2 = jnp.max(le2, axis=1, keepdims=True)
    i2 = jnp.min(jnp.where(le2 == m2, lanef, big), axis=1, keepdims=True)
    ratio = jnp.exp(m2 - m1)
    g1 = p_grp / (1.0 + ratio)
    g2 = g1 * ratio
    e1 = i1 - N_GROUPS
    e2 = i2 - N_GROUPS

    is1 = lanef == e1
    is2 = lanef == e2
    onehot = jnp.where(is1, 1.0, jnp.where(is2, 1.0, 0.0))
    before = jnp.dot(ltri_ref[...], onehot.astype(jnp.bfloat16),
                     preferred_element_type=jnp.float32) + cnt_ref[...]
    rank1 = jnp.sum(jnp.where(is1, before, 0.0), axis=1, keepdims=True)
    rank2 = jnp.sum(jnp.where(is2, before, 0.0), axis=1, keepdims=True)
    cnt = cnt_ref[...] + jnp.sum(onehot, axis=0, keepdims=True)
    cnt_ref[...] = cnt
    cnt_out_ref[...] = cnt

    packed = jnp.where(lane == 0, g1,
                       jnp.where(lane == 1, g2,
                                 jnp.where(lane == 2, e1,
                                           jnp.where(lane == 3, e2,
                                                     jnp.where(lane == 4, rank1,
                                                               jnp.where(lane == 5, rank2, 0.0))))))
    r_ref[...] = packed
    rt_ref[...] = jnp.transpose(packed)[0:SUBLANES, :]


def _seq_of_step(s):
    sm1 = jnp.maximum(s - 1, 0)
    return lax.shift_right_logical(sm1, CHUNKS.bit_length() - 1), jnp.bitwise_and(sm1, CHUNKS - 1)


assert CHUNKS & (CHUNKS - 1) == 0


PRESUM_ROWS = 32


def _presum_kernel(st_ref, coef_ref, o_ref):
    coef = coef_ref[...]
    for n in range(PRESUM_ROWS):
        o_ref[n:n + 1, :] = jnp.sum(st_ref[0, n] * coef, axis=0, keepdims=True)


def _state_presum(j, state, coef):
    _, n, k, _ = state.shape
    return pl.pallas_call(
        _presum_kernel,
        grid=(n // PRESUM_ROWS,),
        in_specs=[pl.BlockSpec((1, PRESUM_ROWS, k, D), lambda i: (j, i, 0, 0)),
                  pl.BlockSpec((k, D), lambda i: (0, 0))],
        out_specs=pl.BlockSpec((PRESUM_ROWS, D), lambda i: (i, 0)),
        out_shape=jax.ShapeDtypeStruct((n, D), jnp.float32),
        compiler_params=pltpu.CompilerParams(dimension_semantics=("arbitrary",),
                                             vmem_limit_bytes=VMEM_LIMIT),
        name="state_presum",
    )(state, coef)


POOL_X0 = POOL_LEAD + POOL_CARRY


def _window_sums(ext_ref, lvl2_ref, lvl4_ref, lvl8_ref):
    n = POOL_CARRY + TB
    gd = POOL_GROUP_DIM

    def doubled(src_ref, shift, c0):
        return src_ref[POOL_LEAD:POOL_LEAD + n, c0:] + src_ref[POOL_LEAD - shift:POOL_LEAD - shift + n, c0:]

    lvl2_ref[POOL_LEAD:POOL_LEAD + n, :] = doubled(ext_ref, 1, 0)
    lvl4_ref[POOL_LEAD:POOL_LEAD + n, :] = doubled(lvl2_ref, 2, gd)
    lvl8_ref[POOL_LEAD:POOL_LEAD + n, :] = doubled(lvl4_ref, 4, gd)
    s16 = (lvl8_ref[POOL_X0:POOL_X0 + TB, gd:2 * gd] +
           lvl8_ref[POOL_X0 - 8:POOL_X0 - 8 + TB, gd:2 * gd])
    return (lvl2_ref[POOL_X0:POOL_X0 + TB, 0:gd], lvl4_ref[POOL_X0:POOL_X0 + TB, 0:gd],
            lvl8_ref[POOL_X0:POOL_X0 + TB, 0:gd], s16)


assert POOL_WINDOWS == (2, 4, 8, 16) and POOL_LEAD >= 4 and POOL_LEAD % SUBLANES == 0


def _pool_kernel(first_layer, head_ref, x_ref, ps_ref, wp_ref, sc_ref, g_ref, b_ref, wr_ref,
                 h_ref, r_ref, cnt_out_ref, hp_ref, rt_ref, ext_ref, m_ref, carry_ref, cnt_ref, ltri_ref,
                 lvl2_ref, lvl4_ref, lvl8_ref, start_next=lambda: None):
    s = pl.program_id(0)
    seq, chunk = _seq_of_step(s)
    _route_init(cnt_ref, ltri_ref)
    route_refs = (g_ref, b_ref, wr_ref, h_ref, r_ref, cnt_out_ref, hp_ref, rt_ref, cnt_ref, ltri_ref)

    @pl.when(s == 0)
    def _():
        ext_ref[0:POOL_LEAD, :] = jnp.zeros((POOL_LEAD, D), jnp.float32)
        lvl2_ref[0:POOL_LEAD, :] = jnp.zeros((POOL_LEAD, D), jnp.float32)
        lvl4_ref[0:POOL_LEAD, :] = jnp.zeros((POOL_LEAD, D - POOL_GROUP_DIM), jnp.float32)
        x = head_ref[...] if first_layer else x_ref[...]
        start_next()
        ext_ref[POOL_LEAD:POOL_X0, :] = jnp.zeros((POOL_CARRY, D), jnp.float32)
        ext_ref[POOL_X0:POOL_X0 + TB, :] = x
        carry_ref[...] = x[0:N_HEAD, :]
        pos = jnp.bitwise_and(lax.broadcasted_iota(jnp.int32, (N_HEAD, 1), 0), N_META - 1)
        for g, w in enumerate(POOL_WINDOWS):
            lo, hi = g * POOL_GROUP_DIM, (g + 1) * POOL_GROUP_DIM
            xm = x[0:N_HEAD, lo:hi]
            acc = xm
            for k in range(1, w):
                acc = acc + jnp.where(pos >= k, ext_ref[POOL_X0 - k:POOL_X0 - k + N_HEAD, lo:hi], 0.0)
            cnt = jnp.minimum(pos + 1, w).astype(jnp.float32)
            d_meta = acc / cnt - xm
            xs = x[N_HEAD:TB, lo:hi]
            d_samp = (ps_ref[:, lo:hi] + xs) / float(w) - xs
            diff = jnp.concatenate([d_meta, d_samp], axis=0)
            y = jnp.dot(diff.astype(jnp.bfloat16), wp_ref[g], preferred_element_type=jnp.float32)
            m_ref[:, lo:hi] = y * sc_ref[:, lo:hi]
        _residual_route(x, m_ref[...], *route_refs)

    @pl.when(s > 0)
    def _():
        @pl.when(chunk == 0)
        def _():
            ext_ref[POOL_LEAD:POOL_X0, :] = carry_ref[pl.ds(pl.multiple_of(seq * N_META, N_META), N_META), :]

        x = x_ref[...]
        start_next()
        ext_ref[POOL_X0:POOL_X0 + TB, :] = x
        sums = _window_sums(ext_ref, lvl2_ref, lvl4_ref, lvl8_ref)
        ms = []
        for g, w in enumerate(POOL_WINDOWS):
            lo, hi = g * POOL_GROUP_DIM, (g + 1) * POOL_GROUP_DIM
            diff = sums[g] / float(w) - x[:, lo:hi]
            y = jnp.dot(diff.astype(jnp.bfloat16), wp_ref[g], preferred_element_type=jnp.float32)
            ms.append(y * sc_ref[:, lo:hi])
        _residual_route(x, jnp.concatenate(ms, axis=1), *route_refs)
        ext_ref[POOL_LEAD:POOL_X0, :] = ext_ref[POOL_LEAD + TB:POOL_X0 + TB, :]


assert N_META >= max(POOL_WINDOWS)


def _const_spec(shape):
    nd = len(shape)
    return pl.BlockSpec(shape, lambda s, *_: (0,) * nd)


_MIXER_OUT_SPECS = [pl.BlockSpec((TB, D), lambda s, *_: (s, 0)),
                    pl.BlockSpec((TB, LANES), lambda s, *_: (s, 0)),
                    pl.BlockSpec((1, LANES), lambda s, *_: (0, 0)),
                    pl.BlockSpec((TB, PACK_ROWS, LANES), lambda s, *_: (s, 0, 0)),
                    pl.BlockSpec((SUBLANES, TB), lambda s, *_: (s, 0))]
_MIXER_OUT_SHAPES = [jax.ShapeDtypeStruct((NTOK, D), jnp.float32),
                     jax.ShapeDtypeStruct((NTOK, LANES), jnp.float32),
                     jax.ShapeDtypeStruct((1, LANES), jnp.float32),
                     jax.ShapeDtypeStruct((NTOK, PACK_ROWS, LANES), jnp.uint32),
                     jax.ShapeDtypeStruct((N_STEP * SUBLANES, TB), jnp.float32)]


def _combine_rows(slot_ref, h_ref, r_ref, ys_hbm, g_ref, b_ref, ybuf, sem):
    i = pl.program_id(0)
    n = pl.num_programs(0)

    def start(step, buf):
        base = step * TB
        _tile_gather_start(ys_hbm, ybuf.at[buf], sem.at[buf],
                           lambda r: slot_ref[(r % TOP_K) * NTOK + base + r // TOP_K], TOP_K * TB,
                           dst_tile=lambda r: (r % TOP_K) * TB + r // TOP_K)

    @pl.when(i == 0)
    def _():
        start(0, 0)

    buf = i % 2
    _tile_gather_wait(ys_hbm, ybuf.at[buf], sem.at[buf], TOP_K * TB)
    yb = ybuf.at[buf]
    y0 = _tile_load(yb.at[pl.ds(0, TB * SUBLANES), :], TB)
    y1 = _tile_load(yb.at[pl.ds(TB * SUBLANES, TB * SUBLANES), :], TB)
    r = r_ref[...]
    rows = _layer_norm(ALPHA * h_ref[...] + (r[:, 0:1] * y0 + r[:, 1:2] * y1), g_ref[...], b_ref[...])

    def drain():
        @pl.when(i == n - 1)
        def _():
            _tile_gather_wait(ys_hbm, ybuf.at[1 - buf], sem.at[1 - buf], TOP_K * TB)

    return rows, lambda: start(jnp.minimum(i + 1, n - 1), 1 - buf), drain


_COMBINE_SCRATCH = [pltpu.VMEM((2, TOP_K * TB * SUBLANES, LANES), jnp.float32),
                    pltpu.SemaphoreType.DMA((2,))]


def _fused_mixer_kernel(body, n_in, slot_ref, hprev_ref, rprev_ref, ys_hbm, fg_ref, fb_ref, *rest):
    mixer_in = rest[:n_in]
    xs_out_ref, tail_ref = rest[n_in:n_in + 2]
    x_scr, ybuf, sem = rest[-3:]
    i = pl.program_id(0)
    x, start_next, drain = _combine_rows(slot_ref, hprev_ref, rprev_ref, ys_hbm, fg_ref, fb_ref, ybuf, sem)
    x_scr[...] = x

    @pl.when(i == 0)
    def _():
        xs_out_ref[...] = x[N_HEAD:TB, :]

    _, chunk = _seq_of_step(i)

    @pl.when(jnp.logical_and(i > 0, chunk == CHUNKS - 1))
    def _():
        tail_ref[0] = x[TB - POOL_CARRY:TB, :]

    body(False, None, x_scr, *mixer_in, *rest[n_in + 2:-3], start_next=start_next)
    drain()


def _mixer_call(name, body, mixer_in_specs, extra_out_specs, extra_out_shapes, scratch_shapes, mixer_inputs,
                first_inputs=None, fused_inputs=None):
    out_specs = _MIXER_OUT_SPECS + extra_out_specs
    out_shape = _MIXER_OUT_SHAPES + extra_out_shapes
    params = pltpu.CompilerParams(dimension_semantics=("arbitrary",), vmem_limit_bytes=VMEM_LIMIT)
    if fused_inputs is None:
        return pl.pallas_call(
            functools.partial(body, True),
            grid=(N_STEP,),
            in_specs=[_const_spec((TB, D)),
                      pl.BlockSpec((TB, D), lambda s, *_: (jnp.maximum(s - 1, 0), 0))] + mixer_in_specs,
            out_specs=out_specs, out_shape=out_shape, scratch_shapes=scratch_shapes,
            compiler_params=params, name=name,
        )(*first_inputs, *mixer_inputs)
    seq_block = lambda s, *_: (jnp.maximum(s - 1, 0) // CHUNKS, 0, 0)
    grid_spec = pltpu.PrefetchScalarGridSpec(
        num_scalar_prefetch=1,
        grid=(N_STEP,),
        in_specs=[pl.BlockSpec((TB, D), lambda s, *_: (s, 0)),
                  pl.BlockSpec((TB, LANES), lambda s, *_: (s, 0)),
                  pl.BlockSpec(memory_space=pl.ANY),
                  _const_spec((1, D)), _const_spec((1, D))] + mixer_in_specs,
        out_specs=[_const_spec((DEC_BATCH, D)), pl.BlockSpec((1, POOL_CARRY, D), seq_block)] + out_specs,
        scratch_shapes=scratch_shapes + [pltpu.VMEM((TB, D), jnp.float32)] + _COMBINE_SCRATCH,
    )
    return pl.pallas_call(
        functools.partial(_fused_mixer_kernel, body, len(mixer_in_specs)),
        grid_spec=grid_spec,
        out_shape=[jax.ShapeDtypeStruct((DEC_BATCH, D), jnp.float32),
                   jax.ShapeDtypeStruct((BATCH, POOL_CARRY, D), jnp.float32)] + out_shape,
        compiler_params=params, name=name + "_fused",
    )(*fused_inputs, *mixer_inputs)


def _pool_layer(ps, wp_bf, scale, ln_g, ln_b, wr, **inputs):
    ext_rows = POOL_X0 + TB
    return _mixer_call(
        "pool_mixer", _pool_kernel,
        [_const_spec((DEC_BATCH, D)),
         _const_spec((len(POOL_WINDOWS), POOL_GROUP_DIM, POOL_GROUP_DIM)),
         _const_spec((1, D)), _const_spec((1, D)), _const_spec((1, D)),
         _const_spec((D, 2 * LANES))],
        [], [],
        [pltpu.VMEM((ext_rows, D), jnp.float32),
         pltpu.VMEM((TB, D), jnp.float32),
         pltpu.VMEM((N_HEAD, D), jnp.float32),
         pltpu.VMEM((1, LANES), jnp.float32),
         pltpu.VMEM((TB, TB), jnp.bfloat16),
         pltpu.VMEM((ext_rows, D), jnp.float32),
         pltpu.VMEM((ext_rows, D - POOL_GROUP_DIM), jnp.float32),
         pltpu.VMEM((ext_rows, D - 2 * POOL_GROUP_DIM), jnp.float32)],
        (ps, wp_bf, scale, ln_g, ln_b, wr), **inputs)


def _depthwise_conv(ext_ref, wdw_ref, v_ref):
    base = CONV_CARRY - CONV_STATE
    for cb in range(D // CONV_CB):
        lo, hi = cb * CONV_CB, (cb + 1) * CONV_CB

        def body(i, carry, lo=lo, hi=hi):
            r0 = pl.multiple_of(i * CONV_RB, CONV_RB)
            sub = ext_ref.at[pl.ds(r0, CONV_RB + CONV_CARRY), :]
            v = None
            for r in range(SUBLANES):
                qs = [q for q in range((base + CONV_WIDTH) // SUBLANES + 1)
                      if 0 <= SUBLANES * q + r - base < CONV_WIDTH]
                z0 = SUBLANES * qs[0] + r
                z = sub[z0:SUBLANES * qs[-1] + r + CONV_RB, lo:hi]
                p = None
                for q in qs:
                    k = SUBLANES * q + r - base
                    off = SUBLANES * (q - qs[0])
                    term = wdw_ref[k:k + 1, lo:hi] * z[off:off + CONV_RB]
                    p = term if p is None else p + term
                v = p if v is None else v + p
            v_ref[pl.ds(r0, CONV_RB), lo:hi] = v
            return carry
        lax.fori_loop(0, TB // CONV_RB, body, 0)


def _conv_kernel(first_layer, head_ref, x_ref, vs_ref, wglu_ref, bglu_ref, wdw_ref, bdw_ref, lg_ref, lb_ref,
                 wpw_ref, bpw_ref, g_ref, b_ref, wr_ref, h_ref, r_ref, cnt_out_ref, hp_ref, rt_ref, ust_ref, us_ref,
                 ext_ref, v_ref, carry_ref, cnt_ref, ltri_ref, start_next=lambda: None):
    s = pl.program_id(0)
    seq, chunk = _seq_of_step(s)
    _route_init(cnt_ref, ltri_ref)
    route_refs = (g_ref, b_ref, wr_ref, h_ref, r_ref, cnt_out_ref, hp_ref, rt_ref, cnt_ref, ltri_ref)

    def glu(x):
        hh = jnp.dot(x.astype(jnp.bfloat16), wglu_ref[...], preferred_element_type=jnp.float32) + bglu_ref[...]
        return hh[:, :D] * jax.nn.sigmoid(hh[:, D:])

    def finish(x, v):
        v = _layer_norm(v + bdw_ref[...], lg_ref[...], lb_ref[...])
        v = v * jax.nn.sigmoid(v)
        m = jnp.dot(v.astype(jnp.bfloat16), wpw_ref[...], preferred_element_type=jnp.float32) + bpw_ref[...]
        _residual_route(x, m, *route_refs)

    @pl.when(s == 0)
    def _():
        x = head_ref[...] if first_layer else x_ref[...]
        start_next()
        u = glu(x)
        um = u[0:N_HEAD, :]
        us = u[N_HEAD:TB, :]
        carry_ref[...] = um
        us_ref[...] = us
        ext_ref[0:N_META, :] = jnp.zeros((N_META, D), jnp.float32)
        ext_ref[N_META:N_META + N_HEAD, :] = um
        pos = jnp.bitwise_and(lax.broadcasted_iota(jnp.int32, (N_HEAD, 1), 0), N_META - 1)
        acc = wdw_ref[CONV_WIDTH - 1:CONV_WIDTH, :] * um
        for d in range(1, N_META):
            k = CONV_WIDTH - 1 - d
            acc = acc + wdw_ref[k:k + 1, :] * jnp.where(pos >= d, ext_ref[N_META - d:N_META - d + N_HEAD, :], 0.0)
        v_samp = vs_ref[...] + wdw_ref[CONV_WIDTH - 1:CONV_WIDTH, :] * us
        finish(x, jnp.concatenate([acc, v_samp], axis=0))

    @pl.when(s > 0)
    def _():
        @pl.when(chunk == 0)
        def _():
            ext_ref[0:CONV_CARRY - N_META, :] = jnp.zeros((CONV_CARRY - N_META, D), jnp.float32)
            ext_ref[CONV_CARRY - N_META:CONV_CARRY, :] = carry_ref[
                pl.ds(pl.multiple_of(seq * N_META, N_META), N_META), :]

        x = x_ref[...]
        start_next()
        ext_ref[CONV_CARRY:CONV_CARRY + TB, :] = glu(x)
        _depthwise_conv(ext_ref, wdw_ref, v_ref)
        finish(x, v_ref[...])
        ext_ref[0:CONV_CARRY, :] = ext_ref[TB:TB + CONV_CARRY, :]

        @pl.when(chunk == CHUNKS - 1)
        def _():
            ust_ref[0] = ext_ref[CONV_CARRY + TB - CONV_STATE:CONV_CARRY + TB, :]


assert N_META <= CONV_STATE <= CONV_CARRY


def _conv_layer(vs, wglu_bf, bglu, wdw, bdw, cln_g, cln_b, wpw_bf, bpw, ln_g, ln_b, wr, **inputs):
    return _mixer_call(
        "conv_mixer", _conv_kernel,
        [_const_spec((DEC_BATCH, D)),
         _const_spec((D, 2 * D)), _const_spec((1, 2 * D)),
         _const_spec((CONV_WIDTH, D)), _const_spec((1, D)),
         _const_spec((1, D)), _const_spec((1, D)),
         _const_spec((D, D)), _const_spec((1, D)),
         _const_spec((1, D)), _const_spec((1, D)),
         _const_spec((D, 2 * LANES))],
        [pl.BlockSpec((1, CONV_STATE, D), lambda s, *_: (jnp.maximum(s - 1, 0) // CHUNKS, 0, 0)),
         _const_spec((DEC_BATCH, D))],
        [jax.ShapeDtypeStruct((BATCH, CONV_STATE, D), jnp.float32),
         jax.ShapeDtypeStruct((DEC_BATCH, D), jnp.float32)],
        [pltpu.VMEM((CONV_CARRY + TB, D), jnp.float32),
         pltpu.VMEM((TB, D), jnp.float32),
         pltpu.VMEM((N_HEAD, D), jnp.float32),
         pltpu.VMEM((1, LANES), jnp.float32),
         pltpu.VMEM((TB, TB), jnp.bfloat16)],
        (vs, wglu_bf, bglu, wdw, bdw, cln_g, cln_b, wpw_bf, bpw, ln_g, ln_b, wr), **inputs)


def _tile_gather_start(src_hbm, dst, sem, row_index, n_rows, dst_tile=lambda r: r):
    for r in range(n_rows):
        tok = row_index(r)
        pltpu.make_async_copy(src_hbm.at[pl.ds(pl.multiple_of(tok * SUBLANES, SUBLANES), SUBLANES), :],
                              dst.at[pl.ds(dst_tile(r) * SUBLANES, SUBLANES), :], sem).start(priority=r % 2)


def _tile_gather_wait(src_hbm, dst, sem, n_rows):
    pltpu.make_async_copy(src_hbm.at[pl.ds(0, n_rows * SUBLANES), :], dst, sem).wait()


def _expert_kernel(be_ref, nused_ref, base_ref, limit_ref, order_ref, hp_ref, wg_ref, wu_ref, wd_ref, ys_ref,
                   xbuf_even, xbuf_odd, wgb, wub, wdb):
    i = pl.program_id(0)
    nused = nused_ref[0]

    def gather(blk, xbuf):
        b0 = base_ref[blk]
        lim = limit_ref[blk]
        for r in range(BM):
            a = order_ref[jnp.minimum(b0 + r, lim)]
            xbuf[r] = hp_ref[jnp.where(a >= NTOK, a - NTOK, a)]

    def block(cur, nxt):
        gather(jnp.minimum(i + 1, jnp.maximum(nused - 1, 0)), nxt)
        words = jnp.concatenate([cur[:, s, :] for s in range(PACK_ROWS)], axis=1)
        xb = _unpack_bf16(words)
        gate = jnp.dot(xb, wgb[...], preferred_element_type=jnp.float32)
        up = jnp.dot(xb, wub[...], preferred_element_type=jnp.float32)
        hid = gate * jax.nn.sigmoid(gate) * up
        y = jnp.dot(hid.astype(jnp.bfloat16), wdb[...], preferred_element_type=jnp.float32)
        _tile_store(ys_ref, y, BM)

    @pl.when(i == 0)
    def _():
        gather(0, xbuf_even)

    @pl.when(jnp.logical_and(i < nused, jnp.logical_or(i == 0, be_ref[i] != be_ref[jnp.maximum(i - 1, 0)])))
    def _():
        wgb[...] = wg_ref[0, 0].astype(jnp.bfloat16)
        wub[...] = wu_ref[0, 0].astype(jnp.bfloat16)
        wdb[...] = wd_ref[0, 0].astype(jnp.bfloat16)

    odd = jnp.bitwise_and(i, 1) == 1

    @pl.when(jnp.logical_and(i < nused, jnp.logical_not(odd)))
    def _():
        block(xbuf_even, xbuf_odd)

    @pl.when(jnp.logical_and(i < nused, odd))
    def _():
        block(xbuf_odd, xbuf_even)

    @pl.when(i >= nused)
    def _():
        ys_ref[...] = jnp.zeros((BM * SUBLANES, LANES), jnp.float32)


def _expert_layer(layer, block_expert, nused, base, limit, order, h_packed, w_gate, w_up, w_down):
    def w_spec(shape):
        return pl.BlockSpec((1, 1) + shape, lambda i, be, *_: (layer, be[i], 0, 0))
    grid_spec = pltpu.PrefetchScalarGridSpec(
        num_scalar_prefetch=5,
        grid=(NB,),
        in_specs=[pl.BlockSpec((NTOK, PACK_ROWS, LANES), lambda i, *_: (0, 0, 0)),
                  w_spec((D, D_EXPERT)), w_spec((D, D_EXPERT)), w_spec((D_EXPERT, D))],
        out_specs=pl.BlockSpec((BM * SUBLANES, LANES), lambda i, *_: (i, 0)),
        scratch_shapes=[pltpu.VMEM((BM, PACK_ROWS, LANES), jnp.uint32),
                        pltpu.VMEM((BM, PACK_ROWS, LANES), jnp.uint32),
                        pltpu.VMEM((D, D_EXPERT), jnp.bfloat16),
                        pltpu.VMEM((D, D_EXPERT), jnp.bfloat16),
                        pltpu.VMEM((D_EXPERT, D), jnp.bfloat16)],
    )
    return pl.pallas_call(
        _expert_kernel,
        grid_spec=grid_spec,
        out_shape=jax.ShapeDtypeStruct((N_SLOT * SUBLANES, LANES), jnp.float32),
        compiler_params=pltpu.CompilerParams(dimension_semantics=("arbitrary",),
                                             vmem_limit_bytes=EXPERT_VMEM_LIMIT),
        name="expert_mlp",
    )(block_expert, nused, base, limit, order, h_packed, w_gate, w_up, w_down)


def _final_combine_kernel(slot_ref, h_ref, r_ref, ys_hbm, g_ref, b_ref, yp_ref, ysamp_ref, ybuf, sem):
    out, start_next, drain = _combine_rows(slot_ref, h_ref, r_ref, ys_hbm, g_ref, b_ref, ybuf, sem)
    start_next()
    yp_ref[...] = out

    @pl.when(pl.program_id(0) == 0)
    def _():
        ysamp_ref[...] = out[N_HEAD:TB, :]

    drain()


def _final_combine(slot, h, r_all, ys, ln_g, ln_b):
    grid_spec = pltpu.PrefetchScalarGridSpec(
        num_scalar_prefetch=1,
        grid=(N_STEP,),
        in_specs=[pl.BlockSpec((TB, D), lambda i, sl: (i, 0)),
                  pl.BlockSpec((TB, LANES), lambda i, sl: (i, 0)),
                  pl.BlockSpec(memory_space=pl.ANY),
                  pl.BlockSpec((1, D), lambda i, sl: (0, 0)),
                  pl.BlockSpec((1, D), lambda i, sl: (0, 0))],
        out_specs=[pl.BlockSpec((TB, D), lambda i, sl: (jnp.maximum(i - 1, 0), 0)),
                   pl.BlockSpec((DEC_BATCH, D), lambda i, sl: (0, 0))],
        scratch_shapes=_COMBINE_SCRATCH,
    )
    return pl.pallas_call(
        _final_combine_kernel,
        grid_spec=grid_spec,
        out_shape=[jax.ShapeDtypeStruct((BATCH * SEQ, D), jnp.float32),
                   jax.ShapeDtypeStruct((DEC_BATCH, D), jnp.float32)],
        compiler_params=pltpu.CompilerParams(dimension_semantics=("arbitrary",),
                                             vmem_limit_bytes=VMEM_LIMIT),
        name="moe_combine",
    )(slot, h, r_all, ys, ln_g, ln_b)


def _routing_metadata(rt, cnt):
    experts = jnp.arange(N_EXPERTS, dtype=jnp.int32)
    counts = cnt[0, :N_EXPERTS].astype(jnp.int32)
    nblk = (counts + BM - 1) // BM
    blk_end = jnp.cumsum(nblk)
    pstart = (blk_end - nblk) * BM
    starts = jnp.cumsum(counts) - counts
    nused = blk_end[-1]
    blk = jnp.arange(NB, dtype=jnp.int32)
    block_expert = jnp.sum((blk[:, None] >= blk_end[None, :]).astype(jnp.int32), axis=1)
    last_expert = jnp.max(jnp.where(nblk > 0, experts, 0))
    block_expert = jnp.where(blk < nused, block_expert, last_expert).astype(jnp.int32)
    of_block = block_expert[:, None] == experts[None, :]
    base = blk * BM + jnp.sum(jnp.where(of_block, (starts - pstart)[None, :], 0), axis=1)
    limit = jnp.sum(jnp.where(of_block, (starts + counts - 1)[None, :], 0), axis=1)
    rows = rt.reshape(N_STEP, SUBLANES, TB)
    expert = jnp.transpose(rows[:, 2:2 + TOP_K, :], (1, 0, 2)).reshape(N_ASSIGN).astype(jnp.int32)
    rank = jnp.transpose(rows[:, 2 + TOP_K:2 + 2 * TOP_K, :], (1, 0, 2)).reshape(N_ASSIGN).astype(jnp.int32)
    slot = rank + jnp.sum(jnp.where(expert[:, None] == experts[None, :], pstart[None, :], 0), axis=1)
    order = jnp.argsort(expert * RANK_RANGE + rank).astype(jnp.int32)
    return (block_expert, nused.reshape(1).astype(jnp.int32), base.astype(jnp.int32),
            limit.astype(jnp.int32), order, slot.astype(jnp.int32))


def kernel(x_prompt, x_sample, state_pool, state_conv, meta, w_pool, pool_scale, w_glu, b_glu, w_dw, b_dw, conv_ln_g, conv_ln_b, w_pw, b_pw, ln_mix_g, ln_mix_b, ln_ffn_g, ln_ffn_b, w_router_group, w_router_expert, w_gate, w_up, w_down):
    f32 = jnp.float32
    x_samp2d = x_sample.reshape(DEC_BATCH, D)
    head = jnp.concatenate([jnp.tile(meta.astype(f32), (BATCH, 1)), x_samp2d], axis=0)
    k_idx = jnp.arange(POOL_STATE)[:, None]
    win = jnp.repeat(jnp.asarray(POOL_WINDOWS), POOL_GROUP_DIM)[None, :]
    pool_coef = (k_idx >= (POOL_STATE + 1 - win)).astype(f32)

    new_pool_p, new_pool_s, new_conv_p, new_conv_s = [], [], [], []
    for i in range(DEPTH):
        j = i // 2
        wr32 = jnp.concatenate([w_router_group[i], w_router_expert[i],
                                jnp.zeros((D, LANES - N_GROUPS - N_EXPERTS), f32)], axis=1)
        wr_hi = wr32.astype(jnp.bfloat16)
        wr = jnp.concatenate([wr_hi, (wr32 - wr_hi.astype(f32)).astype(jnp.bfloat16)], axis=1)
        ln_g, ln_b = ln_mix_g[i].reshape(1, D), ln_mix_b[i].reshape(1, D)
        if i == 0:
            inputs = dict(first_inputs=(head, x_prompt.reshape(BATCH * SEQ, D)))
        else:
            inputs = dict(fused_inputs=(slot, h1, r_all, ys, ln_ffn_g[i - 1].reshape(1, D),
                                        ln_ffn_b[i - 1].reshape(1, D)))
        if i % 2 == 0:
            ps = _state_presum(j, state_pool, pool_coef)
            outs = _pool_layer(ps, w_pool[j].astype(jnp.bfloat16), pool_scale[j].reshape(1, D),
                               ln_g, ln_b, wr, **inputs)
            if i == 0:
                tail, x_s = x_prompt[:, SEQ - POOL_STATE:], x_samp2d
            else:
                x_s, tail16, outs = outs[0], outs[1], outs[2:]
                tail = tail16[:, POOL_CARRY - POOL_STATE:]
            h1, r_all, cnt, h1p, rt = outs
            new_pool_p.append(tail)
            new_pool_s.append(jnp.concatenate([state_pool[j][:, 1:], x_s[:, None, :]], axis=1))
        else:
            vs = _state_presum(j, state_conv, w_dw[j][:CONV_STATE])
            outs = _conv_layer(
                vs, w_glu[j].astype(jnp.bfloat16), b_glu[j].reshape(1, 2 * D), w_dw[j],
                b_dw[j].reshape(1, D), conv_ln_g[j].reshape(1, D), conv_ln_b[j].reshape(1, D),
                w_pw[j].astype(jnp.bfloat16), b_pw[j].reshape(1, D), ln_g, ln_b, wr, **inputs)
            h1, r_all, cnt, h1p, rt, ust, u_s = outs if i == 0 else outs[2:]
            new_conv_p.append(ust)
            new_conv_s.append(jnp.concatenate([state_conv[j][:, 1:], u_s[:, None, :]], axis=1))
        block_expert, nused, base, limit, order, slot = _routing_metadata(rt, cnt)
        ys = _expert_layer(i, block_expert, nused, base, limit, order, h1p, w_gate, w_up, w_down)

    y_prompt, y_samp = _final_combine(slot, h1, r_all, ys, ln_ffn_g[DEPTH - 1].reshape(1, D),
                                      ln_ffn_b[DEPTH - 1].reshape(1, D))
    return (y_prompt.reshape(BATCH, SEQ, D), y_samp.reshape(DEC_BATCH, 1, D),
            jnp.stack(new_pool_p), jnp.stack(new_pool_s), jnp.stack(new_conv_p), jnp.stack(new_conv_s))
```

```python
import functools

import jax
import jax.numpy as jnp
from jax import lax
from jax.experimental import pallas as pl
from jax.experimental.pallas import tpu as pltpu

D = 1024
BATCH = 8
SEQ = 2048
DEPTH = 4
DEC_BATCH = 128
N_META = 16
POOL_WINDOWS = (2, 4, 8, 16)
POOL_GROUP_DIM = D // len(POOL_WINDOWS)
POOL_STATE = max(POOL_WINDOWS) - 1
CONV_WIDTH = 31
CONV_STATE = CONV_WIDTH - 1
N_GROUPS = 4
EXPERTS_PER_GROUP = 8
N_EXPERTS = N_GROUPS * EXPERTS_PER_GROUP
TOP_K = 2
D_EXPERT = D // 2
ALPHA = (2.0 * DEPTH) ** 0.25
LN_EPS = 1e-5

LANES = 128
SUBLANES = 8
N_HEAD = BATCH * N_META
TB = 256
NTOK = N_HEAD + DEC_BATCH + BATCH * SEQ
N_STEP = NTOK // TB
CHUNKS = SEQ // TB
POOL_CARRY = 16
POOL_LEAD = 8
CONV_CARRY = 32
CONV_RB = 128
CONV_CB = 128
BM = 512
N_ASSIGN = NTOK * TOP_K
RANK_RANGE = 1 << 16
NB = N_ASSIGN // BM + N_EXPERTS
N_SLOT = NB * BM
VMEM_LIMIT = 48 * 1024 * 1024
EXPERT_VMEM_LIMIT = 56 * 1024 * 1024
NEG = -1e30

assert D == SUBLANES * LANES and N_HEAD + DEC_BATCH == TB and SEQ % TB == 0
assert N_ASSIGN % BM == 0 and TB % CONV_RB == 0 and N_META == POOL_CARRY and CONV_STATE <= TB
assert RANK_RANGE > N_ASSIGN and N_EXPERTS * RANK_RANGE < 2 ** 31 and TOP_K == 2


def _tile_load(ref, n):
    return jnp.concatenate([ref[pl.ds(s, n, stride=SUBLANES), :] for s in range(SUBLANES)], axis=1)


def _tile_store(ref, val, n):
    for s in range(SUBLANES):
        ref[pl.ds(s, n, stride=SUBLANES), :] = val[:, s * LANES:(s + 1) * LANES]


def _layer_norm(x, g, b):
    mu = jnp.mean(x, axis=-1, keepdims=True)
    xc = x - mu
    var = jnp.mean(xc * xc, axis=-1, keepdims=True)
    return xc * lax.rsqrt(var + LN_EPS) * g + b


PACK_ROWS = D // 2 // LANES


def _pack_bf16(h_bf16, hp_ref):
    bits = pltpu.bitcast(h_bf16.astype(jnp.float32), jnp.uint32)
    words = jnp.bitwise_or(lax.shift_right_logical(bits[:, :D // 2], jnp.uint32(16)),
                           jnp.bitwise_and(bits[:, D // 2:], jnp.uint32(0xFFFF0000)))
    for s in range(PACK_ROWS):
        hp_ref[:, s, :] = words[:, s * LANES:(s + 1) * LANES]


def _unpack_bf16(words):
    lo = pltpu.bitcast(lax.shift_left(words, jnp.uint32(16)), jnp.float32)
    hi = pltpu.bitcast(jnp.bitwise_and(words, jnp.uint32(0xFFFF0000)), jnp.float32)
    return jnp.concatenate([lo, hi], axis=1).astype(jnp.bfloat16)


def _route_init(cnt_ref, ltri_ref):
    @pl.when(pl.program_id(0) == 0)
    def _():
        cnt_ref[...] = jnp.zeros((1, LANES), jnp.float32)
        row = lax.broadcasted_iota(jnp.int32, (TB, TB), 0)
        col = lax.broadcasted_iota(jnp.int32, (TB, TB), 1)
        ltri_ref[...] = jnp.where(col < row, 1.0, 0.0).astype(jnp.bfloat16)


def _residual_route(x, m, g_ref, b_ref, wr_ref, h_ref, r_ref, cnt_out_ref, hp_ref, rt_ref, cnt_ref, ltri_ref):
    h = _layer_norm(ALPHA * x + m, g_ref[...], b_ref[...])
    h_ref[...] = h
    h_hi = h.astype(jnp.bfloat16)
    _pack_bf16(h_hi, hp_ref)
    h_lo = (h - h_hi.astype(jnp.float32)).astype(jnp.bfloat16)
    hi_both = jnp.dot(h_hi, wr_ref[...], preferred_element_type=jnp.float32)
    lo_hi = jnp.dot(h_lo, wr_ref[:, 0:LANES], preferred_element_type=jnp.float32)
    logits = hi_both[:, 0:LANES] + (hi_both[:, LANES:2 * LANES] + lo_hi)
    lane = lax.broadcasted_iota(jnp.int32, logits.shape, 1)
    lanef = lane.astype(jnp.float32)
    big = jnp.float32(1e9)
    lg = jnp.where(lane < N_GROUPS, logits, NEG)
    mg = jnp.max(lg, axis=1, keepdims=True)
    gidx = jnp.min(jnp.where(lg == mg, lanef, big), axis=1, keepdims=True)
    p_grp = 1.0 / jnp.sum(jnp.where(lane < N_GROUPS, jnp.exp(lg - mg), 0.0), axis=1, keepdims=True)
    lo = N_GROUPS + EXPERTS_PER_GROUP * gidx
    le = jnp.where(lanef >= lo, jnp.where(lanef < lo + EXPERTS_PER_GROUP, logits, NEG), NEG)
    m1 = jnp.max(le, axis=1, keepdims=True)
    i1 = jnp.min(jnp.where(le == m1, lanef, big), axis=1, keepdims=True)
    le2 = jnp.where(lanef == i1, NEG, le)
    m2 = jnp.max(le2, axis=1, keepdims=True)
    i2 = jnp.min(jnp.where(le2 == m2, lanef, big), axis=1, keepdims=True)
    ratio = jnp.exp(m2 - m1)
    g1 = p_grp / (1.0 + ratio)
    g2 = g1 * ratio
    e1 = i1 - N_GROUPS
    e2 = i2 - N_GROUPS

    is1 = lanef == e1
    is2 = lanef == e2
    onehot = jnp.where(is1, 1.0, jnp.where(is2, 1.0, 0.0))
    before = jnp.dot(ltri_ref[...], onehot.astype(jnp.bfloat16),
                     preferred_element_type=jnp.float32) + cnt_ref[...]
    rank1 = jnp.sum(jnp.where(is1, before, 0.0), axis=1, keepdims=True)
    rank2 = jnp.sum(jnp.where(is2, before, 0.0), axis=1, keepdims=True)
    cnt = cnt_ref[...] + jnp.sum(onehot, axis=0, keepdims=True)
    cnt_ref[...] = cnt
    cnt_out_ref[...] = cnt

    packed = jnp.where(lane == 0, g1,
                       jnp.where(lane == 1, g2,
                                 jnp.where(lane == 2, e1,
                                           jnp.where(lane == 3, e2,
                                                     jnp.where(lane == 4, rank1,
                                                               jnp.where(lane == 5, rank2, 0.0))))))
    r_ref[...] = packed
    rt_ref[...] = jnp.transpose(packed)[0:SUBLANES, :]


def _seq_of_step(s):
    sm1 = jnp.maximum(s - 1, 0)
    return lax.shift_right_logical(sm1, CHUNKS.bit_length() - 1), jnp.bitwise_and(sm1, CHUNKS - 1)


assert CHUNKS & (CHUNKS - 1) == 0


PRESUM_ROWS = 32


def _presum_kernel(st_ref, coef_ref, o_ref):
    coef = coef_ref[...]
    for n in range(PRESUM_ROWS):
        o_ref[n:n + 1, :] = jnp.sum(st_ref[0, n] * coef, axis=0, keepdims=True)


def _state_presum(j, state, coef):
    _, n, k, _ = state.shape
    return pl.pallas_call(
        _presum_kernel,
        grid=(n // PRESUM_ROWS,),
        in_specs=[pl.BlockSpec((1, PRESUM_ROWS, k, D), lambda i: (j, i, 0, 0)),
                  pl.BlockSpec((k, D), lambda i: (0, 0))],
        out_specs=pl.BlockSpec((PRESUM_ROWS, D), lambda i: (i, 0)),
        out_shape=jax.ShapeDtypeStruct((n, D), jnp.float32),
        compiler_params=pltpu.CompilerParams(dimension_semantics=("arbitrary",),
                                             vmem_limit_bytes=VMEM_LIMIT),
        name="state_presum",
    )(state, coef)


def _state_shift_kernel(st_ref, new_ref, o_ref):
    k = st_ref.shape[2]
    for n in range(PRESUM_ROWS):
        o_ref[0, n, 0:k - 1, :] = st_ref[0, n, 1:k, :]
        o_ref[0, n, k - 1:k, :] = new_ref[0, n:n + 1, :]


def _state_shift(state, new_rows):
    nj, n, k, _ = state.shape
    return pl.pallas_call(
        _state_shift_kernel,
        grid=(nj, n // PRESUM_ROWS),
        in_specs=[pl.BlockSpec((1, PRESUM_ROWS, k, D), lambda j, i: (j, i, 0, 0)),
                  pl.BlockSpec((1, PRESUM_ROWS, D), lambda j, i: (j, i, 0))],
        out_specs=pl.BlockSpec((1, PRESUM_ROWS, k, D), lambda j, i: (j, i, 0, 0)),
        out_shape=jax.ShapeDtypeStruct(state.shape, jnp.float32),
        compiler_params=pltpu.CompilerParams(dimension_semantics=("arbitrary", "arbitrary"),
                                             vmem_limit_bytes=VMEM_LIMIT),
        name="state_shift",
    )(state, new_rows)


POOL_X0 = POOL_LEAD + POOL_CARRY


def _window_sums(ext_ref, lvl2_ref, lvl4_ref, lvl8_ref):
    n = POOL_CARRY + TB
    gd = POOL_GROUP_DIM

    def doubled(src_ref, shift, c0):
        return src_ref[POOL_LEAD:POOL_LEAD + n, c0:] + src_ref[POOL_LEAD - shift:POOL_LEAD - shift + n, c0:]

    lvl2_ref[POOL_LEAD:POOL_LEAD + n, :] = doubled(ext_ref, 1, 0)
    lvl4_ref[POOL_LEAD:POOL_LEAD + n, :] = doubled(lvl2_ref, 2, gd)
    lvl8_ref[POOL_LEAD:POOL_LEAD + n, :] = doubled(lvl4_ref, 4, gd)
    s16 = (lvl8_ref[POOL_X0:POOL_X0 + TB, gd:2 * gd] +
           lvl8_ref[POOL_X0 - 8:POOL_X0 - 8 + TB, gd:2 * gd])
    return (lvl2_ref[POOL_X0:POOL_X0 + TB, 0:gd], lvl4_ref[POOL_X0:POOL_X0 + TB, 0:gd],
            lvl8_ref[POOL_X0:POOL_X0 + TB, 0:gd], s16)


assert POOL_WINDOWS == (2, 4, 8, 16) and POOL_LEAD >= 4 and POOL_LEAD % SUBLANES == 0


def _pool_kernel(first_layer, head_ref, x_ref, ps_ref, wp_ref, sc_ref, g_ref, b_ref, wr_ref,
                 h_ref, r_ref, cnt_out_ref, hp_ref, rt_ref, ext_ref, m_ref, carry_ref, cnt_ref, ltri_ref,
                 lvl2_ref, lvl4_ref, lvl8_ref, start_next=lambda: None):
    s = pl.program_id(0)
    seq, chunk = _seq_of_step(s)
    _route_init(cnt_ref, ltri_ref)
    route_refs = (g_ref, b_ref, wr_ref, h_ref, r_ref, cnt_out_ref, hp_ref, rt_ref, cnt_ref, ltri_ref)

    @pl.when(s == 0)
    def _():
        ext_ref[0:POOL_LEAD, :] = jnp.zeros((POOL_LEAD, D), jnp.float32)
        lvl2_ref[0:POOL_LEAD, :] = jnp.zeros((POOL_LEAD, D), jnp.float32)
        lvl4_ref[0:POOL_LEAD, :] = jnp.zeros((POOL_LEAD, D - POOL_GROUP_DIM), jnp.float32)
        x = head_ref[...] if first_layer else x_ref[...]
        start_next()
        ext_ref[POOL_LEAD:POOL_X0, :] = jnp.zeros((POOL_CARRY, D), jnp.float32)
        ext_ref[POOL_X0:POOL_X0 + TB, :] = x
        carry_ref[...] = x[0:N_HEAD, :]
        pos = jnp.bitwise_and(lax.broadcasted_iota(jnp.int32, (N_HEAD, 1), 0), N_META - 1)
        for g, w in enumerate(POOL_WINDOWS):
            lo, hi = g * POOL_GROUP_DIM, (g + 1) * POOL_GROUP_DIM
            xm = x[0:N_HEAD, lo:hi]
            acc = xm
            for k in range(1, w):
                acc = acc + jnp.where(pos >= k, ext_ref[POOL_X0 - k:POOL_X0 - k + N_HEAD, lo:hi], 0.0)
            cnt = jnp.minimum(pos + 1, w).astype(jnp.float32)
            d_meta = acc / cnt - xm
            xs = x[N_HEAD:TB, lo:hi]
            d_samp = (ps_ref[:, lo:hi] + xs) / float(w) - xs
            diff = jnp.concatenate([d_meta, d_samp], axis=0)
            y = jnp.dot(diff.astype(jnp.bfloat16), wp_ref[g], preferred_element_type=jnp.float32)
            m_ref[:, lo:hi] = y * sc_ref[:, lo:hi]
        _residual_route(x, m_ref[...], *route_refs)

    @pl.when(s > 0)
    def _():
        @pl.when(chunk == 0)
        def _():
            ext_ref[POOL_LEAD:POOL_X0, :] = carry_ref[pl.ds(pl.multiple_of(seq * N_META, N_META), N_META), :]

        x = x_ref[...]
        start_next()
        ext_ref[POOL_X0:POOL_X0 + TB, :] = x
        sums = _window_sums(ext_ref, lvl2_ref, lvl4_ref, lvl8_ref)
        ms = []
        for g, w in enumerate(POOL_WINDOWS):
            lo, hi = g * POOL_GROUP_DIM, (g + 1) * POOL_GROUP_DIM
            diff = sums[g] / float(w) - x[:, lo:hi]
            y = jnp.dot(diff.astype(jnp.bfloat16), wp_ref[g], preferred_element_type=jnp.float32)
            ms.append(y * sc_ref[:, lo:hi])
        _residual_route(x, jnp.concatenate(ms, axis=1), *route_refs)
        ext_ref[POOL_LEAD:POOL_X0, :] = ext_ref[POOL_LEAD + TB:POOL_X0 + TB, :]


assert N_META >= max(POOL_WINDOWS)


def _const_spec(shape):
    nd = len(shape)
    return pl.BlockSpec(shape, lambda s, *_: (0,) * nd)


_MIXER_OUT_SPECS = [pl.BlockSpec((TB, D), lambda s, *_: (s, 0)),
                    pl.BlockSpec((TB, LANES), lambda s, *_: (s, 0)),
                    pl.BlockSpec((1, LANES), lambda s, *_: (0, 0)),
                    pl.BlockSpec((TB, PACK_ROWS, LANES), lambda s, *_: (s, 0, 0)),
                    pl.BlockSpec((SUBLANES, TB), lambda s, *_: (s, 0))]
_MIXER_OUT_SHAPES = [jax.ShapeDtypeStruct((NTOK, D), jnp.float32),
                     jax.ShapeDtypeStruct((NTOK, LANES), jnp.float32),
                     jax.ShapeDtypeStruct((1, LANES), jnp.float32),
                     jax.ShapeDtypeStruct((NTOK, PACK_ROWS, LANES), jnp.uint32),
                     jax.ShapeDtypeStruct((N_STEP * SUBLANES, TB), jnp.float32)]


def _combine_rows(slot_ref, h_ref, r_ref, ys_hbm, g_ref, b_ref, ybuf, sem):
    i = pl.program_id(0)
    n = pl.num_programs(0)

    def start(step, buf):
        base = step * TB
        _tile_gather_start(ys_hbm, ybuf.at[buf], sem.at[buf],
                           lambda r: slot_ref[(r % TOP_K) * NTOK + base + r // TOP_K], TOP_K * TB,
                           dst_tile=lambda r: (r % TOP_K) * TB + r // TOP_K)

    @pl.when(i == 0)
    def _():
        start(0, 0)

    buf = i % 2
    _tile_gather_wait(ys_hbm, ybuf.at[buf], sem.at[buf], TOP_K * TB)
    yb = ybuf.at[buf]
    y0 = _tile_load(yb.at[pl.ds(0, TB * SUBLANES), :], TB)
    y1 = _tile_load(yb.at[pl.ds(TB * SUBLANES, TB * SUBLANES), :], TB)
    r = r_ref[...]
    rows = _layer_norm(ALPHA * h_ref[...] + (r[:, 0:1] * y0 + r[:, 1:2] * y1), g_ref[...], b_ref[...])

    def drain():
        @pl.when(i == n - 1)
        def _():
            _tile_gather_wait(ys_hbm, ybuf.at[1 - buf], sem.at[1 - buf], TOP_K * TB)

    return rows, lambda: start(jnp.minimum(i + 1, n - 1), 1 - buf), drain


_COMBINE_SCRATCH = [pltpu.VMEM((2, TOP_K * TB * SUBLANES, LANES), jnp.float32),
                    pltpu.SemaphoreType.DMA((2,))]


def _fused_mixer_kernel(body, n_in, slot_ref, hprev_ref, rprev_ref, ys_hbm, fg_ref, fb_ref, *rest):
    mixer_in = rest[:n_in]
    xs_out_ref, tail_ref = rest[n_in:n_in + 2]
    x_scr, ybuf, sem = rest[-3:]
    i = pl.program_id(0)
    x, start_next, drain = _combine_rows(slot_ref, hprev_ref, rprev_ref, ys_hbm, fg_ref, fb_ref, ybuf, sem)
    x_scr[...] = x

    @pl.when(i == 0)
    def _():
        xs_out_ref[...] = x[N_HEAD:TB, :]

    _, chunk = _seq_of_step(i)

    @pl.when(jnp.logical_and(i > 0, chunk == CHUNKS - 1))
    def _():
        tail_ref[0] = x[TB - POOL_CARRY:TB, :]

    body(False, None, x_scr, *mixer_in, *rest[n_in + 2:-3], start_next=start_next)
    drain()


def _mixer_call(name, body, mixer_in_specs, extra_out_specs, extra_out_shapes, scratch_shapes, mixer_inputs,
                first_inputs=None, fused_inputs=None):
    out_specs = _MIXER_OUT_SPECS + extra_out_specs
    out_shape = _MIXER_OUT_SHAPES + extra_out_shapes
    params = pltpu.CompilerParams(dimension_semantics=("arbitrary",), vmem_limit_bytes=VMEM_LIMIT)
    if fused_inputs is None:
        return pl.pallas_call(
            functools.partial(body, True),
            grid=(N_STEP,),
            in_specs=[_const_spec((TB, D)),
                      pl.BlockSpec((TB, D), lambda s, *_: (jnp.maximum(s - 1, 0), 0))] + mixer_in_specs,
            out_specs=out_specs, out_shape=out_shape, scratch_shapes=scratch_shapes,
            compiler_params=params, name=name,
        )(*first_inputs, *mixer_inputs)
    seq_block = lambda s, *_: (jnp.maximum(s - 1, 0) // CHUNKS, 0, 0)
    grid_spec = pltpu.PrefetchScalarGridSpec(
        num_scalar_prefetch=1,
        grid=(N_STEP,),
        in_specs=[pl.BlockSpec((TB, D), lambda s, *_: (s, 0)),
                  pl.BlockSpec((TB, LANES), lambda s, *_: (s, 0)),
                  pl.BlockSpec(memory_space=pl.ANY),
                  _const_spec((1, D)), _const_spec((1, D))] + mixer_in_specs,
        out_specs=[_const_spec((DEC_BATCH, D)), pl.BlockSpec((1, POOL_CARRY, D), seq_block)] + out_specs,
        scratch_shapes=scratch_shapes + [pltpu.VMEM((TB, D), jnp.float32)] + _COMBINE_SCRATCH,
    )
    return pl.pallas_call(
        functools.partial(_fused_mixer_kernel, body, len(mixer_in_specs)),
        grid_spec=grid_spec,
        out_shape=[jax.ShapeDtypeStruct((DEC_BATCH, D), jnp.float32),
                   jax.ShapeDtypeStruct((BATCH, POOL_CARRY, D), jnp.float32)] + out_shape,
        compiler_params=params, name=name + "_fused",
    )(*fused_inputs, *mixer_inputs)


def _pool_layer(ps, wp_bf, scale, ln_g, ln_b, wr, **inputs):
    ext_rows = POOL_X0 + TB
    return _mixer_call(
        "pool_mixer", _pool_kernel,
        [_const_spec((DEC_BATCH, D)),
         _const_spec((len(POOL_WINDOWS), POOL_GROUP_DIM, POOL_GROUP_DIM)),
         _const_spec((1, D)), _const_spec((1, D)), _const_spec((1, D)),
         _const_spec((D, 2 * LANES))],
        [], [],
        [pltpu.VMEM((ext_rows, D), jnp.float32),
         pltpu.VMEM((TB, D), jnp.float32),
         pltpu.VMEM((N_HEAD, D), jnp.float32),
         pltpu.VMEM((1, LANES), jnp.float32),
         pltpu.VMEM((TB, TB), jnp.bfloat16),
         pltpu.VMEM((ext_rows, D), jnp.float32),
         pltpu.VMEM((ext_rows, D - POOL_GROUP_DIM), jnp.float32),
         pltpu.VMEM((ext_rows, D - 2 * POOL_GROUP_DIM), jnp.float32)],
        (ps, wp_bf, scale, ln_g, ln_b, wr), **inputs)


def _depthwise_conv(ext_ref, wdw_ref, v_ref):
    base = CONV_CARRY - CONV_STATE
    for cb in range(D // CONV_CB):
        lo, hi = cb * CONV_CB, (cb + 1) * CONV_CB

        def body(i, carry, lo=lo, hi=hi):
            r0 = pl.multiple_of(i * CONV_RB, CONV_RB)
            sub = ext_ref.at[pl.ds(r0, CONV_RB + CONV_CARRY), :]
            v = None
            for r in range(SUBLANES):
                qs = [q for q in range((base + CONV_WIDTH) // SUBLANES + 1)
                      if 0 <= SUBLANES * q + r - base < CONV_WIDTH]
                z0 = SUBLANES * qs[0] + r
                z = sub[z0:SUBLANES * qs[-1] + r + CONV_RB, lo:hi]
                p = None
                for q in qs:
                    k = SUBLANES * q + r - base
                    off = SUBLANES * (q - qs[0])
                    term = wdw_ref[k:k + 1, lo:hi] * z[off:off + CONV_RB]
                    p = term if p is None else p + term
                v = p if v is None else v + p
            v_ref[pl.ds(r0, CONV_RB), lo:hi] = v
            return carry
        lax.fori_loop(0, TB // CONV_RB, body, 0)


def _conv_kernel(first_layer, head_ref, x_ref, vs_ref, wglu_ref, bglu_ref, wdw_ref, bdw_ref, lg_ref, lb_ref,
                 wpw_ref, bpw_ref, g_ref, b_ref, wr_ref, h_ref, r_ref, cnt_out_ref, hp_ref, rt_ref, ust_ref, us_ref,
                 ext_ref, v_ref, carry_ref, cnt_ref, ltri_ref, start_next=lambda: None):
    s = pl.program_id(0)
    seq, chunk = _seq_of_step(s)
    _route_init(cnt_ref, ltri_ref)
    route_refs = (g_ref, b_ref, wr_ref, h_ref, r_ref, cnt_out_ref, hp_ref, rt_ref, cnt_ref, ltri_ref)

    def glu(x):
        hh = jnp.dot(x.astype(jnp.bfloat16), wglu_ref[...], preferred_element_type=jnp.float32) + bglu_ref[...]
        return hh[:, :D] * jax.nn.sigmoid(hh[:, D:])

    def finish(x, v):
        v = _layer_norm(v + bdw_ref[...], lg_ref[...], lb_ref[...])
        v = v * jax.nn.sigmoid(v)
        m = jnp.dot(v.astype(jnp.bfloat16), wpw_ref[...], preferred_element_type=jnp.float32) + bpw_ref[...]
        _residual_route(x, m, *route_refs)

    @pl.when(s == 0)
    def _():
        x = head_ref[...] if first_layer else x_ref[...]
        start_next()
        u = glu(x)
        um = u[0:N_HEAD, :]
        us = u[N_HEAD:TB, :]
        carry_ref[...] = um
        us_ref[...] = us
        ext_ref[0:N_META, :] = jnp.zeros((N_META, D), jnp.float32)
        ext_ref[N_META:N_META + N_HEAD, :] = um
        pos = jnp.bitwise_and(lax.broadcasted_iota(jnp.int32, (N_HEAD, 1), 0), N_META - 1)
        acc = wdw_ref[CONV_WIDTH - 1:CONV_WIDTH, :] * um
        for d in range(1, N_META):
            k = CONV_WIDTH - 1 - d
            acc = acc + wdw_ref[k:k + 1, :] * jnp.where(pos >= d, ext_ref[N_META - d:N_META - d + N_HEAD, :], 0.0)
        v_samp = vs_ref[...] + wdw_ref[CONV_WIDTH - 1:CONV_WIDTH, :] * us
        finish(x, jnp.concatenate([acc, v_samp], axis=0))

    @pl.when(s > 0)
    def _():
        @pl.when(chunk == 0)
        def _():
            ext_ref[0:CONV_CARRY - N_META, :] = jnp.zeros((CONV_CARRY - N_META, D), jnp.float32)
            ext_ref[CONV_CARRY - N_META:CONV_CARRY, :] = carry_ref[
                pl.ds(pl.multiple_of(seq * N_META, N_META), N_META), :]

        x = x_ref[...]
        start_next()
        ext_ref[CONV_CARRY:CONV_CARRY + TB, :] = glu(x)
        _depthwise_conv(ext_ref, wdw_ref, v_ref)
        finish(x, v_ref[...])
        ext_ref[0:CONV_CARRY, :] = ext_ref[TB:TB + CONV_CARRY, :]

        @pl.when(chunk == CHUNKS - 1)
        def _():
            ust_ref[0] = ext_ref[CONV_CARRY + TB - CONV_STATE:CONV_CARRY + TB, :]


assert N_META <= CONV_STATE <= CONV_CARRY


def _conv_layer(vs, wglu_bf, bglu, wdw, bdw, cln_g, cln_b, wpw_bf, bpw, ln_g, ln_b, wr, **inputs):
    return _mixer_call(
        "conv_mixer", _conv_kernel,
        [_const_spec((DEC_BATCH, D)),
         _const_spec((D, 2 * D)), _const_spec((1, 2 * D)),
         _const_spec((CONV_WIDTH, D)), _const_spec((1, D)),
         _const_spec((1, D)), _const_spec((1, D)),
         _const_spec((D, D)), _const_spec((1, D)),
         _const_spec((1, D)), _const_spec((1, D)),
         _const_spec((D, 2 * LANES))],
        [pl.BlockSpec((1, CONV_STATE, D), lambda s, *_: (jnp.maximum(s - 1, 0) // CHUNKS, 0, 0)),
         _const_spec((DEC_BATCH, D))],
        [jax.ShapeDtypeStruct((BATCH, CONV_STATE, D), jnp.float32),
         jax.ShapeDtypeStruct((DEC_BATCH, D), jnp.float32)],
        [pltpu.VMEM((CONV_CARRY + TB, D), jnp.float32),
         pltpu.VMEM((TB, D), jnp.float32),
         pltpu.VMEM((N_HEAD, D), jnp.float32),
         pltpu.VMEM((1, LANES), jnp.float32),
         pltpu.VMEM((TB, TB), jnp.bfloat16)],
        (vs, wglu_bf, bglu, wdw, bdw, cln_g, cln_b, wpw_bf, bpw, ln_g, ln_b, wr), **inputs)


def _tile_gather_start(src_hbm, dst, sem, row_index, n_rows, dst_tile=lambda r: r):
    for r in range(n_rows):
        tok = row_index(r)
        pltpu.make_async_copy(src_hbm.at[pl.ds(pl.multiple_of(tok * SUBLANES, SUBLANES), SUBLANES), :],
                              dst.at[pl.ds(dst_tile(r) * SUBLANES, SUBLANES), :], sem).start(priority=r % 2)


def _tile_gather_wait(src_hbm, dst, sem, n_rows):
    pltpu.make_async_copy(src_hbm.at[pl.ds(0, n_rows * SUBLANES), :], dst, sem).wait()


def _expert_kernel(be_ref, nused_ref, base_ref, limit_ref, order_ref, hp_ref, wg_ref, wu_ref, wd_ref, ys_ref,
                   xbuf_even, xbuf_odd, wgb, wub, wdb):
    i = pl.program_id(0)
    nused = nused_ref[0]

    def gather(blk, xbuf):
        b0 = base_ref[blk]
        lim = limit_ref[blk]
        for r in range(BM):
            a = order_ref[jnp.minimum(b0 + r, lim)]
            xbuf[r] = hp_ref[jnp.where(a >= NTOK, a - NTOK, a)]

    def block(cur, nxt):
        gather(jnp.minimum(i + 1, jnp.maximum(nused - 1, 0)), nxt)
        words = jnp.concatenate([cur[:, s, :] for s in range(PACK_ROWS)], axis=1)
        xb = _unpack_bf16(words)
        gate = jnp.dot(xb, wgb[...], preferred_element_type=jnp.float32)
        up = jnp.dot(xb, wub[...], preferred_element_type=jnp.float32)
        hid = gate * jax.nn.sigmoid(gate) * up
        y = jnp.dot(hid.astype(jnp.bfloat16), wdb[...], preferred_element_type=jnp.float32)
        _tile_store(ys_ref, y, BM)

    @pl.when(i == 0)
    def _():
        gather(0, xbuf_even)

    @pl.when(jnp.logical_and(i < nused, jnp.logical_or(i == 0, be_ref[i] != be_ref[jnp.maximum(i - 1, 0)])))
    def _():
        wgb[...] = wg_ref[0, 0].astype(jnp.bfloat16)
        wub[...] = wu_ref[0, 0].astype(jnp.bfloat16)
        wdb[...] = wd_ref[0, 0].astype(jnp.bfloat16)

    odd = jnp.bitwise_and(i, 1) == 1

    @pl.when(jnp.logical_and(i < nused, jnp.logical_not(odd)))
    def _():
        block(xbuf_even, xbuf_odd)

    @pl.when(jnp.logical_and(i < nused, odd))
    def _():
        block(xbuf_odd, xbuf_even)

    @pl.when(i >= nused)
    def _():
        ys_ref[...] = jnp.zeros((BM * SUBLANES, LANES), jnp.float32)


def _expert_layer(layer, block_expert, nused, base, limit, order, h_packed, w_gate, w_up, w_down):
    def w_spec(shape):
        return pl.BlockSpec((1, 1) + shape, lambda i, be, *_: (layer, be[i], 0, 0))
    grid_spec = pltpu.PrefetchScalarGridSpec(
        num_scalar_prefetch=5,
        grid=(NB,),
        in_specs=[pl.BlockSpec((NTOK, PACK_ROWS, LANES), lambda i, *_: (0, 0, 0)),
                  w_spec((D, D_EXPERT)), w_spec((D, D_EXPERT)), w_spec((D_EXPERT, D))],
        out_specs=pl.BlockSpec((BM * SUBLANES, LANES), lambda i, *_: (i, 0)),
        scratch_shapes=[pltpu.VMEM((BM, PACK_ROWS, LANES), jnp.uint32),
                        pltpu.VMEM((BM, PACK_ROWS, LANES), jnp.uint32),
                        pltpu.VMEM((D, D_EXPERT), jnp.bfloat16),
                        pltpu.VMEM((D, D_EXPERT), jnp.bfloat16),
                        pltpu.VMEM((D_EXPERT, D), jnp.bfloat16)],
    )
    return pl.pallas_call(
        _expert_kernel,
        grid_spec=grid_spec,
        out_shape=jax.ShapeDtypeStruct((N_SLOT * SUBLANES, LANES), jnp.float32),
        compiler_params=pltpu.CompilerParams(dimension_semantics=("arbitrary",),
                                             vmem_limit_bytes=EXPERT_VMEM_LIMIT),
        name="expert_mlp",
    )(block_expert, nused, base, limit, order, h_packed, w_gate, w_up, w_down)


def _final_combine_kernel(slot_ref, h_ref, r_ref, ys_hbm, g_ref, b_ref, yp_ref, ysamp_ref, ybuf, sem):
    out, start_next, drain = _combine_rows(slot_ref, h_ref, r_ref, ys_hbm, g_ref, b_ref, ybuf, sem)
    start_next()
    yp_ref[...] = out

    @pl.when(pl.program_id(0) == 0)
    def _():
        ysamp_ref[...] = out[N_HEAD:TB, :]

    drain()


def _final_combine(slot, h, r_all, ys, ln_g, ln_b):
    grid_spec = pltpu.PrefetchScalarGridSpec(
        num_scalar_prefetch=1,
        grid=(N_STEP,),
        in_specs=[pl.BlockSpec((TB, D), lambda i, sl: (i, 0)),
                  pl.BlockSpec((TB, LANES), lambda i, sl: (i, 0)),
                  pl.BlockSpec(memory_space=pl.ANY),
                  pl.BlockSpec((1, D), lambda i, sl: (0, 0)),
                  pl.BlockSpec((1, D), lambda i, sl: (0, 0))],
        out_specs=[pl.BlockSpec((TB, D), lambda i, sl: (jnp.maximum(i - 1, 0), 0)),
                   pl.BlockSpec((DEC_BATCH, D), lambda i, sl: (0, 0))],
        scratch_shapes=_COMBINE_SCRATCH,
    )
    return pl.pallas_call(
        _final_combine_kernel,
        grid_spec=grid_spec,
        out_shape=[jax.ShapeDtypeStruct((BATCH * SEQ, D), jnp.float32),
                   jax.ShapeDtypeStruct((DEC_BATCH, D), jnp.float32)],
        compiler_params=pltpu.CompilerParams(dimension_semantics=("arbitrary",),
                                             vmem_limit_bytes=VMEM_LIMIT),
        name="moe_combine",
    )(slot, h, r_all, ys, ln_g, ln_b)


def _routing_metadata(rt, cnt):
    experts = jnp.arange(N_EXPERTS, dtype=jnp.int32)
    counts = cnt[0, :N_EXPERTS].astype(jnp.int32)
    nblk = (counts + BM - 1) // BM
    blk_end = jnp.cumsum(nblk)
    pstart = (blk_end - nblk) * BM
    starts = jnp.cumsum(counts) - counts
    nused = blk_end[-1]
    blk = jnp.arange(NB, dtype=jnp.int32)
    block_expert = jnp.sum((blk[:, None] >= blk_end[None, :]).astype(jnp.int32), axis=1)
    last_expert = jnp.max(jnp.where(nblk > 0, experts, 0))
    block_expert = jnp.where(blk < nused, block_expert, last_expert).astype(jnp.int32)
    of_block = block_expert[:, None] == experts[None, :]
    base = blk * BM + jnp.sum(jnp.where(of_block, (starts - pstart)[None, :], 0), axis=1)
    limit = jnp.sum(jnp.where(of_block, (starts + counts - 1)[None, :], 0), axis=1)
    rows = rt.reshape(N_STEP, SUBLANES, TB)
    expert = jnp.transpose(rows[:, 2:2 + TOP_K, :], (1, 0, 2)).reshape(N_ASSIGN).astype(jnp.int32)
    rank = jnp.transpose(rows[:, 2 + TOP_K:2 + 2 * TOP_K, :], (1, 0, 2)).reshape(N_ASSIGN).astype(jnp.int32)
    slot = rank + jnp.sum(jnp.where(expert[:, None] == experts[None, :], pstart[None, :], 0), axis=1)
    order = jnp.argsort(expert * RANK_RANGE + rank).astype(jnp.int32)
    return (block_expert, nused.reshape(1).astype(jnp.int32), base.astype(jnp.int32),
            limit.astype(jnp.int32), order, slot.astype(jnp.int32))


def kernel(x_prompt, x_sample, state_pool, state_conv, meta, w_pool, pool_scale, w_glu, b_glu, w_dw, b_dw, conv_ln_g, conv_ln_b, w_pw, b_pw, ln_mix_g, ln_mix_b, ln_ffn_g, ln_ffn_b, w_router_group, w_router_expert, w_gate, w_up, w_down):
    f32 = jnp.float32
    x_samp2d = x_sample.reshape(DEC_BATCH, D)
    head = jnp.concatenate([jnp.tile(meta.astype(f32), (BATCH, 1)), x_samp2d], axis=0)
    k_idx = jnp.arange(POOL_STATE)[:, None]
    win = jnp.repeat(jnp.asarray(POOL_WINDOWS), POOL_GROUP_DIM)[None, :]
    pool_coef = (k_idx >= (POOL_STATE + 1 - win)).astype(f32)

    new_pool_p, new_pool_s, new_conv_p, new_conv_s = [], [], [], []
    for i in range(DEPTH):
        j = i // 2
        wr32 = jnp.concatenate([w_router_group[i], w_router_expert[i],
                                jnp.zeros((D, LANES - N_GROUPS - N_EXPERTS), f32)], axis=1)
        wr_hi = wr32.astype(jnp.bfloat16)
        wr = jnp.concatenate([wr_hi, (wr32 - wr_hi.astype(f32)).astype(jnp.bfloat16)], axis=1)
        ln_g, ln_b = ln_mix_g[i].reshape(1, D), ln_mix_b[i].reshape(1, D)
        if i == 0:
            inputs = dict(first_inputs=(head, x_prompt.reshape(BATCH * SEQ, D)))
        else:
            inputs = dict(fused_inputs=(slot, h1, r_all, ys, ln_ffn_g[i - 1].reshape(1, D),
                                        ln_ffn_b[i - 1].reshape(1, D)))
        if i % 2 == 0:
            ps = _state_presum(j, state_pool, pool_coef)
            outs = _pool_layer(ps, w_pool[j].astype(jnp.bfloat16), pool_scale[j].reshape(1, D),
                               ln_g, ln_b, wr, **inputs)
            if i == 0:
                tail, x_s = x_prompt[:, SEQ - POOL_STATE:], x_samp2d
            else:
                x_s, tail16, outs = outs[0], outs[1], outs[2:]
                tail = tail16[:, POOL_CARRY - POOL_STATE:]
            h1, r_all, cnt, h1p, rt = outs
            new_pool_p.append(tail)
            new_pool_s.append(x_s)
        else:
            vs = _state_presum(j, state_conv, w_dw[j][:CONV_STATE])
            outs = _conv_layer(
                vs, w_glu[j].astype(jnp.bfloat16), b_glu[j].reshape(1, 2 * D), w_dw[j],
                b_dw[j].reshape(1, D), conv_ln_g[j].reshape(1, D), conv_ln_b[j].reshape(1, D),
                w_pw[j].astype(jnp.bfloat16), b_pw[j].reshape(1, D), ln_g, ln_b, wr, **inputs)
            h1, r_all, cnt, h1p, rt, ust, u_s = outs if i == 0 else outs[2:]
            new_conv_p.append(ust)
            new_conv_s.append(u_s)
        block_expert, nused, base, limit, order, slot = _routing_metadata(rt, cnt)
        ys = _expert_layer(i, block_expert, nused, base, limit, order, h1p, w_gate, w_up, w_down)

    y_prompt, y_samp = _final_combine(slot, h1, r_all, ys, ln_ffn_g[DEPTH - 1].reshape(1, D),
                                      ln_ffn_b[DEPTH - 1].reshape(1, D))
    return (y_prompt.reshape(BATCH, SEQ, D), y_samp.reshape(DEC_BATCH, 1, D),
            jnp.stack(new_pool_p), _state_shift(state_pool, jnp.stack(new_pool_s)),
            jnp.stack(new_conv_p), _state_shift(state_conv, jnp.stack(new_conv_s)))
```

```python
import functools

import jax
import jax.numpy as jnp
from jax import lax
from jax.experimental import pallas as pl
from jax.experimental.pallas import tpu as pltpu

D = 1024
BATCH = 8
SEQ = 2048
DEPTH = 4
DEC_BATCH = 128
N_META = 16
POOL_WINDOWS = (2, 4, 8, 16)
POOL_GROUP_DIM = D // len(POOL_WINDOWS)
POOL_STATE = max(POOL_WINDOWS) - 1
CONV_WIDTH = 31
CONV_STATE = CONV_WIDTH - 1
N_GROUPS = 4
EXPERTS_PER_GROUP = 8
N_EXPERTS = N_GROUPS * EXPERTS_PER_GROUP
TOP_K = 2
D_EXPERT = D // 2
ALPHA = (2.0 * DEPTH) ** 0.25
LN_EPS = 1e-5

LANES = 128
SUBLANES = 8
N_HEAD = BATCH * N_META
TB = 256
NTOK = N_HEAD + DEC_BATCH + BATCH * SEQ
N_STEP = NTOK // TB
CHUNKS = SEQ // TB
POOL_CARRY = 16
POOL_LEAD = 8
CONV_CARRY = 32
CONV_RB = 128
CONV_CB = 128
BM = 512
N_ASSIGN = NTOK * TOP_K
RANK_RANGE = 1 << 16
NB = N_ASSIGN // BM + N_EXPERTS
N_SLOT = NB * BM
VMEM_LIMIT = 48 * 1024 * 1024
EXPERT_VMEM_LIMIT = 56 * 1024 * 1024
NEG = -1e30

assert D == SUBLANES * LANES and N_HEAD + DEC_BATCH == TB and SEQ % TB == 0
assert N_ASSIGN % BM == 0 and TB % CONV_RB == 0 and N_META == POOL_CARRY and CONV_STATE <= TB
assert RANK_RANGE > N_ASSIGN and N_EXPERTS * RANK_RANGE < 2 ** 31 and TOP_K == 2


def _tile_load(ref, n):
    return jnp.concatenate([ref[pl.ds(s, n, stride=SUBLANES), :] for s in range(SUBLANES)], axis=1)


def _tile_store(ref, val, n):
    for s in range(SUBLANES):
        ref[pl.ds(s, n, stride=SUBLANES), :] = val[:, s * LANES:(s + 1) * LANES]


def _layer_norm(x, g, b):
    mu = jnp.mean(x, axis=-1, keepdims=True)
    xc = x - mu
    var = jnp.mean(xc * xc, axis=-1, keepdims=True)
    return xc * lax.rsqrt(var + LN_EPS) * g + b


PACK_ROWS = D // 2 // LANES


def _pack_bf16(h_bf16, hp_ref):
    bits = pltpu.bitcast(h_bf16.astype(jnp.float32), jnp.uint32)
    words = jnp.bitwise_or(lax.shift_right_logical(bits[:, :D // 2], jnp.uint32(16)),
                           jnp.bitwise_and(bits[:, D // 2:], jnp.uint32(0xFFFF0000)))
    for s in range(PACK_ROWS):
        hp_ref[:, s, :] = words[:, s * LANES:(s + 1) * LANES]


def _unpack_bf16(words):
    lo = pltpu.bitcast(lax.shift_left(words, jnp.uint32(16)), jnp.float32)
    hi = pltpu.bitcast(jnp.bitwise_and(words, jnp.uint32(0xFFFF0000)), jnp.float32)
    return jnp.concatenate([lo, hi], axis=1).astype(jnp.bfloat16)


def _route_init(cnt_ref, ltri_ref):
    @pl.when(pl.program_id(0) == 0)
    def _():
        cnt_ref[...] = jnp.zeros((1, LANES), jnp.float32)
        row = lax.broadcasted_iota(jnp.int32, (TB, TB), 0)
        col = lax.broadcasted_iota(jnp.int32, (TB, TB), 1)
        ltri_ref[...] = jnp.where(col < row, 1.0, 0.0).astype(jnp.bfloat16)


def _residual_route(x, m, g_ref, b_ref, wr_ref, h_ref, r_ref, cnt_out_ref, hp_ref, rt_ref, cnt_ref, ltri_ref):
    h = _layer_norm(ALPHA * x + m, g_ref[...], b_ref[...])
    h_ref[...] = h
    h_hi = h.astype(jnp.bfloat16)
    _pack_bf16(h_hi, hp_ref)
    h_lo = (h - h_hi.astype(jnp.float32)).astype(jnp.bfloat16)
    hi_both = jnp.dot(h_hi, wr_ref[...], preferred_element_type=jnp.float32)
    lo_hi = jnp.dot(h_lo, wr_ref[:, 0:LANES], preferred_element_type=jnp.float32)
    logits = hi_both[:, 0:LANES] + (hi_both[:, LANES:2 * LANES] + lo_hi)
    lane = lax.broadcasted_iota(jnp.int32, logits.shape, 1)
    lanef = lane.astype(jnp.float32)
    big = jnp.float32(1e9)
    lg = jnp.where(lane < N_GROUPS, logits, NEG)
    mg = jnp.max(lg, axis=1, keepdims=True)
    gidx = jnp.min(jnp.where(lg == mg, lanef, big), axis=1, keepdims=True)
    p_grp = 1.0 / jnp.sum(jnp.where(lane < N_GROUPS, jnp.exp(lg - mg), 0.0), axis=1, keepdims=True)
    lo = N_GROUPS + EXPERTS_PER_GROUP * gidx
    le = jnp.where(lanef >= lo, jnp.where(lanef < lo + EXPERTS_PER_GROUP, logits, NEG), NEG)
    m1 = jnp.max(le, axis=1, keepdims=True)
    i1 = jnp.min(jnp.where(le == m1, lanef, big), axis=1, keepdims=True)
    le2 = jnp.where(lanef == i1, NEG, le)
    m2 = jnp.max(le2, axis=1, keepdims=True)
    i2 = jnp.min(jnp.where(le2 == m2, lanef, big), axis=1, keepdims=True)
    ratio = jnp.exp(m2 - m1)
    g1 = p_grp / (1.0 + ratio)
    g2 = g1 * ratio
    e1 = i1 - N_GROUPS
    e2 = i2 - N_GROUPS

    is1 = lanef == e1
    is2 = lanef == e2
    onehot = jnp.where(is1, 1.0, jnp.where(is2, 1.0, 0.0))
    before = jnp.dot(ltri_ref[...], onehot.astype(jnp.bfloat16),
                     preferred_element_type=jnp.float32) + cnt_ref[...]
    rank1 = jnp.sum(jnp.where(is1, before, 0.0), axis=1, keepdims=True)
    rank2 = jnp.sum(jnp.where(is2, before, 0.0), axis=1, keepdims=True)
    cnt = cnt_ref[...] + jnp.sum(onehot, axis=0, keepdims=True)
    cnt_ref[...] = cnt
    cnt_out_ref[...] = cnt

    packed = jnp.where(lane == 0, g1,
                       jnp.where(lane == 1, g2,
                                 jnp.where(lane == 2, e1,
                                           jnp.where(lane == 3, e2,
                                                     jnp.where(lane == 4, rank1,
                                                               jnp.where(lane == 5, rank2, 0.0))))))
    r_ref[...] = packed
    rt_ref[...] = jnp.transpose(packed)[0:SUBLANES, :]


def _seq_of_step(s):
    sm1 = jnp.maximum(s - 1, 0)
    return lax.shift_right_logical(sm1, CHUNKS.bit_length() - 1), jnp.bitwise_and(sm1, CHUNKS - 1)


assert CHUNKS & (CHUNKS - 1) == 0


PRESUM_ROWS = 32


def _presum_kernel(st_ref, coef_ref, o_ref):
    k = st_ref.shape[1]
    acc = coef_ref[0:1, :] * st_ref[0, 0]
    for t in range(1, k):
        acc = acc + coef_ref[t:t + 1, :] * st_ref[0, t]
    o_ref[...] = acc


def _state_presum(j, state_t, coef):
    _, k, n, _ = state_t.shape
    return pl.pallas_call(
        _presum_kernel,
        grid=(n // PRESUM_ROWS,),
        in_specs=[pl.BlockSpec((1, k, PRESUM_ROWS, D), lambda i: (j, 0, i, 0)),
                  pl.BlockSpec((k, D), lambda i: (0, 0))],
        out_specs=pl.BlockSpec((PRESUM_ROWS, D), lambda i: (i, 0)),
        out_shape=jax.ShapeDtypeStruct((n, D), jnp.float32),
        compiler_params=pltpu.CompilerParams(dimension_semantics=("arbitrary",),
                                             vmem_limit_bytes=VMEM_LIMIT),
        name="state_presum",
    )(state_t, coef)


def _state_shift_kernel(st_ref, new_ref, o_ref):
    k = st_ref.shape[1]
    for t in range(k - 1):
        o_ref[0, t] = st_ref[0, t + 1]
    o_ref[0, k - 1] = new_ref[0]


def _state_shift(state_t, new_rows):
    nj, k, n, _ = state_t.shape
    return pl.pallas_call(
        _state_shift_kernel,
        grid=(nj, n // PRESUM_ROWS),
        in_specs=[pl.BlockSpec((1, k, PRESUM_ROWS, D), lambda j, i: (j, 0, i, 0)),
                  pl.BlockSpec((1, PRESUM_ROWS, D), lambda j, i: (j, i, 0))],
        out_specs=pl.BlockSpec((1, k, PRESUM_ROWS, D), lambda j, i: (j, 0, i, 0)),
        out_shape=jax.ShapeDtypeStruct(state_t.shape, jnp.float32),
        compiler_params=pltpu.CompilerParams(dimension_semantics=("arbitrary", "arbitrary"),
                                             vmem_limit_bytes=VMEM_LIMIT),
        name="state_shift",
    )(state_t, new_rows)


POOL_X0 = POOL_LEAD + POOL_CARRY


def _window_sums(ext_ref, lvl2_ref, lvl4_ref, lvl8_ref):
    n = POOL_CARRY + TB
    gd = POOL_GROUP_DIM

    def doubled(src_ref, shift, c0):
        return src_ref[POOL_LEAD:POOL_LEAD + n, c0:] + src_ref[POOL_LEAD - shift:POOL_LEAD - shift + n, c0:]

    lvl2_ref[POOL_LEAD:POOL_LEAD + n, :] = doubled(ext_ref, 1, 0)
    lvl4_ref[POOL_LEAD:POOL_LEAD + n, :] = doubled(lvl2_ref, 2, gd)
    lvl8_ref[POOL_LEAD:POOL_LEAD + n, :] = doubled(lvl4_ref, 4, gd)
    s16 = (lvl8_ref[POOL_X0:POOL_X0 + TB, gd:2 * gd] +
           lvl8_ref[POOL_X0 - 8:POOL_X0 - 8 + TB, gd:2 * gd])
    return (lvl2_ref[POOL_X0:POOL_X0 + TB, 0:gd], lvl4_ref[POOL_X0:POOL_X0 + TB, 0:gd],
            lvl8_ref[POOL_X0:POOL_X0 + TB, 0:gd], s16)


assert POOL_WINDOWS == (2, 4, 8, 16) and POOL_LEAD >= 4 and POOL_LEAD % SUBLANES == 0


def _pool_kernel(first_layer, head_ref, x_ref, ps_ref, wp_ref, sc_ref, g_ref, b_ref, wr_ref,
                 h_ref, r_ref, cnt_out_ref, hp_ref, rt_ref, ext_ref, m_ref, carry_ref, cnt_ref, ltri_ref,
                 lvl2_ref, lvl4_ref, lvl8_ref, start_next=lambda: None):
    s = pl.program_id(0)
    seq, chunk = _seq_of_step(s)
    _route_init(cnt_ref, ltri_ref)
    route_refs = (g_ref, b_ref, wr_ref, h_ref, r_ref, cnt_out_ref, hp_ref, rt_ref, cnt_ref, ltri_ref)

    @pl.when(s == 0)
    def _():
        ext_ref[0:POOL_LEAD, :] = jnp.zeros((POOL_LEAD, D), jnp.float32)
        lvl2_ref[0:POOL_LEAD, :] = jnp.zeros((POOL_LEAD, D), jnp.float32)
        lvl4_ref[0:POOL_LEAD, :] = jnp.zeros((POOL_LEAD, D - POOL_GROUP_DIM), jnp.float32)
        x = head_ref[...] if first_layer else x_ref[...]
        start_next()
        ext_ref[POOL_LEAD:POOL_X0, :] = jnp.zeros((POOL_CARRY, D), jnp.float32)
        ext_ref[POOL_X0:POOL_X0 + TB, :] = x
        carry_ref[...] = x[0:N_HEAD, :]
        pos = jnp.bitwise_and(lax.broadcasted_iota(jnp.int32, (N_HEAD, 1), 0), N_META - 1)
        for g, w in enumerate(POOL_WINDOWS):
            lo, hi = g * POOL_GROUP_DIM, (g + 1) * POOL_GROUP_DIM
            xm = x[0:N_HEAD, lo:hi]
            acc = xm
            for k in range(1, w):
                acc = acc + jnp.where(pos >= k, ext_ref[POOL_X0 - k:POOL_X0 - k + N_HEAD, lo:hi], 0.0)
            cnt = jnp.minimum(pos + 1, w).astype(jnp.float32)
            d_meta = acc / cnt - xm
            xs = x[N_HEAD:TB, lo:hi]
            d_samp = (ps_ref[:, lo:hi] + xs) / float(w) - xs
            diff = jnp.concatenate([d_meta, d_samp], axis=0)
            y = jnp.dot(diff.astype(jnp.bfloat16), wp_ref[g], preferred_element_type=jnp.float32)
            m_ref[:, lo:hi] = y * sc_ref[:, lo:hi]
        _residual_route(x, m_ref[...], *route_refs)

    @pl.when(s > 0)
    def _():
        @pl.when(chunk == 0)
        def _():
            ext_ref[POOL_LEAD:POOL_X0, :] = carry_ref[pl.ds(pl.multiple_of(seq * N_META, N_META), N_META), :]

        x = x_ref[...]
        start_next()
        ext_ref[POOL_X0:POOL_X0 + TB, :] = x
        sums = _window_sums(ext_ref, lvl2_ref, lvl4_ref, lvl8_ref)
        ms = []
        for g, w in enumerate(POOL_WINDOWS):
            lo, hi = g * POOL_GROUP_DIM, (g + 1) * POOL_GROUP_DIM
            diff = sums[g] / float(w) - x[:, lo:hi]
            y = jnp.dot(diff.astype(jnp.bfloat16), wp_ref[g], preferred_element_type=jnp.float32)
            ms.append(y * sc_ref[:, lo:hi])
        _residual_route(x, jnp.concatenate(ms, axis=1), *route_refs)
        ext_ref[POOL_LEAD:POOL_X0, :] = ext_ref[POOL_LEAD + TB:POOL_X0 + TB, :]


assert N_META >= max(POOL_WINDOWS)


def _const_spec(shape):
    nd = len(shape)
    return pl.BlockSpec(shape, lambda s, *_: (0,) * nd)


_MIXER_OUT_SPECS = [pl.BlockSpec((TB, D), lambda s, *_: (s, 0)),
                    pl.BlockSpec((TB, LANES), lambda s, *_: (s, 0)),
                    pl.BlockSpec((1, LANES), lambda s, *_: (0, 0)),
                    pl.BlockSpec((TB, PACK_ROWS, LANES), lambda s, *_: (s, 0, 0)),
                    pl.BlockSpec((SUBLANES, TB), lambda s, *_: (s, 0))]
_MIXER_OUT_SHAPES = [jax.ShapeDtypeStruct((NTOK, D), jnp.float32),
                     jax.ShapeDtypeStruct((NTOK, LANES), jnp.float32),
                     jax.ShapeDtypeStruct((1, LANES), jnp.float32),
                     jax.ShapeDtypeStruct((NTOK, PACK_ROWS, LANES), jnp.uint32),
                     jax.ShapeDtypeStruct((N_STEP * SUBLANES, TB), jnp.float32)]


def _combine_rows(slot_ref, h_ref, r_ref, ys_hbm, g_ref, b_ref, ybuf, sem):
    i = pl.program_id(0)
    n = pl.num_programs(0)

    def start(step, buf):
        base = step * TB
        _tile_gather_start(ys_hbm, ybuf.at[buf], sem.at[buf],
                           lambda r: slot_ref[(r % TOP_K) * NTOK + base + r // TOP_K], TOP_K * TB,
                           dst_tile=lambda r: (r % TOP_K) * TB + r // TOP_K)

    @pl.when(i == 0)
    def _():
        start(0, 0)

    buf = i % 2
    _tile_gather_wait(ys_hbm, ybuf.at[buf], sem.at[buf], TOP_K * TB)
    yb = ybuf.at[buf]
    y0 = _tile_load(yb.at[pl.ds(0, TB * SUBLANES), :], TB)
    y1 = _tile_load(yb.at[pl.ds(TB * SUBLANES, TB * SUBLANES), :], TB)
    r = r_ref[...]
    rows = _layer_norm(ALPHA * h_ref[...] + (r[:, 0:1] * y0 + r[:, 1:2] * y1), g_ref[...], b_ref[...])

    def drain():
        @pl.when(i == n - 1)
        def _():
            _tile_gather_wait(ys_hbm, ybuf.at[1 - buf], sem.at[1 - buf], TOP_K * TB)

    return rows, lambda: start(jnp.minimum(i + 1, n - 1), 1 - buf), drain


_COMBINE_SCRATCH = [pltpu.VMEM((2, TOP_K * TB * SUBLANES, LANES), jnp.float32),
                    pltpu.SemaphoreType.DMA((2,))]


def _fused_mixer_kernel(body, n_in, slot_ref, hprev_ref, rprev_ref, ys_hbm, fg_ref, fb_ref, *rest):
    mixer_in = rest[:n_in]
    xs_out_ref, tail_ref = rest[n_in:n_in + 2]
    x_scr, ybuf, sem = rest[-3:]
    i = pl.program_id(0)
    x, start_next, drain = _combine_rows(slot_ref, hprev_ref, rprev_ref, ys_hbm, fg_ref, fb_ref, ybuf, sem)
    x_scr[...] = x

    @pl.when(i == 0)
    def _():
        xs_out_ref[...] = x[N_HEAD:TB, :]

    _, chunk = _seq_of_step(i)

    @pl.when(jnp.logical_and(i > 0, chunk == CHUNKS - 1))
    def _():
        tail_ref[0] = x[TB - POOL_CARRY:TB, :]

    body(False, None, x_scr, *mixer_in, *rest[n_in + 2:-3], start_next=start_next)
    drain()


def _mixer_call(name, body, mixer_in_specs, extra_out_specs, extra_out_shapes, scratch_shapes, mixer_inputs,
                first_inputs=None, fused_inputs=None):
    out_specs = _MIXER_OUT_SPECS + extra_out_specs
    out_shape = _MIXER_OUT_SHAPES + extra_out_shapes
    params = pltpu.CompilerParams(dimension_semantics=("arbitrary",), vmem_limit_bytes=VMEM_LIMIT)
    if fused_inputs is None:
        return pl.pallas_call(
            functools.partial(body, True),
            grid=(N_STEP,),
            in_specs=[_const_spec((TB, D)),
                      pl.BlockSpec((TB, D), lambda s, *_: (jnp.maximum(s - 1, 0), 0))] + mixer_in_specs,
            out_specs=out_specs, out_shape=out_shape, scratch_shapes=scratch_shapes,
            compiler_params=params, name=name,
        )(*first_inputs, *mixer_inputs)
    seq_block = lambda s, *_: (jnp.maximum(s - 1, 0) // CHUNKS, 0, 0)
    grid_spec = pltpu.PrefetchScalarGridSpec(
        num_scalar_prefetch=1,
        grid=(N_STEP,),
        in_specs=[pl.BlockSpec((TB, D), lambda s, *_: (s, 0)),
                  pl.BlockSpec((TB, LANES), lambda s, *_: (s, 0)),
                  pl.BlockSpec(memory_space=pl.ANY),
                  _const_spec((1, D)), _const_spec((1, D))] + mixer_in_specs,
        out_specs=[_const_spec((DEC_BATCH, D)), pl.BlockSpec((1, POOL_CARRY, D), seq_block)] + out_specs,
        scratch_shapes=scratch_shapes + [pltpu.VMEM((TB, D), jnp.float32)] + _COMBINE_SCRATCH,
    )
    return pl.pallas_call(
        functools.partial(_fused_mixer_kernel, body, len(mixer_in_specs)),
        grid_spec=grid_spec,
        out_shape=[jax.ShapeDtypeStruct((DEC_BATCH, D), jnp.float32),
                   jax.ShapeDtypeStruct((BATCH, POOL_CARRY, D), jnp.float32)] + out_shape,
        compiler_params=params, name=name + "_fused",
    )(*fused_inputs, *mixer_inputs)


def _pool_layer(ps, wp_bf, scale, ln_g, ln_b, wr, **inputs):
    ext_rows = POOL_X0 + TB
    return _mixer_call(
        "pool_mixer", _pool_kernel,
        [_const_spec((DEC_BATCH, D)),
         _const_spec((len(POOL_WINDOWS), POOL_GROUP_DIM, POOL_GROUP_DIM)),
         _const_spec((1, D)), _const_spec((1, D)), _const_spec((1, D)),
         _const_spec((D, 2 * LANES))],
        [], [],
        [pltpu.VMEM((ext_rows, D), jnp.float32),
         pltpu.VMEM((TB, D), jnp.float32),
         pltpu.VMEM((N_HEAD, D), jnp.float32),
         pltpu.VMEM((1, LANES), jnp.float32),
         pltpu.VMEM((TB, TB), jnp.bfloat16),
         pltpu.VMEM((ext_rows, D), jnp.float32),
         pltpu.VMEM((ext_rows, D - POOL_GROUP_DIM), jnp.float32),
         pltpu.VMEM((ext_rows, D - 2 * POOL_GROUP_DIM), jnp.float32)],
        (ps, wp_bf, scale, ln_g, ln_b, wr), **inputs)


def _depthwise_conv(ext_ref, wdw_ref, v_ref):
    base = CONV_CARRY - CONV_STATE
    for cb in range(D // CONV_CB):
        lo, hi = cb * CONV_CB, (cb + 1) * CONV_CB

        def body(i, carry, lo=lo, hi=hi):
            r0 = pl.multiple_of(i * CONV_RB, CONV_RB)
            sub = ext_ref.at[pl.ds(r0, CONV_RB + CONV_CARRY), :]
            v = None
            for r in range(SUBLANES):
                qs = [q for q in range((base + CONV_WIDTH) // SUBLANES + 1)
                      if 0 <= SUBLANES * q + r - base < CONV_WIDTH]
                z0 = SUBLANES * qs[0] + r
                z = sub[z0:SUBLANES * qs[-1] + r + CONV_RB, lo:hi]
                p = None
                for q in qs:
                    k = SUBLANES * q + r - base
                    off = SUBLANES * (q - qs[0])
                    term = wdw_ref[k:k + 1, lo:hi] * z[off:off + CONV_RB]
                    p = term if p is None else p + term
                v = p if v is None else v + p
            v_ref[pl.ds(r0, CONV_RB), lo:hi] = v
            return carry
        lax.fori_loop(0, TB // CONV_RB, body, 0)


def _conv_kernel(first_layer, head_ref, x_ref, vs_ref, wglu_ref, bglu_ref, wdw_ref, bdw_ref, lg_ref, lb_ref,
                 wpw_ref, bpw_ref, g_ref, b_ref, wr_ref, h_ref, r_ref, cnt_out_ref, hp_ref, rt_ref, ust_ref, us_ref,
                 ext_ref, v_ref, carry_ref, cnt_ref, ltri_ref, start_next=lambda: None):
    s = pl.program_id(0)
    seq, chunk = _seq_of_step(s)
    _route_init(cnt_ref, ltri_ref)
    route_refs = (g_ref, b_ref, wr_ref, h_ref, r_ref, cnt_out_ref, hp_ref, rt_ref, cnt_ref, ltri_ref)

    def glu(x):
        hh = jnp.dot(x.astype(jnp.bfloat16), wglu_ref[...], preferred_element_type=jnp.float32) + bglu_ref[...]
        return hh[:, :D] * jax.nn.sigmoid(hh[:, D:])

    def finish(x, v):
        v = _layer_norm(v + bdw_ref[...], lg_ref[...], lb_ref[...])
        v = v * jax.nn.sigmoid(v)
        m = jnp.dot(v.astype(jnp.bfloat16), wpw_ref[...], preferred_element_type=jnp.float32) + bpw_ref[...]
        _residual_route(x, m, *route_refs)

    @pl.when(s == 0)
    def _():
        x = head_ref[...] if first_layer else x_ref[...]
        start_next()
        u = glu(x)
        um = u[0:N_HEAD, :]
        us = u[N_HEAD:TB, :]
        carry_ref[...] = um
        us_ref[...] = us
        ext_ref[0:N_META, :] = jnp.zeros((N_META, D), jnp.float32)
        ext_ref[N_META:N_META + N_HEAD, :] = um
        pos = jnp.bitwise_and(lax.broadcasted_iota(jnp.int32, (N_HEAD, 1), 0), N_META - 1)
        acc = wdw_ref[CONV_WIDTH - 1:CONV_WIDTH, :] * um
        for d in range(1, N_META):
            k = CONV_WIDTH - 1 - d
            acc = acc + wdw_ref[k:k + 1, :] * jnp.where(pos >= d, ext_ref[N_META - d:N_META - d + N_HEAD, :], 0.0)
        v_samp = vs_ref[...] + wdw_ref[CONV_WIDTH - 1:CONV_WIDTH, :] * us
        finish(x, jnp.concatenate([acc, v_samp], axis=0))

    @pl.when(s > 0)
    def _():
        @pl.when(chunk == 0)
        def _():
            ext_ref[0:CONV_CARRY - N_META, :] = jnp.zeros((CONV_CARRY - N_META, D), jnp.float32)
            ext_ref[CONV_CARRY - N_META:CONV_CARRY, :] = carry_ref[
                pl.ds(pl.multiple_of(seq * N_META, N_META), N_META), :]

        x = x_ref[...]
        start_next()
        ext_ref[CONV_CARRY:CONV_CARRY + TB, :] = glu(x)
        _depthwise_conv(ext_ref, wdw_ref, v_ref)
        finish(x, v_ref[...])
        ext_ref[0:CONV_CARRY, :] = ext_ref[TB:TB + CONV_CARRY, :]

        @pl.when(chunk == CHUNKS - 1)
        def _():
            ust_ref[0] = ext_ref[CONV_CARRY + TB - CONV_STATE:CONV_CARRY + TB, :]


assert N_META <= CONV_STATE <= CONV_CARRY


def _conv_layer(vs, wglu_bf, bglu, wdw, bdw, cln_g, cln_b, wpw_bf, bpw, ln_g, ln_b, wr, **inputs):
    return _mixer_call(
        "conv_mixer", _conv_kernel,
        [_const_spec((DEC_BATCH, D)),
         _const_spec((D, 2 * D)), _const_spec((1, 2 * D)),
         _const_spec((CONV_WIDTH, D)), _const_spec((1, D)),
         _const_spec((1, D)), _const_spec((1, D)),
         _const_spec((D, D)), _const_spec((1, D)),
         _const_spec((1, D)), _const_spec((1, D)),
         _const_spec((D, 2 * LANES))],
        [pl.BlockSpec((1, CONV_STATE, D), lambda s, *_: (jnp.maximum(s - 1, 0) // CHUNKS, 0, 0)),
         _const_spec((DEC_BATCH, D))],
        [jax.ShapeDtypeStruct((BATCH, CONV_STATE, D), jnp.float32),
         jax.ShapeDtypeStruct((DEC_BATCH, D), jnp.float32)],
        [pltpu.VMEM((CONV_CARRY + TB, D), jnp.float32),
         pltpu.VMEM((TB, D), jnp.float32),
         pltpu.VMEM((N_HEAD, D), jnp.float32),
         pltpu.VMEM((1, LANES), jnp.float32),
         pltpu.VMEM((TB, TB), jnp.bfloat16)],
        (vs, wglu_bf, bglu, wdw, bdw, cln_g, cln_b, wpw_bf, bpw, ln_g, ln_b, wr), **inputs)


def _tile_gather_start(src_hbm, dst, sem, row_index, n_rows, dst_tile=lambda r: r):
    for r in range(n_rows):
        tok = row_index(r)
        pltpu.make_async_copy(src_hbm.at[pl.ds(pl.multiple_of(tok * SUBLANES, SUBLANES), SUBLANES), :],
                              dst.at[pl.ds(dst_tile(r) * SUBLANES, SUBLANES), :], sem).start(priority=r % 2)


def _tile_gather_wait(src_hbm, dst, sem, n_rows):
    pltpu.make_async_copy(src_hbm.at[pl.ds(0, n_rows * SUBLANES), :], dst, sem).wait()


def _expert_kernel(be_ref, nused_ref, base_ref, limit_ref, order_ref, hp_ref, wg_ref, wu_ref, wd_ref, ys_ref,
                   xbuf_even, xbuf_odd, wgb, wub, wdb):
    i = pl.program_id(0)
    nused = nused_ref[0]

    def gather(blk, xbuf):
        b0 = base_ref[blk]
        lim = limit_ref[blk]
        for r in range(BM):
            a = order_ref[jnp.minimum(b0 + r, lim)]
            xbuf[r] = hp_ref[jnp.where(a >= NTOK, a - NTOK, a)]

    def block(cur, nxt):
        gather(jnp.minimum(i + 1, jnp.maximum(nused - 1, 0)), nxt)
        words = jnp.concatenate([cur[:, s, :] for s in range(PACK_ROWS)], axis=1)
        xb = _unpack_bf16(words)
        gate = jnp.dot(xb, wgb[...], preferred_element_type=jnp.float32)
        up = jnp.dot(xb, wub[...], preferred_element_type=jnp.float32)
        hid = gate * jax.nn.sigmoid(gate) * up
        y = jnp.dot(hid.astype(jnp.bfloat16), wdb[...], preferred_element_type=jnp.float32)
        _tile_store(ys_ref, y, BM)

    @pl.when(i == 0)
    def _():
        gather(0, xbuf_even)

    @pl.when(jnp.logical_and(i < nused, jnp.logical_or(i == 0, be_ref[i] != be_ref[jnp.maximum(i - 1, 0)])))
    def _():
        wgb[...] = wg_ref[0, 0].astype(jnp.bfloat16)
        wub[...] = wu_ref[0, 0].astype(jnp.bfloat16)
        wdb[...] = wd_ref[0, 0].astype(jnp.bfloat16)

    odd = jnp.bitwise_and(i, 1) == 1

    @pl.when(jnp.logical_and(i < nused, jnp.logical_not(odd)))
    def _():
        block(xbuf_even, xbuf_odd)

    @pl.when(jnp.logical_and(i < nused, odd))
    def _():
        block(xbuf_odd, xbuf_even)

    @pl.when(i >= nused)
    def _():
        ys_ref[...] = jnp.zeros((BM * SUBLANES, LANES), jnp.float32)


def _expert_layer(layer, block_expert, nused, base, limit, order, h_packed, w_gate, w_up, w_down):
    def w_spec(shape):
        return pl.BlockSpec((1, 1) + shape, lambda i, be, *_: (layer, be[i], 0, 0))
    grid_spec = pltpu.PrefetchScalarGridSpec(
        num_scalar_prefetch=5,
        grid=(NB,),
        in_specs=[pl.BlockSpec((NTOK, PACK_ROWS, LANES), lambda i, *_: (0, 0, 0)),
                  w_spec((D, D_EXPERT)), w_spec((D, D_EXPERT)), w_spec((D_EXPERT, D))],
        out_specs=pl.BlockSpec((BM * SUBLANES, LANES), lambda i, *_: (i, 0)),
        scratch_shapes=[pltpu.VMEM((BM, PACK_ROWS, LANES), jnp.uint32),
                        pltpu.VMEM((BM, PACK_ROWS, LANES), jnp.uint32),
                        pltpu.VMEM((D, D_EXPERT), jnp.bfloat16),
                        pltpu.VMEM((D, D_EXPERT), jnp.bfloat16),
                        pltpu.VMEM((D_EXPERT, D), jnp.bfloat16)],
    )
    return pl.pallas_call(
        _expert_kernel,
        grid_spec=grid_spec,
        out_shape=jax.ShapeDtypeStruct((N_SLOT * SUBLANES, LANES), jnp.float32),
        compiler_params=pltpu.CompilerParams(dimension_semantics=("arbitrary",),
                                             vmem_limit_bytes=EXPERT_VMEM_LIMIT),
        name="expert_mlp",
    )(block_expert, nused, base, limit, order, h_packed, w_gate, w_up, w_down)


def _final_combine_kernel(slot_ref, h_ref, r_ref, ys_hbm, g_ref, b_ref, yp_ref, ysamp_ref, ybuf, sem):
    out, start_next, drain = _combine_rows(slot_ref, h_ref, r_ref, ys_hbm, g_ref, b_ref, ybuf, sem)
    start_next()
    yp_ref[...] = out

    @pl.when(pl.program_id(0) == 0)
    def _():
        ysamp_ref[...] = out[N_HEAD:TB, :]

    drain()


def _final_combine(slot, h, r_all, ys, ln_g, ln_b):
    grid_spec = pltpu.PrefetchScalarGridSpec(
        num_scalar_prefetch=1,
        grid=(N_STEP,),
        in_specs=[pl.BlockSpec((TB, D), lambda i, sl: (i, 0)),
                  pl.BlockSpec((TB, LANES), lambda i, sl: (i, 0)),
                  pl.BlockSpec(memory_space=pl.ANY),
                  pl.BlockSpec((1, D), lambda i, sl: (0, 0)),
                  pl.BlockSpec((1, D), lambda i, sl: (0, 0))],
        out_specs=[pl.BlockSpec((TB, D), lambda i, sl: (jnp.maximum(i - 1, 0), 0)),
                   pl.BlockSpec((DEC_BATCH, D), lambda i, sl: (0, 0))],
        scratch_shapes=_COMBINE_SCRATCH,
    )
    return pl.pallas_call(
        _final_combine_kernel,
        grid_spec=grid_spec,
        out_shape=[jax.ShapeDtypeStruct((BATCH * SEQ, D), jnp.float32),
                   jax.ShapeDtypeStruct((DEC_BATCH, D), jnp.float32)],
        compiler_params=pltpu.CompilerParams(dimension_semantics=("arbitrary",),
                                             vmem_limit_bytes=VMEM_LIMIT),
        name="moe_combine",
    )(slot, h, r_all, ys, ln_g, ln_b)


def _routing_metadata(rt, cnt):
    experts = jnp.arange(N_EXPERTS, dtype=jnp.int32)
    counts = cnt[0, :N_EXPERTS].astype(jnp.int32)
    nblk = (counts + BM - 1) // BM
    blk_end = jnp.cumsum(nblk)
    pstart = (blk_end - nblk) * BM
    starts = jnp.cumsum(counts) - counts
    nused = blk_end[-1]
    blk = jnp.arange(NB, dtype=jnp.int32)
    block_expert = jnp.sum((blk[:, None] >= blk_end[None, :]).astype(jnp.int32), axis=1)
    last_expert = jnp.max(jnp.where(nblk > 0, experts, 0))
    block_expert = jnp.where(blk < nused, block_expert, last_expert).astype(jnp.int32)
    of_block = block_expert[:, None] == experts[None, :]
    base = blk * BM + jnp.sum(jnp.where(of_block, (starts - pstart)[None, :], 0), axis=1)
    limit = jnp.sum(jnp.where(of_block, (starts + counts - 1)[None, :], 0), axis=1)
    rows = rt.reshape(N_STEP, SUBLANES, TB)
    expert = jnp.transpose(rows[:, 2:2 + TOP_K, :], (1, 0, 2)).reshape(N_ASSIGN).astype(jnp.int32)
    rank = jnp.transpose(rows[:, 2 + TOP_K:2 + 2 * TOP_K, :], (1, 0, 2)).reshape(N_ASSIGN).astype(jnp.int32)
    slot = rank + jnp.sum(jnp.where(expert[:, None] == experts[None, :], pstart[None, :], 0), axis=1)
    order = jnp.argsort(expert * RANK_RANGE + rank).astype(jnp.int32)
    return (block_expert, nused.reshape(1).astype(jnp.int32), base.astype(jnp.int32),
            limit.astype(jnp.int32), order, slot.astype(jnp.int32))


def kernel(x_prompt, x_sample, state_pool, state_conv, meta, w_pool, pool_scale, w_glu, b_glu, w_dw, b_dw, conv_ln_g, conv_ln_b, w_pw, b_pw, ln_mix_g, ln_mix_b, ln_ffn_g, ln_ffn_b, w_router_group, w_router_expert, w_gate, w_up, w_down):
    f32 = jnp.float32
    x_samp2d = x_sample.reshape(DEC_BATCH, D)
    head = jnp.concatenate([jnp.tile(meta.astype(f32), (BATCH, 1)), x_samp2d], axis=0)
    k_idx = jnp.arange(POOL_STATE)[:, None]
    win = jnp.repeat(jnp.asarray(POOL_WINDOWS), POOL_GROUP_DIM)[None, :]
    pool_coef = (k_idx >= (POOL_STATE + 1 - win)).astype(f32)

    pool_t = jnp.transpose(state_pool, (0, 2, 1, 3))
    conv_t = jnp.transpose(state_conv, (0, 2, 1, 3))

    new_pool_p, new_pool_s, new_conv_p, new_conv_s = [], [], [], []
    for i in range(DEPTH):
        j = i // 2
        wr32 = jnp.concatenate([w_router_group[i], w_router_expert[i],
                                jnp.zeros((D, LANES - N_GROUPS - N_EXPERTS), f32)], axis=1)
        wr_hi = wr32.astype(jnp.bfloat16)
        wr = jnp.concatenate([wr_hi, (wr32 - wr_hi.astype(f32)).astype(jnp.bfloat16)], axis=1)
        ln_g, ln_b = ln_mix_g[i].reshape(1, D), ln_mix_b[i].reshape(1, D)
        if i == 0:
            inputs = dict(first_inputs=(head, x_prompt.reshape(BATCH * SEQ, D)))
        else:
            inputs = dict(fused_inputs=(slot, h1, r_all, ys, ln_ffn_g[i - 1].reshape(1, D),
                                        ln_ffn_b[i - 1].reshape(1, D)))
        if i % 2 == 0:
            ps = _state_presum(j, pool_t, pool_coef)
            outs = _pool_layer(ps, w_pool[j].astype(jnp.bfloat16), pool_scale[j].reshape(1, D),
                               ln_g, ln_b, wr, **inputs)
            if i == 0:
                tail, x_s = x_prompt[:, SEQ - POOL_STATE:], x_samp2d
            else:
                x_s, tail16, outs = outs[0], outs[1], outs[2:]
                tail = tail16[:, POOL_CARRY - POOL_STATE:]
            h1, r_all, cnt, h1p, rt = outs
            new_pool_p.append(tail)
            new_pool_s.append(x_s)
        else:
            vs = _state_presum(j, conv_t, w_dw[j][:CONV_STATE])
            outs = _conv_layer(
                vs, w_glu[j].astype(jnp.bfloat16), b_glu[j].reshape(1, 2 * D), w_dw[j],
                b_dw[j].reshape(1, D), conv_ln_g[j].reshape(1, D), conv_ln_b[j].reshape(1, D),
                w_pw[j].astype(jnp.bfloat16), b_pw[j].reshape(1, D), ln_g, ln_b, wr, **inputs)
            h1, r_all, cnt, h1p, rt, ust, u_s = outs if i == 0 else outs[2:]
            new_conv_p.append(ust)
            new_conv_s.append(u_s)
        block_expert, nused, base, limit, order, slot = _routing_metadata(rt, cnt)
        ys = _expert_layer(i, block_expert, nused, base, limit, order, h1p, w_gate, w_up, w_down)

    y_prompt, y_samp = _final_combine(slot, h1, r_all, ys, ln_ffn_g[DEPTH - 1].reshape(1, D),
                                      ln_ffn_b[DEPTH - 1].reshape(1, D))
    return (y_prompt.reshape(BATCH, SEQ, D), y_samp.reshape(DEC_BATCH, 1, D),
            jnp.stack(new_pool_p), jnp.transpose(_state_shift(pool_t, jnp.stack(new_pool_s)), (0, 2, 1, 3)),
            jnp.stack(new_conv_p), jnp.transpose(_state_shift(conv_t, jnp.stack(new_conv_s)), (0, 2, 1, 3)))
```

```python
import functools

import jax
import jax.numpy as jnp
from jax import lax
from jax.experimental import pallas as pl
from jax.experimental.pallas import tpu as pltpu

D = 1024
BATCH = 8
SEQ = 2048
DEPTH = 4
DEC_BATCH = 128
N_META = 16
POOL_WINDOWS = (2, 4, 8, 16)
POOL_GROUP_DIM = D // len(POOL_WINDOWS)
POOL_STATE = max(POOL_WINDOWS) - 1
CONV_WIDTH = 31
CONV_STATE = CONV_WIDTH - 1
N_GROUPS = 4
EXPERTS_PER_GROUP = 8
N_EXPERTS = N_GROUPS * EXPERTS_PER_GROUP
TOP_K = 2
D_EXPERT = D // 2
ALPHA = (2.0 * DEPTH) ** 0.25
LN_EPS = 1e-5

LANES = 128
SUBLANES = 8
N_HEAD = BATCH * N_META
TB = 256
NTOK = N_HEAD + DEC_BATCH + BATCH * SEQ
N_STEP = NTOK // TB
CHUNKS = SEQ // TB
POOL_CARRY = 16
POOL_LEAD = 8
CONV_CARRY = 32
CONV_RB = 128
CONV_CB = 128
BM = 512
N_ASSIGN = NTOK * TOP_K
RANK_RANGE = 1 << 16
NB = N_ASSIGN // BM + N_EXPERTS
N_SLOT = NB * BM
VMEM_LIMIT = 48 * 1024 * 1024
EXPERT_VMEM_LIMIT = 56 * 1024 * 1024
NEG = -1e30

assert D == SUBLANES * LANES and N_HEAD + DEC_BATCH == TB and SEQ % TB == 0
assert N_ASSIGN % BM == 0 and TB % CONV_RB == 0 and N_META == POOL_CARRY and CONV_STATE <= TB
assert RANK_RANGE > N_ASSIGN and N_ASSIGN * RANK_RANGE < 2 ** 32 and TOP_K == 2


def _tile_load(ref, n):
    return jnp.concatenate([ref[pl.ds(s, n, stride=SUBLANES), :] for s in range(SUBLANES)], axis=1)


def _tile_store(ref, val, n):
    for s in range(SUBLANES):
        ref[pl.ds(s, n, stride=SUBLANES), :] = val[:, s * LANES:(s + 1) * LANES]


def _layer_norm(x, g, b):
    mu = jnp.mean(x, axis=-1, keepdims=True)
    xc = x - mu
    var = jnp.mean(xc * xc, axis=-1, keepdims=True)
    return xc * lax.rsqrt(var + LN_EPS) * g + b


PACK_ROWS = D // 2 // LANES


def _pack_bf16(h_bf16, hp_ref):
    bits = pltpu.bitcast(h_bf16.astype(jnp.float32), jnp.uint32)
    words = jnp.bitwise_or(lax.shift_right_logical(bits[:, :D // 2], jnp.uint32(16)),
                           jnp.bitwise_and(bits[:, D // 2:], jnp.uint32(0xFFFF0000)))
    for s in range(PACK_ROWS):
        hp_ref[:, s, :] = words[:, s * LANES:(s + 1) * LANES]


def _unpack_bf16(words):
    lo = pltpu.bitcast(lax.shift_left(words, jnp.uint32(16)), jnp.float32)
    hi = pltpu.bitcast(jnp.bitwise_and(words, jnp.uint32(0xFFFF0000)), jnp.float32)
    return jnp.concatenate([lo, hi], axis=1).astype(jnp.bfloat16)


def _route_init(cnt_ref, ltri_ref):
    @pl.when(pl.program_id(0) == 0)
    def _():
        cnt_ref[...] = jnp.zeros((1, LANES), jnp.float32)
        row = lax.broadcasted_iota(jnp.int32, (TB, TB), 0)
        col = lax.broadcasted_iota(jnp.int32, (TB, TB), 1)
        ltri_ref[...] = jnp.where(col < row, 1.0, 0.0).astype(jnp.bfloat16)


def _residual_route(x, m, g_ref, b_ref, wr_ref, h_ref, r_ref, cnt_out_ref, hp_ref, rt_ref, cnt_ref, ltri_ref):
    h = _layer_norm(ALPHA * x + m, g_ref[...], b_ref[...])
    h_ref[...] = h
    h_hi = h.astype(jnp.bfloat16)
    _pack_bf16(h_hi, hp_ref)
    h_lo = (h - h_hi.astype(jnp.float32)).astype(jnp.bfloat16)
    hi_both = jnp.dot(h_hi, wr_ref[...], preferred_element_type=jnp.float32)
    lo_hi = jnp.dot(h_lo, wr_ref[:, 0:LANES], preferred_element_type=jnp.float32)
    logits = hi_both[:, 0:LANES] + (hi_both[:, LANES:2 * LANES] + lo_hi)
    lane = lax.broadcasted_iota(jnp.int32, logits.shape, 1)
    lanef = lane.astype(jnp.float32)
    big = jnp.float32(1e9)
    lg = jnp.where(lane < N_GROUPS, logits, NEG)
    mg = jnp.max(lg, axis=1, keepdims=True)
    gidx = jnp.min(jnp.where(lg == mg, lanef, big), axis=1, keepdims=True)
    p_grp = 1.0 / jnp.sum(jnp.where(lane < N_GROUPS, jnp.exp(lg - mg), 0.0), axis=1, keepdims=True)
    lo = N_GROUPS + EXPERTS_PER_GROUP * gidx
    le = jnp.where(lanef >= lo, jnp.where(lanef < lo + EXPERTS_PER_GROUP, logits, NEG), NEG)
    m1 = jnp.max(le, axis=1, keepdims=True)
    i1 = jnp.min(jnp.where(le == m1, lanef, big), axis=1, keepdims=True)
    le2 = jnp.where(lanef == i1, NEG, le)
    m2 = jnp.max(le2, axis=1, keepdims=True)
    i2 = jnp.min(jnp.where(le2 == m2, lanef, big), axis=1, keepdims=True)
    ratio = jnp.exp(m2 - m1)
    g1 = p_grp / (1.0 + ratio)
    g2 = g1 * ratio
    e1 = i1 - N_GROUPS
    e2 = i2 - N_GROUPS

    is1 = lanef == e1
    is2 = lanef == e2
    onehot = jnp.where(is1, 1.0, jnp.where(is2, 1.0, 0.0))
    before = jnp.dot(ltri_ref[...], onehot.astype(jnp.bfloat16),
                     preferred_element_type=jnp.float32) + cnt_ref[...]
    rank1 = jnp.sum(jnp.where(is1, before, 0.0), axis=1, keepdims=True)
    rank2 = jnp.sum(jnp.where(is2, before, 0.0), axis=1, keepdims=True)
    cnt = cnt_ref[...] + jnp.sum(onehot, axis=0, keepdims=True)
    cnt_ref[...] = cnt
    cnt_out_ref[...] = cnt

    packed = jnp.where(lane == 0, g1,
                       jnp.where(lane == 1, g2,
                                 jnp.where(lane == 2, e1,
                                           jnp.where(lane == 3, e2,
                                                     jnp.where(lane == 4, rank1,
                                                               jnp.where(lane == 5, rank2, 0.0))))))
    r_ref[...] = packed
    rt_ref[...] = jnp.transpose(packed)[0:SUBLANES, :]


def _seq_of_step(s):
    sm1 = jnp.maximum(s - 1, 0)
    return lax.shift_right_logical(sm1, CHUNKS.bit_length() - 1), jnp.bitwise_and(sm1, CHUNKS - 1)


assert CHUNKS & (CHUNKS - 1) == 0


PRESUM_ROWS = 32


def _presum_kernel(st_ref, coef_ref, o_ref):
    k = st_ref.shape[1]
    acc = coef_ref[0:1, :] * st_ref[0, 0]
    for t in range(1, k):
        acc = acc + coef_ref[t:t + 1, :] * st_ref[0, t]
    o_ref[...] = acc


def _state_presum(j, state_t, coef):
    _, k, n, _ = state_t.shape
    return pl.pallas_call(
        _presum_kernel,
        grid=(n // PRESUM_ROWS,),
        in_specs=[pl.BlockSpec((1, k, PRESUM_ROWS, D), lambda i: (j, 0, i, 0)),
                  pl.BlockSpec((k, D), lambda i: (0, 0))],
        out_specs=pl.BlockSpec((PRESUM_ROWS, D), lambda i: (i, 0)),
        out_shape=jax.ShapeDtypeStruct((n, D), jnp.float32),
        compiler_params=pltpu.CompilerParams(dimension_semantics=("arbitrary",),
                                             vmem_limit_bytes=VMEM_LIMIT),
        name="state_presum",
    )(state_t, coef)


def _state_shift_kernel(st_ref, new_ref, o_ref):
    k = st_ref.shape[1]
    for t in range(k - 1):
        o_ref[0, t] = st_ref[0, t + 1]
    o_ref[0, k - 1] = new_ref[0]


def _state_shift(state_t, new_rows):
    nj, k, n, _ = state_t.shape
    return pl.pallas_call(
        _state_shift_kernel,
        grid=(nj, n // PRESUM_ROWS),
        in_specs=[pl.BlockSpec((1, k, PRESUM_ROWS, D), lambda j, i: (j, 0, i, 0)),
                  pl.BlockSpec((1, PRESUM_ROWS, D), lambda j, i: (j, i, 0))],
        out_specs=pl.BlockSpec((1, k, PRESUM_ROWS, D), lambda j, i: (j, 0, i, 0)),
        out_shape=jax.ShapeDtypeStruct(state_t.shape, jnp.float32),
        compiler_params=pltpu.CompilerParams(dimension_semantics=("arbitrary", "arbitrary"),
                                             vmem_limit_bytes=VMEM_LIMIT),
        name="state_shift",
    )(state_t, new_rows)


POOL_X0 = POOL_LEAD + POOL_CARRY


def _window_sums(ext_ref, lvl2_ref, lvl4_ref, lvl8_ref):
    n = POOL_CARRY + TB
    gd = POOL_GROUP_DIM

    def doubled(src_ref, shift, c0):
        return src_ref[POOL_LEAD:POOL_LEAD + n, c0:] + src_ref[POOL_LEAD - shift:POOL_LEAD - shift + n, c0:]

    lvl2_ref[POOL_LEAD:POOL_LEAD + n, :] = doubled(ext_ref, 1, 0)
    lvl4_ref[POOL_LEAD:POOL_LEAD + n, :] = doubled(lvl2_ref, 2, gd)
    lvl8_ref[POOL_LEAD:POOL_LEAD + n, :] = doubled(lvl4_ref, 4, gd)
    s16 = (lvl8_ref[POOL_X0:POOL_X0 + TB, gd:2 * gd] +
           lvl8_ref[POOL_X0 - 8:POOL_X0 - 8 + TB, gd:2 * gd])
    return (lvl2_ref[POOL_X0:POOL_X0 + TB, 0:gd], lvl4_ref[POOL_X0:POOL_X0 + TB, 0:gd],
            lvl8_ref[POOL_X0:POOL_X0 + TB, 0:gd], s16)


assert POOL_WINDOWS == (2, 4, 8, 16) and POOL_LEAD >= 4 and POOL_LEAD % SUBLANES == 0


def _pool_kernel(first_layer, head_ref, x_ref, ps_ref, wp_ref, sc_ref, g_ref, b_ref, wr_ref,
                 h_ref, r_ref, cnt_out_ref, hp_ref, rt_ref, ext_ref, m_ref, carry_ref, cnt_ref, ltri_ref,
                 lvl2_ref, lvl4_ref, lvl8_ref, start_next=lambda: None):
    s = pl.program_id(0)
    seq, chunk = _seq_of_step(s)
    _route_init(cnt_ref, ltri_ref)
    route_refs = (g_ref, b_ref, wr_ref, h_ref, r_ref, cnt_out_ref, hp_ref, rt_ref, cnt_ref, ltri_ref)

    @pl.when(s == 0)
    def _():
        ext_ref[0:POOL_LEAD, :] = jnp.zeros((POOL_LEAD, D), jnp.float32)
        lvl2_ref[0:POOL_LEAD, :] = jnp.zeros((POOL_LEAD, D), jnp.float32)
        lvl4_ref[0:POOL_LEAD, :] = jnp.zeros((POOL_LEAD, D - POOL_GROUP_DIM), jnp.float32)
        x = head_ref[...] if first_layer else x_ref[...]
        start_next()
        ext_ref[POOL_LEAD:POOL_X0, :] = jnp.zeros((POOL_CARRY, D), jnp.float32)
        ext_ref[POOL_X0:POOL_X0 + TB, :] = x
        carry_ref[...] = x[0:N_HEAD, :]
        pos = jnp.bitwise_and(lax.broadcasted_iota(jnp.int32, (N_HEAD, 1), 0), N_META - 1)
        for g, w in enumerate(POOL_WINDOWS):
            lo, hi = g * POOL_GROUP_DIM, (g + 1) * POOL_GROUP_DIM
            xm = x[0:N_HEAD, lo:hi]
            acc = xm
            for k in range(1, w):
                acc = acc + jnp.where(pos >= k, ext_ref[POOL_X0 - k:POOL_X0 - k + N_HEAD, lo:hi], 0.0)
            cnt = jnp.minimum(pos + 1, w).astype(jnp.float32)
            d_meta = acc / cnt - xm
            xs = x[N_HEAD:TB, lo:hi]
            d_samp = (ps_ref[:, lo:hi] + xs) / float(w) - xs
            diff = jnp.concatenate([d_meta, d_samp], axis=0)
            y = jnp.dot(diff.astype(jnp.bfloat16), wp_ref[g], preferred_element_type=jnp.float32)
            m_ref[:, lo:hi] = y * sc_ref[:, lo:hi]
        _residual_route(x, m_ref[...], *route_refs)

    @pl.when(s > 0)
    def _():
        @pl.when(chunk == 0)
        def _():
            ext_ref[POOL_LEAD:POOL_X0, :] = carry_ref[pl.ds(pl.multiple_of(seq * N_META, N_META), N_META), :]

        x = x_ref[...]
        start_next()
        ext_ref[POOL_X0:POOL_X0 + TB, :] = x
        sums = _window_sums(ext_ref, lvl2_ref, lvl4_ref, lvl8_ref)
        ms = []
        for g, w in enumerate(POOL_WINDOWS):
            lo, hi = g * POOL_GROUP_DIM, (g + 1) * POOL_GROUP_DIM
            diff = sums[g] / float(w) - x[:, lo:hi]
            y = jnp.dot(diff.astype(jnp.bfloat16), wp_ref[g], preferred_element_type=jnp.float32)
            ms.append(y * sc_ref[:, lo:hi])
        _residual_route(x, jnp.concatenate(ms, axis=1), *route_refs)
        ext_ref[POOL_LEAD:POOL_X0, :] = ext_ref[POOL_LEAD + TB:POOL_X0 + TB, :]


assert N_META >= max(POOL_WINDOWS)


def _const_spec(shape):
    nd = len(shape)
    return pl.BlockSpec(shape, lambda s, *_: (0,) * nd)


_MIXER_OUT_SPECS = [pl.BlockSpec((TB, D), lambda s, *_: (s, 0)),
                    pl.BlockSpec((TB, LANES), lambda s, *_: (s, 0)),
                    pl.BlockSpec((1, LANES), lambda s, *_: (0, 0)),
                    pl.BlockSpec((TB, PACK_ROWS, LANES), lambda s, *_: (s, 0, 0)),
                    pl.BlockSpec((SUBLANES, TB), lambda s, *_: (s, 0))]
_MIXER_OUT_SHAPES = [jax.ShapeDtypeStruct((NTOK, D), jnp.float32),
                     jax.ShapeDtypeStruct((NTOK, LANES), jnp.float32),
                     jax.ShapeDtypeStruct((1, LANES), jnp.float32),
                     jax.ShapeDtypeStruct((NTOK, PACK_ROWS, LANES), jnp.uint32),
                     jax.ShapeDtypeStruct((N_STEP * SUBLANES, TB), jnp.float32)]


def _combine_rows(slot_ref, h_ref, r_ref, ys_hbm, g_ref, b_ref, ybuf, sem):
    i = pl.program_id(0)
    n = pl.num_programs(0)

    def start(step, buf):
        base = step * TB
        _tile_gather_start(ys_hbm, ybuf.at[buf], sem.at[buf],
                           lambda r: slot_ref[(r % TOP_K) * NTOK + base + r // TOP_K], TOP_K * TB,
                           dst_tile=lambda r: (r % TOP_K) * TB + r // TOP_K)

    @pl.when(i == 0)
    def _():
        start(0, 0)

    buf = i % 2
    _tile_gather_wait(ys_hbm, ybuf.at[buf], sem.at[buf], TOP_K * TB)
    yb = ybuf.at[buf]
    y0 = _tile_load(yb.at[pl.ds(0, TB * SUBLANES), :], TB)
    y1 = _tile_load(yb.at[pl.ds(TB * SUBLANES, TB * SUBLANES), :], TB)
    r = r_ref[...]
    rows = _layer_norm(ALPHA * h_ref[...] + (r[:, 0:1] * y0 + r[:, 1:2] * y1), g_ref[...], b_ref[...])

    def drain():
        @pl.when(i == n - 1)
        def _():
            _tile_gather_wait(ys_hbm, ybuf.at[1 - buf], sem.at[1 - buf], TOP_K * TB)

    return rows, lambda: start(jnp.minimum(i + 1, n - 1), 1 - buf), drain


_COMBINE_SCRATCH = [pltpu.VMEM((2, TOP_K * TB * SUBLANES, LANES), jnp.float32),
                    pltpu.SemaphoreType.DMA((2,))]


def _fused_mixer_kernel(body, n_in, slot_ref, hprev_ref, rprev_ref, ys_hbm, fg_ref, fb_ref, *rest):
    mixer_in = rest[:n_in]
    xs_out_ref, tail_ref = rest[n_in:n_in + 2]
    x_scr, ybuf, sem = rest[-3:]
    i = pl.program_id(0)
    x, start_next, drain = _combine_rows(slot_ref, hprev_ref, rprev_ref, ys_hbm, fg_ref, fb_ref, ybuf, sem)
    x_scr[...] = x

    @pl.when(i == 0)
    def _():
        xs_out_ref[...] = x[N_HEAD:TB, :]

    _, chunk = _seq_of_step(i)

    @pl.when(jnp.logical_and(i > 0, chunk == CHUNKS - 1))
    def _():
        tail_ref[0] = x[TB - POOL_CARRY:TB, :]

    body(False, None, x_scr, *mixer_in, *rest[n_in + 2:-3], start_next=start_next)
    drain()


def _mixer_call(name, body, mixer_in_specs, extra_out_specs, extra_out_shapes, scratch_shapes, mixer_inputs,
                first_inputs=None, fused_inputs=None):
    out_specs = _MIXER_OUT_SPECS + extra_out_specs
    out_shape = _MIXER_OUT_SHAPES + extra_out_shapes
    params = pltpu.CompilerParams(dimension_semantics=("arbitrary",), vmem_limit_bytes=VMEM_LIMIT)
    if fused_inputs is None:
        return pl.pallas_call(
            functools.partial(body, True),
            grid=(N_STEP,),
            in_specs=[_const_spec((TB, D)),
                      pl.BlockSpec((TB, D), lambda s, *_: (jnp.maximum(s - 1, 0), 0))] + mixer_in_specs,
            out_specs=out_specs, out_shape=out_shape, scratch_shapes=scratch_shapes,
            compiler_params=params, name=name,
        )(*first_inputs, *mixer_inputs)
    seq_block = lambda s, *_: (jnp.maximum(s - 1, 0) // CHUNKS, 0, 0)
    grid_spec = pltpu.PrefetchScalarGridSpec(
        num_scalar_prefetch=1,
        grid=(N_STEP,),
        in_specs=[pl.BlockSpec((TB, D), lambda s, *_: (s, 0)),
                  pl.BlockSpec((TB, LANES), lambda s, *_: (s, 0)),
                  pl.BlockSpec(memory_space=pl.ANY),
                  _const_spec((1, D)), _const_spec((1, D))] + mixer_in_specs,
        out_specs=[_const_spec((DEC_BATCH, D)), pl.BlockSpec((1, POOL_CARRY, D), seq_block)] + out_specs,
        scratch_shapes=scratch_shapes + [pltpu.VMEM((TB, D), jnp.float32)] + _COMBINE_SCRATCH,
    )
    return pl.pallas_call(
        functools.partial(_fused_mixer_kernel, body, len(mixer_in_specs)),
        grid_spec=grid_spec,
        out_shape=[jax.ShapeDtypeStruct((DEC_BATCH, D), jnp.float32),
                   jax.ShapeDtypeStruct((BATCH, POOL_CARRY, D), jnp.float32)] + out_shape,
        compiler_params=params, name=name + "_fused",
    )(*fused_inputs, *mixer_inputs)


def _pool_layer(ps, wp_bf, scale, ln_g, ln_b, wr, **inputs):
    ext_rows = POOL_X0 + TB
    return _mixer_call(
        "pool_mixer", _pool_kernel,
        [_const_spec((DEC_BATCH, D)),
         _const_spec((len(POOL_WINDOWS), POOL_GROUP_DIM, POOL_GROUP_DIM)),
         _const_spec((1, D)), _const_spec((1, D)), _const_spec((1, D)),
         _const_spec((D, 2 * LANES))],
        [], [],
        [pltpu.VMEM((ext_rows, D), jnp.float32),
         pltpu.VMEM((TB, D), jnp.float32),
         pltpu.VMEM((N_HEAD, D), jnp.float32),
         pltpu.VMEM((1, LANES), jnp.float32),
         pltpu.VMEM((TB, TB), jnp.bfloat16),
         pltpu.VMEM((ext_rows, D), jnp.float32),
         pltpu.VMEM((ext_rows, D - POOL_GROUP_DIM), jnp.float32),
         pltpu.VMEM((ext_rows, D - 2 * POOL_GROUP_DIM), jnp.float32)],
        (ps, wp_bf, scale, ln_g, ln_b, wr), **inputs)


def _depthwise_conv(ext_ref, wdw_ref, v_ref):
    base = CONV_CARRY - CONV_STATE
    for cb in range(D // CONV_CB):
        lo, hi = cb * CONV_CB, (cb + 1) * CONV_CB

        def body(i, carry, lo=lo, hi=hi):
            r0 = pl.multiple_of(i * CONV_RB, CONV_RB)
            sub = ext_ref.at[pl.ds(r0, CONV_RB + CONV_CARRY), :]
            v = None
            for r in range(SUBLANES):
                qs = [q for q in range((base + CONV_WIDTH) // SUBLANES + 1)
                      if 0 <= SUBLANES * q + r - base < CONV_WIDTH]
                z0 = SUBLANES * qs[0] + r
                z = sub[z0:SUBLANES * qs[-1] + r + CONV_RB, lo:hi]
                p = None
                for q in qs:
                    k = SUBLANES * q + r - base
                    off = SUBLANES * (q - qs[0])
                    term = wdw_ref[k:k + 1, lo:hi] * z[off:off + CONV_RB]
                    p = term if p is None else p + term
                v = p if v is None else v + p
            v_ref[pl.ds(r0, CONV_RB), lo:hi] = v
            return carry
        lax.fori_loop(0, TB // CONV_RB, body, 0)


def _conv_kernel(first_layer, head_ref, x_ref, vs_ref, wglu_ref, bglu_ref, wdw_ref, bdw_ref, lg_ref, lb_ref,
                 wpw_ref, bpw_ref, g_ref, b_ref, wr_ref, h_ref, r_ref, cnt_out_ref, hp_ref, rt_ref, ust_ref, us_ref,
                 ext_ref, v_ref, carry_ref, cnt_ref, ltri_ref, start_next=lambda: None):
    s = pl.program_id(0)
    seq, chunk = _seq_of_step(s)
    _route_init(cnt_ref, ltri_ref)
    route_refs = (g_ref, b_ref, wr_ref, h_ref, r_ref, cnt_out_ref, hp_ref, rt_ref, cnt_ref, ltri_ref)

    def glu(x):
        hh = jnp.dot(x.astype(jnp.bfloat16), wglu_ref[...], preferred_element_type=jnp.float32) + bglu_ref[...]
        return hh[:, :D] * jax.nn.sigmoid(hh[:, D:])

    def finish(x, v):
        v = _layer_norm(v + bdw_ref[...], lg_ref[...], lb_ref[...])
        v = v * jax.nn.sigmoid(v)
        m = jnp.dot(v.astype(jnp.bfloat16), wpw_ref[...], preferred_element_type=jnp.float32) + bpw_ref[...]
        _residual_route(x, m, *route_refs)

    @pl.when(s == 0)
    def _():
        x = head_ref[...] if first_layer else x_ref[...]
        start_next()
        u = glu(x)
        um = u[0:N_HEAD, :]
        us = u[N_HEAD:TB, :]
        carry_ref[...] = um
        us_ref[...] = us
        ext_ref[0:N_META, :] = jnp.zeros((N_META, D), jnp.float32)
        ext_ref[N_META:N_META + N_HEAD, :] = um
        pos = jnp.bitwise_and(lax.broadcasted_iota(jnp.int32, (N_HEAD, 1), 0), N_META - 1)
        acc = wdw_ref[CONV_WIDTH - 1:CONV_WIDTH, :] * um
        for d in range(1, N_META):
            k = CONV_WIDTH - 1 - d
            acc = acc + wdw_ref[k:k + 1, :] * jnp.where(pos >= d, ext_ref[N_META - d:N_META - d + N_HEAD, :], 0.0)
        v_samp = vs_ref[...] + wdw_ref[CONV_WIDTH - 1:CONV_WIDTH, :] * us
        finish(x, jnp.concatenate([acc, v_samp], axis=0))

    @pl.when(s > 0)
    def _():
        @pl.when(chunk == 0)
        def _():
            ext_ref[0:CONV_CARRY - N_META, :] = jnp.zeros((CONV_CARRY - N_META, D), jnp.float32)
            ext_ref[CONV_CARRY - N_META:CONV_CARRY, :] = carry_ref[
                pl.ds(pl.multiple_of(seq * N_META, N_META), N_META), :]

        x = x_ref[...]
        start_next()
        ext_ref[CONV_CARRY:CONV_CARRY + TB, :] = glu(x)
        _depthwise_conv(ext_ref, wdw_ref, v_ref)
        finish(x, v_ref[...])
        ext_ref[0:CONV_CARRY, :] = ext_ref[TB:TB + CONV_CARRY, :]

        @pl.when(chunk == CHUNKS - 1)
        def _():
            ust_ref[0] = ext_ref[CONV_CARRY + TB - CONV_STATE:CONV_CARRY + TB, :]


assert N_META <= CONV_STATE <= CONV_CARRY


def _conv_layer(vs, wglu_bf, bglu, wdw, bdw, cln_g, cln_b, wpw_bf, bpw, ln_g, ln_b, wr, **inputs):
    return _mixer_call(
        "conv_mixer", _conv_kernel,
        [_const_spec((DEC_BATCH, D)),
         _const_spec((D, 2 * D)), _const_spec((1, 2 * D)),
         _const_spec((CONV_WIDTH, D)), _const_spec((1, D)),
         _const_spec((1, D)), _const_spec((1, D)),
         _const_spec((D, D)), _const_spec((1, D)),
         _const_spec((1, D)), _const_spec((1, D)),
         _const_spec((D, 2 * LANES))],
        [pl.BlockSpec((1, CONV_STATE, D), lambda s, *_: (jnp.maximum(s - 1, 0) // CHUNKS, 0, 0)),
         _const_spec((DEC_BATCH, D))],
        [jax.ShapeDtypeStruct((BATCH, CONV_STATE, D), jnp.float32),
         jax.ShapeDtypeStruct((DEC_BATCH, D), jnp.float32)],
        [pltpu.VMEM((CONV_CARRY + TB, D), jnp.float32),
         pltpu.VMEM((TB, D), jnp.float32),
         pltpu.VMEM((N_HEAD, D), jnp.float32),
         pltpu.VMEM((1, LANES), jnp.float32),
         pltpu.VMEM((TB, TB), jnp.bfloat16)],
        (vs, wglu_bf, bglu, wdw, bdw, cln_g, cln_b, wpw_bf, bpw, ln_g, ln_b, wr), **inputs)


def _tile_gather_start(src_hbm, dst, sem, row_index, n_rows, dst_tile=lambda r: r):
    for r in range(n_rows):
        tok = row_index(r)
        pltpu.make_async_copy(src_hbm.at[pl.ds(pl.multiple_of(tok * SUBLANES, SUBLANES), SUBLANES), :],
                              dst.at[pl.ds(dst_tile(r) * SUBLANES, SUBLANES), :], sem).start(priority=r % 2)


def _tile_gather_wait(src_hbm, dst, sem, n_rows):
    pltpu.make_async_copy(src_hbm.at[pl.ds(0, n_rows * SUBLANES), :], dst, sem).wait()


def _expert_kernel(be_ref, nused_ref, base_ref, limit_ref, order_ref, hp_ref, wg_ref, wu_ref, wd_ref, ys_ref,
                   xbuf_even, xbuf_odd, wgb, wub, wdb):
    i = pl.program_id(0)
    nused = nused_ref[0]

    def gather(blk, xbuf):
        b0 = base_ref[blk]
        lim = limit_ref[blk]
        for r in range(BM):
            a = order_ref[jnp.minimum(b0 + r, lim)]
            xbuf[r] = hp_ref[jnp.where(a >= NTOK, a - NTOK, a)]

    def block(cur, nxt):
        gather(jnp.minimum(i + 1, jnp.maximum(nused - 1, 0)), nxt)
        words = jnp.concatenate([cur[:, s, :] for s in range(PACK_ROWS)], axis=1)
        xb = _unpack_bf16(words)
        gate = jnp.dot(xb, wgb[...], preferred_element_type=jnp.float32)
        up = jnp.dot(xb, wub[...], preferred_element_type=jnp.float32)
        hid = gate * jax.nn.sigmoid(gate) * up
        y = jnp.dot(hid.astype(jnp.bfloat16), wdb[...], preferred_element_type=jnp.float32)
        _tile_store(ys_ref, y, BM)

    @pl.when(i == 0)
    def _():
        gather(0, xbuf_even)

    @pl.when(jnp.logical_and(i < nused, jnp.logical_or(i == 0, be_ref[i] != be_ref[jnp.maximum(i - 1, 0)])))
    def _():
        wgb[...] = wg_ref[0, 0].astype(jnp.bfloat16)
        wub[...] = wu_ref[0, 0].astype(jnp.bfloat16)
        wdb[...] = wd_ref[0, 0].astype(jnp.bfloat16)

    odd = jnp.bitwise_and(i, 1) == 1

    @pl.when(jnp.logical_and(i < nused, jnp.logical_not(odd)))
    def _():
        block(xbuf_even, xbuf_odd)

    @pl.when(jnp.logical_and(i < nused, odd))
    def _():
        block(xbuf_odd, xbuf_even)

    @pl.when(i >= nused)
    def _():
        ys_ref[...] = jnp.zeros((BM * SUBLANES, LANES), jnp.float32)


def _expert_layer(layer, block_expert, nused, base, limit, order, h_packed, w_gate, w_up, w_down):
    def w_spec(shape):
        return pl.BlockSpec((1, 1) + shape, lambda i, be, *_: (layer, be[i], 0, 0))
    grid_spec = pltpu.PrefetchScalarGridSpec(
        num_scalar_prefetch=5,
        grid=(NB,),
        in_specs=[pl.BlockSpec((NTOK, PACK_ROWS, LANES), lambda i, *_: (0, 0, 0)),
                  w_spec((D, D_EXPERT)), w_spec((D, D_EXPERT)), w_spec((D_EXPERT, D))],
        out_specs=pl.BlockSpec((BM * SUBLANES, LANES), lambda i, *_: (i, 0)),
        scratch_shapes=[pltpu.VMEM((BM, PACK_ROWS, LANES), jnp.uint32),
                        pltpu.VMEM((BM, PACK_ROWS, LANES), jnp.uint32),
                        pltpu.VMEM((D, D_EXPERT), jnp.bfloat16),
                        pltpu.VMEM((D, D_EXPERT), jnp.bfloat16),
                        pltpu.VMEM((D_EXPERT, D), jnp.bfloat16)],
    )
    return pl.pallas_call(
        _expert_kernel,
        grid_spec=grid_spec,
        out_shape=jax.ShapeDtypeStruct((N_SLOT * SUBLANES, LANES), jnp.float32),
        compiler_params=pltpu.CompilerParams(dimension_semantics=("arbitrary",),
                                             vmem_limit_bytes=EXPERT_VMEM_LIMIT),
        name="expert_mlp",
    )(block_expert, nused, base, limit, order, h_packed, w_gate, w_up, w_down)


def _final_combine_kernel(slot_ref, h_ref, r_ref, ys_hbm, g_ref, b_ref, yp_ref, ysamp_ref, ybuf, sem):
    out, start_next, drain = _combine_rows(slot_ref, h_ref, r_ref, ys_hbm, g_ref, b_ref, ybuf, sem)
    start_next()
    yp_ref[...] = out

    @pl.when(pl.program_id(0) == 0)
    def _():
        ysamp_ref[...] = out[N_HEAD:TB, :]

    drain()


def _final_combine(slot, h, r_all, ys, ln_g, ln_b):
    grid_spec = pltpu.PrefetchScalarGridSpec(
        num_scalar_prefetch=1,
        grid=(N_STEP,),
        in_specs=[pl.BlockSpec((TB, D), lambda i, sl: (i, 0)),
                  pl.BlockSpec((TB, LANES), lambda i, sl: (i, 0)),
                  pl.BlockSpec(memory_space=pl.ANY),
                  pl.BlockSpec((1, D), lambda i, sl: (0, 0)),
                  pl.BlockSpec((1, D), lambda i, sl: (0, 0))],
        out_specs=[pl.BlockSpec((TB, D), lambda i, sl: (jnp.maximum(i - 1, 0), 0)),
                   pl.BlockSpec((DEC_BATCH, D), lambda i, sl: (0, 0))],
        scratch_shapes=_COMBINE_SCRATCH,
    )
    return pl.pallas_call(
        _final_combine_kernel,
        grid_spec=grid_spec,
        out_shape=[jax.ShapeDtypeStruct((BATCH * SEQ, D), jnp.float32),
                   jax.ShapeDtypeStruct((DEC_BATCH, D), jnp.float32)],
        compiler_params=pltpu.CompilerParams(dimension_semantics=("arbitrary",),
                                             vmem_limit_bytes=VMEM_LIMIT),
        name="moe_combine",
    )(slot, h, r_all, ys, ln_g, ln_b)


def _routing_metadata(rt, cnt):
    experts = jnp.arange(N_EXPERTS, dtype=jnp.int32)
    counts = cnt[0, :N_EXPERTS].astype(jnp.int32)
    nblk = (counts + BM - 1) // BM
    blk_end = jnp.cumsum(nblk)
    pstart = (blk_end - nblk) * BM
    starts = jnp.cumsum(counts) - counts
    nused = blk_end[-1]
    blk = jnp.arange(NB, dtype=jnp.int32)
    block_expert = jnp.sum((blk[:, None] >= blk_end[None, :]).astype(jnp.int32), axis=1)
    last_expert = jnp.max(jnp.where(nblk > 0, experts, 0))
    block_expert = jnp.where(blk < nused, block_expert, last_expert).astype(jnp.int32)
    of_block = block_expert[:, None] == experts[None, :]
    base = blk * BM + jnp.sum(jnp.where(of_block, (starts - pstart)[None, :], 0), axis=1)
    limit = jnp.sum(jnp.where(of_block, (starts + counts - 1)[None, :], 0), axis=1)
    rows = rt.reshape(N_STEP, SUBLANES, TB)
    expert = jnp.transpose(rows[:, 2:2 + TOP_K, :], (1, 0, 2)).reshape(N_ASSIGN).astype(jnp.int32)
    rank = jnp.transpose(rows[:, 2 + TOP_K:2 + 2 * TOP_K, :], (1, 0, 2)).reshape(N_ASSIGN).astype(jnp.int32)
    is_expert = expert[:, None] == experts[None, :]
    slot = rank + jnp.sum(jnp.where(is_expert, pstart[None, :], 0), axis=1)
    position = rank + jnp.sum(jnp.where(is_expert, starts[None, :], 0), axis=1)
    packed = position.astype(jnp.uint32) * jnp.uint32(RANK_RANGE) + jnp.arange(N_ASSIGN, dtype=jnp.uint32)
    order = jnp.bitwise_and(jnp.sort(packed), jnp.uint32(RANK_RANGE - 1)).astype(jnp.int32)
    return (block_expert, nused.reshape(1).astype(jnp.int32), base.astype(jnp.int32),
            limit.astype(jnp.int32), order, slot.astype(jnp.int32))


def kernel(x_prompt, x_sample, state_pool, state_conv, meta, w_pool, pool_scale, w_glu, b_glu, w_dw, b_dw, conv_ln_g, conv_ln_b, w_pw, b_pw, ln_mix_g, ln_mix_b, ln_ffn_g, ln_ffn_b, w_router_group, w_router_expert, w_gate, w_up, w_down):
    f32 = jnp.float32
    x_samp2d = x_sample.reshape(DEC_BATCH, D)
    head = jnp.concatenate([jnp.tile(meta.astype(f32), (BATCH, 1)), x_samp2d], axis=0)
    k_idx = jnp.arange(POOL_STATE)[:, None]
    win = jnp.repeat(jnp.asarray(POOL_WINDOWS), POOL_GROUP_DIM)[None, :]
    pool_coef = (k_idx >= (POOL_STATE + 1 - win)).astype(f32)

    pool_t = jnp.transpose(state_pool, (0, 2, 1, 3))
    conv_t = jnp.transpose(state_conv, (0, 2, 1, 3))

    new_pool_p, new_pool_s, new_conv_p, new_conv_s = [], [], [], []
    for i in range(DEPTH):
        j = i // 2
        wr32 = jnp.concatenate([w_router_group[i], w_router_expert[i],
                                jnp.zeros((D, LANES - N_GROUPS - N_EXPERTS), f32)], axis=1)
        wr_hi = wr32.astype(jnp.bfloat16)
        wr = jnp.concatenate([wr_hi, (wr32 - wr_hi.astype(f32)).astype(jnp.bfloat16)], axis=1)
        ln_g, ln_b = ln_mix_g[i].reshape(1, D), ln_mix_b[i].reshape(1, D)
        if i == 0:
            inputs = dict(first_inputs=(head, x_prompt.reshape(BATCH * SEQ, D)))
        else:
            inputs = dict(fused_inputs=(slot, h1, r_all, ys, ln_ffn_g[i - 1].reshape(1, D),
                                        ln_ffn_b[i - 1].reshape(1, D)))
        if i % 2 == 0:
            ps = _state_presum(j, pool_t, pool_coef)
            outs = _pool_layer(ps, w_pool[j].astype(jnp.bfloat16), pool_scale[j].reshape(1, D),
                               ln_g, ln_b, wr, **inputs)
            if i == 0:
                tail, x_s = x_prompt[:, SEQ - POOL_STATE:], x_samp2d
            else:
                x_s, tail16, outs = outs[0], outs[1], outs[2:]
                tail = tail16[:, POOL_CARRY - POOL_STATE:]
            h1, r_all, cnt, h1p, rt = outs
            new_pool_p.append(tail)
            new_pool_s.append(x_s)
        else:
            vs = _state_presum(j, conv_t, w_dw[j][:CONV_STATE])
            outs = _conv_layer(
                vs, w_glu[j].astype(jnp.bfloat16), b_glu[j].reshape(1, 2 * D), w_dw[j],
                b_dw[j].reshape(1, D), conv_ln_g[j].reshape(1, D), conv_ln_b[j].reshape(1, D),
                w_pw[j].astype(jnp.bfloat16), b_pw[j].reshape(1, D), ln_g, ln_b, wr, **inputs)
            h1, r_all, cnt, h1p, rt, ust, u_s = outs if i == 0 else outs[2:]
            new_conv_p.append(ust)
            new_conv_s.append(u_s)
        block_expert, nused, base, limit, order, slot = _routing_metadata(rt, cnt)
        ys = _expert_layer(i, block_expert, nused, base, limit, order, h1p, w_gate, w_up, w_down)

    y_prompt, y_samp = _final_combine(slot, h1, r_all, ys, ln_ffn_g[DEPTH - 1].reshape(1, D),
                                      ln_ffn_b[DEPTH - 1].reshape(1, D))
    return (y_prompt.reshape(BATCH, SEQ, D), y_samp.reshape(DEC_BATCH, 1, D),
            jnp.stack(new_pool_p), jnp.transpose(_state_shift(pool_t, jnp.stack(new_pool_s)), (0, 2, 1, 3)),
            jnp.stack(new_conv_p), jnp.transpose(_state_shift(conv_t, jnp.stack(new_conv_s)), (0, 2, 1, 3)))
```
